```python
import math
import jax, jax.numpy as jnp
from jax import lax
import numpy as np

D_MODEL = 1024
BATCH = 4
SEQ = 4096
DEPTH = 2

N_AB_LAYERS = (DEPTH + 1) // 2
N_C_LAYERS = DEPTH // 2

RG_WIDTH = D_MODEL
RG_BLOCKS = 8
RG_BLOCK_DIM = RG_WIDTH // RG_BLOCKS
RG_CONV = 4
RG_C = 8.0

HG_WIDTH = D_MODEL
HG_HEAD_DIM = 128
HG_HEADS = HG_WIDTH // HG_HEAD_DIM
HG_CHUNK = 64

AB_IN_WIDTH = 2 * RG_WIDTH + 4 * HG_WIDTH
AB_MIX_WIDTH = RG_WIDTH + HG_WIDTH
AB_SPLITS = [RG_WIDTH, 2 * RG_WIDTH, 2 * RG_WIDTH + HG_WIDTH,
             2 * RG_WIDTH + 2 * HG_WIDTH, 2 * RG_WIDTH + 3 * HG_WIDTH]

RW_WIDTH = 2 * D_MODEL
RW_HEAD_DIM = 64
RW_HEADS = RW_WIDTH // RW_HEAD_DIM
RW_LORA = 64

RMS_EPS = 1e-6
GN_EPS = 64e-5

kernel_name = "hybrid_rglru_hgrn2_rwkv7_trunk"


def rms_norm(x, g):
    x32 = x.astype(jnp.float32)
    y = x32 * lax.rsqrt(jnp.mean(x32 * x32, axis=-1, keepdims=True) + RMS_EPS)
    return (y * g.astype(jnp.float32)).astype(x.dtype)


def causal_depthwise_conv(x, w, b):
    c = x.shape[-1]
    y = lax.conv_general_dilated(
        x, w.astype(x.dtype)[:, None, :], window_strides=(1,),
        padding=[(RG_CONV - 1, 0)], dimension_numbers=("NWC", "WIO", "NWC"),
        feature_group_count=c)
    return y + b.astype(x.dtype)


def rg_lru(x, w_a, b_a, w_x, b_x, lam):
    bsz, s, _ = x.shape
    xb = x.reshape(bsz, s, RG_BLOCKS, RG_BLOCK_DIM)
    gate_r = jax.nn.sigmoid(jnp.einsum("bsnc,ncd->bsnd", xb, w_a).reshape(bsz, s, RG_WIDTH) + b_a)
    gate_i = jax.nn.sigmoid(jnp.einsum("bsnc,ncd->bsnd", xb, w_x).reshape(bsz, s, RG_WIDTH) + b_x)
    log_a = -RG_C * gate_r * jax.nn.softplus(-lam)
    a = jnp.exp(log_a)
    mult = jnp.sqrt(-jnp.expm1(2.0 * log_a))
    u = mult * (gate_i * x)

    def combine(left, right):
        return (left[0] * right[0], right[0] * left[1] + right[1])

    _, h = lax.associative_scan(combine, (a, u), axis=1)
    return h


def hgrn2_mix(q, f_pre, v, lb):
    bsz, s, _ = q.shape
    n = s // HG_CHUNK
    log_f = jnp.log(lb + (1.0 - lb) * jax.nn.sigmoid(f_pre))
    k = (1.0 - lb) * jax.nn.sigmoid(-f_pre)

    def to_chunks(t):
        return t.reshape(bsz, n, HG_CHUNK, HG_HEADS, HG_HEAD_DIM).transpose(0, 3, 1, 2, 4)

    q, k, v, log_f = to_chunks(q), to_chunks(k), to_chunks(v), to_chunks(log_f)
    cum = jnp.cumsum(log_f, axis=3)
    total = cum[:, :, :, -1:, :]
    q_dec = q * jnp.exp(cum)
    k_inv = k * jnp.exp(-cum)
    k_end = k * jnp.exp(total - cum)
    causal = jnp.tril(jnp.ones((HG_CHUNK, HG_CHUNK), dtype=bool))
    scores = jnp.where(causal, jnp.einsum("bhnld,bhnmd->bhnlm", q_dec, k_inv), 0.0)
    o_intra = jnp.einsum("bhnlm,bhnme->bhnle", scores, v)

    def step(state, inp):
        q_c, k_c, v_c, dec_c = inp
        o_c = jnp.einsum("bhld,bhde->bhle", q_c, state)
        state = state * dec_c[..., None] + jnp.einsum("bhld,bhle->bhde", k_c, v_c)
        return state, o_c

    xs = (jnp.moveaxis(q_dec, 2, 0), jnp.moveaxis(k_end, 2, 0), jnp.moveaxis(v, 2, 0),
          jnp.moveaxis(jnp.exp(total[:, :, :, 0, :]), 2, 0))
    state0 = jnp.zeros((bsz, HG_HEADS, HG_HEAD_DIM, HG_HEAD_DIM), jnp.float32)
    _, o_inter = lax.scan(step, state0, xs)
    o = o_intra + jnp.moveaxis(o_inter, 0, 2)
    return o.transpose(0, 2, 3, 1, 4).reshape(bsz, s, HG_HEADS, HG_HEAD_DIM)


def rglru_hgrn2_layer(h, norm_g, w_in, conv_w, conv_b, w_a, b_a, w_x, b_x, lam, lb, hg_g, w_out):
    bsz, s, _ = h.shape
    u = rms_norm(h, norm_g).astype(jnp.float32)
    z = u @ w_in
    xa, ga, q, f_pre, iv, gb = jnp.split(z, AB_SPLITS, axis=-1)
    xa = causal_depthwise_conv(xa, conv_w, conv_b)
    ya = rg_lru(xa, w_a, b_a, w_x, b_x, lam) * jax.nn.silu(ga)
    o = hgrn2_mix(q, f_pre, iv, lb)
    o = o * lax.rsqrt(jnp.mean(o * o, axis=-1, keepdims=True) + RMS_EPS) * hg_g
    yb = o.reshape(bsz, s, HG_WIDTH) * jax.nn.silu(gb)
    out = jnp.concatenate([ya, yb], axis=-1) @ w_out
    return h + out.astype(h.dtype)


def rwkv7_scan(r, decay, k, v, kk, a):
    bsz = r.shape[0]

    def step(state, inp):
        r_t, w_t, k_t, v_t, kk_t, a_t = inp
        s_kk = jnp.einsum("bhvk,bhk->bhv", state, kk_t)
        state = (state * w_t[:, :, None, :]
                 - s_kk[..., None] * (kk_t * a_t)[:, :, None, :]
                 + v_t[..., None] * k_t[:, :, None, :])
        y_t = jnp.einsum("bhvk,bhk->bhv", state, r_t)
        return state, y_t

    xs = tuple(jnp.moveaxis(t, 1, 0) for t in (r, decay, k, v, kk, a))
    state0 = jnp.zeros((bsz, RW_HEADS, RW_HEAD_DIM, RW_HEAD_DIM), jnp.float32)
    _, y = lax.scan(step, state0, xs)
    return jnp.moveaxis(y, 0, 1)


def rwkv7_layer(h, norm_g, mu, w_r, w_k, w_v, w_g, w0, w1, w2, a0, a1, a2,
                k_k, k_a, r_k, lnx_g, lnx_b, w_o):
    bsz, s, _ = h.shape
    u = rms_norm(h, norm_g).astype(jnp.float32)
    delta = jnp.pad(u, ((0, 0), (1, 0), (0, 0)))[:, :-1] - u
    mu = mu.astype(jnp.float32)
    x_r, x_w, x_k, x_v, x_a, x_g = (u + delta * mu[i] for i in range(6))

    def heads(t):
        return t.reshape(bsz, s, RW_HEADS, RW_HEAD_DIM)

    r = x_r @ w_r
    k_raw = x_k @ w_k
    v = x_v @ w_v
    gate = jax.nn.silu(x_g @ w_g)
    w_log = -jax.nn.softplus(-(w0 + jnp.tanh(x_w @ w1) @ w2)) - 0.5
    decay = jnp.exp(-jnp.exp(w_log))
    a = jax.nn.sigmoid(a0 + (x_a @ a1) @ a2)
    kk = heads(k_raw * k_k)
    kk = kk / jnp.maximum(jnp.sqrt(jnp.sum(kk * kk, axis=-1, keepdims=True)), 1e-12)
    k = heads(k_raw * (1.0 + (a - 1.0) * k_a))
    r, v, decay, a = heads(r), heads(v), heads(decay), heads(a)
    y = rwkv7_scan(r, decay, k, v, kk, a)
    mean = jnp.mean(y, axis=-1, keepdims=True)
    var = jnp.mean(jnp.square(y - mean), axis=-1, keepdims=True)
    y = ((y - mean) * lax.rsqrt(var + GN_EPS)).reshape(bsz, s, RW_WIDTH) * lnx_g + lnx_b
    bonus = (jnp.sum(r * k * r_k, axis=-1, keepdims=True) * v).reshape(bsz, s, RW_WIDTH)
    out = ((y + bonus) * gate) @ w_o
    return h + out.astype(h.dtype)


def setup_inputs(seed: int = 0) -> dict:
    key = jax.random.key(seed)
    ks = iter(jax.random.split(key, 40))
    f32 = jnp.float32

    def nrm(shape, scale):
        return jax.random.normal(next(ks), shape, f32) * scale

    x = nrm((BATCH, SEQ, D_MODEL), 1.0)
    ab_norm_g = 1.0 + nrm((N_AB_LAYERS, D_MODEL), 0.02)
    ab_w_in = nrm((N_AB_LAYERS, D_MODEL, AB_IN_WIDTH), D_MODEL ** -0.5)
    rg_conv_w = nrm((N_AB_LAYERS, RG_CONV, RG_WIDTH), RG_CONV ** -0.5)
    rg_conv_b = nrm((N_AB_LAYERS, RG_WIDTH), 0.02)
    rg_w_a = nrm((N_AB_LAYERS, RG_BLOCKS, RG_BLOCK_DIM, RG_BLOCK_DIM), RG_BLOCK_DIM ** -0.5)
    rg_b_a = nrm((N_AB_LAYERS, RG_WIDTH), 0.02)
    rg_w_x = nrm((N_AB_LAYERS, RG_BLOCKS, RG_BLOCK_DIM, RG_BLOCK_DIM), RG_BLOCK_DIM ** -0.5)
    rg_b_x = nrm((N_AB_LAYERS, RG_WIDTH), 0.02)
    a_pow = jax.random.uniform(next(ks), (N_AB_LAYERS, RG_WIDTH), f32, minval=0.9, maxval=0.999)
    p = a_pow ** (1.0 / RG_C)
    rg_lambda = jnp.log(p) - jnp.log1p(-p)
    hg_lb_logits = nrm((N_AB_LAYERS + 1, HG_WIDTH), 0.1)
    hg_norm_g = 1.0 + nrm((N_AB_LAYERS, HG_HEAD_DIM), 0.02)
    ab_w_out = nrm((N_AB_LAYERS, AB_MIX_WIDTH, D_MODEL), AB_MIX_WIDTH ** -0.5)

    c_norm_g = 1.0 + nrm((N_C_LAYERS, D_MODEL), 0.02)
    c_mu = jax.random.uniform(next(ks), (N_C_LAYERS, 6, D_MODEL), f32)
    c_w_r = nrm((N_C_LAYERS, D_MODEL, RW_WIDTH), D_MODEL ** -0.5)
    c_w_k = nrm((N_C_LAYERS, D_MODEL, RW_WIDTH), D_MODEL ** -0.5)
    c_w_v = nrm((N_C_LAYERS, D_MODEL, RW_WIDTH), D_MODEL ** -0.5)
    c_w_g = nrm((N_C_LAYERS, D_MODEL, RW_WIDTH), D_MODEL ** -0.5)
    c_w0 = jnp.linspace(-6.0, -1.0, RW_WIDTH, dtype=f32)[None, :] + nrm((N_C_LAYERS, RW_WIDTH), 0.1)
    c_w1 = nrm((N_C_LAYERS, D_MODEL, RW_LORA), D_MODEL ** -0.5)
    c_w2 = nrm((N_C_LAYERS, RW_LORA, RW_WIDTH), 0.5 * RW_LORA ** -0.5)
    c_a0 = nrm((N_C_LAYERS, RW_WIDTH), 0.1)
    c_a1 = nrm((N_C_LAYERS, D_MODEL, RW_LORA), D_MODEL ** -0.5)
    c_a2 = nrm((N_C_LAYERS, RW_LORA, RW_WIDTH), RW_LORA ** -0.5)
    c_k_k = 0.85 + nrm((N_C_LAYERS, RW_WIDTH), 0.02)
    c_k_a = 1.0 + nrm((N_C_LAYERS, RW_WIDTH), 0.02)
    c_r_k = nrm((N_C_LAYERS, RW_HEADS, RW_HEAD_DIM), 0.1)
    c_lnx_g = 1.0 + nrm((N_C_LAYERS, RW_WIDTH), 0.02)
    c_lnx_b = nrm((N_C_LAYERS, RW_WIDTH), 0.02)
    c_w_o = nrm((N_C_LAYERS, RW_WIDTH, D_MODEL), RW_WIDTH ** -0.5)
    final_g = 1.0 + nrm((D_MODEL,), 0.02)
    return {
        "x": x, "ab_norm_g": ab_norm_g, "ab_w_in": ab_w_in, "rg_conv_w": rg_conv_w,
        "rg_conv_b": rg_conv_b, "rg_w_a": rg_w_a, "rg_b_a": rg_b_a, "rg_w_x": rg_w_x,
        "rg_b_x": rg_b_x, "rg_lambda": rg_lambda, "hg_lb_logits": hg_lb_logits,
        "hg_norm_g": hg_norm_g, "ab_w_out": ab_w_out, "c_norm_g": c_norm_g, "c_mu": c_mu,
        "c_w_r": c_w_r, "c_w_k": c_w_k, "c_w_v": c_w_v, "c_w_g": c_w_g, "c_w0": c_w0,
        "c_w1": c_w1, "c_w2": c_w2, "c_a0": c_a0, "c_a1": c_a1, "c_a2": c_a2,
        "c_k_k": c_k_k, "c_k_a": c_k_a, "c_r_k": c_r_k, "c_lnx_g": c_lnx_g,
        "c_lnx_b": c_lnx_b, "c_w_o": c_w_o, "final_g": final_g,
    }


def reference(x, ab_norm_g, ab_w_in, rg_conv_w, rg_conv_b, rg_w_a, rg_b_a, rg_w_x, rg_b_x,
              rg_lambda, hg_lb_logits, hg_norm_g, ab_w_out, c_norm_g, c_mu, c_w_r, c_w_k,
              c_w_v, c_w_g, c_w0, c_w1, c_w2, c_a0, c_a1, c_a2, c_k_k, c_k_a, c_r_k,
              c_lnx_g, c_lnx_b, c_w_o, final_g):
    lb_table = jnp.cumsum(jax.nn.softmax(hg_lb_logits.astype(jnp.float32), axis=0), axis=0)
    h = x
    for layer in range(DEPTH):
        j = layer // 2
        if layer % 2 == 0:
            h = rglru_hgrn2_layer(h, ab_norm_g[j], ab_w_in[j], rg_conv_w[j], rg_conv_b[j],
                                  rg_w_a[j], rg_b_a[j], rg_w_x[j], rg_b_x[j], rg_lambda[j],
                                  lb_table[j], hg_norm_g[j], ab_w_out[j])
        else:
            h = rwkv7_layer(h, c_norm_g[j], c_mu[j], c_w_r[j], c_w_k[j], c_w_v[j], c_w_g[j],
                            c_w0[j], c_w1[j], c_w2[j], c_a0[j], c_a1[j], c_a2[j], c_k_k[j],
                            c_k_a[j], c_r_k[j], c_lnx_g[j], c_lnx_b[j], c_w_o[j])
    return rms_norm(h, final_g)
```

```python
import functools

import jax
import jax.numpy as jnp
from jax import lax
from jax.experimental import pallas as pl
from jax.experimental.pallas import tpu as pltpu

F32 = jnp.float32
BF16 = jnp.bfloat16

RMS_EPS = 1e-6
GN_EPS = 64e-5
RG_C = 8.0
RG_BLOCKS = 8
RG_CONV = 4
HG_HEAD_DIM = 128
RW_HEAD_DIM = 64
CHUNK = 64
LANES = 128
SUBLANES = 8
VMEM_LIMIT = 56 * 1024 * 1024


def _cparams(*sem):
    return pltpu.CompilerParams(dimension_semantics=sem, vmem_limit_bytes=VMEM_LIMIT)


def _mm(a, b):
    return jnp.dot(a, b, preferred_element_type=F32)


def _mm_nt(a, b):
    return lax.dot_general(a, b, (((1,), (1,)), ((), ())), preferred_element_type=F32)


def _mm_exact(a, b):
    return jnp.dot(a, b, preferred_element_type=F32, precision=lax.Precision.HIGHEST)


def _rms(x, g):
    return x * lax.rsqrt(jnp.mean(x * x, axis=-1, keepdims=True) + RMS_EPS) * g


def _sigmoid(x):
    return 1.0 / (1.0 + jnp.exp(-x))


def _silu(x):
    return x * _sigmoid(x)


def _softplus(x):
    return jnp.maximum(x, 0.0) + jnp.log(1.0 + jnp.exp(-jnp.abs(x)))


def _norm_matmul_kernel(x_ref, g_ref, w_ref, o_ref):
    xn = _rms(x_ref[...], g_ref[...]).astype(BF16)
    o_ref[...] = _mm(xn, w_ref[...])


def _norm_matmul(x, g, w, tm, tn):
    t, d = x.shape
    n = w.shape[1]
    return pl.pallas_call(
        _norm_matmul_kernel,
        grid=(n // tn, t // tm),
        in_specs=[pl.BlockSpec((tm, d), lambda j, i: (i, 0)),
                  pl.BlockSpec((1, d), lambda j, i: (0, 0)),
                  pl.BlockSpec((d, tn), lambda j, i: (0, j))],
        out_specs=pl.BlockSpec((tm, tn), lambda j, i: (i, j)),
        out_shape=jax.ShapeDtypeStruct((t, n), F32),
        compiler_params=_cparams("parallel", "parallel"),
        name="l0_norm_inproj",
    )(x, g, w)


def _rglru_kernel(xa_ref, ga_ref, cw_ref, cb_ref, wa_ref, ba_ref, wx_ref, bx_ref, lam_ref,
                  o_ref, xbuf, a_s, u_s, hcar):
    ts, width = xa_ref.shape
    bd = width // RG_BLOCKS

    @pl.when(pl.program_id(1) == 0)
    def _():
        xbuf[0:SUBLANES, :] = jnp.zeros((SUBLANES, width), F32)
        hcar[...] = jnp.zeros_like(hcar)

    xa = xa_ref[...]
    xbuf[SUBLANES:SUBLANES + ts, :] = xa
    xc = cb_ref[...] + cw_ref[RG_CONV - 1:RG_CONV, :] * xa
    for j in range(1, RG_CONV):
        xc = xc + cw_ref[RG_CONV - 1 - j:RG_CONV - j, :] * xbuf[pl.ds(SUBLANES - j, ts), :]
    xbuf[0:SUBLANES, :] = xa[ts - SUBLANES:ts, :]

    xcb = xc.astype(BF16)
    gr, gi = [], []
    for n in range(RG_BLOCKS):
        blk = xcb[:, n * bd:(n + 1) * bd]
        gr.append(_mm(blk, wa_ref[n]))
        gi.append(_mm(blk, wx_ref[n]))
    gate_r = _sigmoid(jnp.concatenate(gr, axis=-1) + ba_ref[...])
    gate_i = _sigmoid(jnp.concatenate(gi, axis=-1) + bx_ref[...])
    log_a = (-RG_C) * gate_r * _softplus(-lam_ref[...])
    a = jnp.exp(log_a)
    u = jnp.sqrt(-jnp.tanh(log_a) * (1.0 + a * a)) * (gate_i * xc)

    row = lax.broadcasted_iota(jnp.int32, (ts, width), 0) & (SUBLANES - 1)
    d = 1
    while d < SUBLANES:
        keep = row >= d
        u = jnp.where(keep, a * pltpu.roll(u, d, axis=0) + u, u)
        a = jnp.where(keep, a * pltpu.roll(a, d, axis=0), a)
        d *= 2
    a_s[...] = a
    u_s[...] = u

    def body(j, h):
        r0 = pl.multiple_of(j * SUBLANES, SUBLANES)
        hb = a_s[pl.ds(r0, SUBLANES), :] * h + u_s[pl.ds(r0, SUBLANES), :]
        u_s[pl.ds(r0, SUBLANES), :] = hb
        return hb[SUBLANES - 1:SUBLANES, :]

    hcar[0:1, :] = lax.fori_loop(0, ts // SUBLANES, body, hcar[0:1, :])
    o_ref[...] = (u_s[...] * _silu(ga_ref[...])).astype(o_ref.dtype)


def _rglru(z, conv_w, conv_b, w_a, b_a, w_x, b_x, lam, bsz, seq, ts):
    width = conv_w.shape[1]
    nt = seq // ts
    row = lambda b, i: (b * nt + i, 0)
    full2 = lambda b, i: (0, 0)
    full3 = lambda b, i: (0, 0, 0)
    return pl.pallas_call(
        _rglru_kernel,
        grid=(bsz, nt),
        in_specs=[pl.BlockSpec((ts, width), row),
                  pl.BlockSpec((ts, width), lambda b, i: (b * nt + i, 1)),
                  pl.BlockSpec(conv_w.shape, full2),
                  pl.BlockSpec((1, width), full2),
                  pl.BlockSpec(w_a.shape, full3),
                  pl.BlockSpec((1, width), full2),
                  pl.BlockSpec(w_x.shape, full3),
                  pl.BlockSpec((1, width), full2),
                  pl.BlockSpec((1, width), full2)],
        out_specs=pl.BlockSpec((ts, width), row),
        out_shape=jax.ShapeDtypeStruct((bsz * seq, width), BF16),
        scratch_shapes=[pltpu.VMEM((ts + SUBLANES, width), F32),
                        pltpu.VMEM((ts, width), F32),
                        pltpu.VMEM((ts, width), F32),
                        pltpu.VMEM((SUBLANES, width), F32)],
        compiler_params=_cparams("parallel", "arbitrary"),
        name="l0_rglru",
    )(z, z, conv_w, conv_b, w_a, b_a, w_x, b_x, lam)


def _hgrn2_kernel(q_ref, f_ref, v_ref, g_ref, lbl_ref, gn_ref, o_ref, st):
    tc = q_ref.shape[0]

    @pl.when(pl.program_id(2) == 0)
    def _():
        st[...] = jnp.zeros_like(st)

    lbl = lbl_ref[...]
    e = jnp.exp(lbl - jnp.max(lbl, axis=0, keepdims=True))
    lb = e[0:1, :] / jnp.sum(e, axis=0, keepdims=True)

    ri = lax.broadcasted_iota(jnp.int32, (CHUNK, CHUNK), 0)
    ci = lax.broadcasted_iota(jnp.int32, (CHUNK, CHUNK), 1)
    causal = ri >= ci
    tril = jnp.where(causal, 1.0, 0.0).astype(F32)

    state = st[...]
    for n in range(tc // CHUNK):
        sl = slice(n * CHUNK, (n + 1) * CHUNK)
        q, fp, v = q_ref[sl, :], f_ref[sl, :], v_ref[sl, :]
        log_f = jnp.log(lb + (1.0 - lb) * _sigmoid(fp))
        k = (1.0 - lb) * _sigmoid(-fp)
        cum = _mm_exact(tril, log_f)
        total = cum[CHUNK - 1:CHUNK, :]
        q_dec = (q * jnp.exp(cum)).astype(BF16)
        k_inv = (k * jnp.exp(-cum)).astype(BF16)
        k_end = (k * jnp.exp(total - cum)).astype(BF16)
        vb = v.astype(BF16)
        scores = jnp.where(causal, _mm_nt(q_dec, k_inv), 0.0).astype(BF16)
        o = _mm(scores, vb) + _mm_nt(q_dec, state.astype(BF16))
        state = state * jnp.exp(total) + _mm(v.T.astype(BF16), k_end)
        o = o * lax.rsqrt(jnp.mean(o * o, axis=-1, keepdims=True) + RMS_EPS) * gn_ref[...]
        o_ref[sl, :] = (o * _silu(g_ref[sl, :])).astype(o_ref.dtype)
    st[...] = state


def _hgrn2(z, lb_logits, hg_g, bsz, seq, tc, col0):
    width = lb_logits.shape[1]
    heads = width // HG_HEAD_DIM
    nt = seq // tc

    def col(k):
        return lambda b, h, i: (b * nt + i, (col0 + k * width) // HG_HEAD_DIM + h)

    return pl.pallas_call(
        _hgrn2_kernel,
        grid=(bsz, heads, nt),
        in_specs=[pl.BlockSpec((tc, HG_HEAD_DIM), col(0)),
                  pl.BlockSpec((tc, HG_HEAD_DIM), col(1)),
                  pl.BlockSpec((tc, HG_HEAD_DIM), col(2)),
                  pl.BlockSpec((tc, HG_HEAD_DIM), col(3)),
                  pl.BlockSpec((lb_logits.shape[0], HG_HEAD_DIM), lambda b, h, i: (0, h)),
                  pl.BlockSpec((1, HG_HEAD_DIM), lambda b, h, i: (0, 0))],
        out_specs=pl.BlockSpec((tc, HG_HEAD_DIM), lambda b, h, i: (b * nt + i, h)),
        out_shape=jax.ShapeDtypeStruct((bsz * seq, width), BF16),
        scratch_shapes=[pltpu.VMEM((HG_HEAD_DIM, HG_HEAD_DIM), F32)],
        compiler_params=_cparams("parallel", "parallel", "arbitrary"),
        name="l0_hgrn2",
    )(z, z, z, z, lb_logits, hg_g)


def _outproj0_kernel(ya_ref, yb_ref, wa_ref, wb_ref, x_ref, g_ref, h_ref, u_ref):
    h = x_ref[...] + _mm(ya_ref[...], wa_ref[...]) + _mm(yb_ref[...], wb_ref[...])
    h_ref[...] = h
    u_ref[...] = _rms(h, g_ref[...])


def _outproj0(ya, yb, w_a, w_b, x, g, tm):
    t, d = x.shape
    row = lambda i: (i, 0)
    full = lambda i: (0, 0)
    return pl.pallas_call(
        _outproj0_kernel,
        grid=(t // tm,),
        in_specs=[pl.BlockSpec((tm, ya.shape[1]), row),
                  pl.BlockSpec((tm, yb.shape[1]), row),
                  pl.BlockSpec(w_a.shape, full),
                  pl.BlockSpec(w_b.shape, full),
                  pl.BlockSpec((tm, d), row),
                  pl.BlockSpec((1, d), full)],
        out_specs=[pl.BlockSpec((tm, d), row), pl.BlockSpec((tm, d), row)],
        out_shape=[jax.ShapeDtypeStruct((t, d), F32), jax.ShapeDtypeStruct((t, d), F32)],
        compiler_params=_cparams("parallel"),
        name="l0_outproj",
    )(ya, yb, w_a, w_b, x, g)


def _shifted(u, up_ref, first_of_seq):
    prev_last = jnp.where(first_of_seq, 0.0, up_ref[SUBLANES - 1:SUBLANES, :])
    row = lax.broadcasted_iota(jnp.int32, u.shape, 0)
    return jnp.where(row == 0, prev_last, pltpu.roll(u, 1, axis=0))


def _mixproj_kernel(u_ref, up_ref, mu_ref, w_ref, o_ref, *, steps_per_seq, gate_index):
    u = u_ref[...]
    prev = _shifted(u, up_ref, pl.program_id(1) % steps_per_seq == 0)
    x = (u + (prev - u) * mu_ref[0]).astype(BF16)
    y = _mm(x, w_ref[0])

    @pl.when(pl.program_id(0) == gate_index)
    def _():
        o_ref[0] = _silu(y)

    @pl.when(pl.program_id(0) != gate_index)
    def _():
        o_ref[0] = y


def _mixproj(u, mu4, w4, seq, tm, gate_index):
    t, d = u.shape
    nproj, _, n = w4.shape
    rb = tm // SUBLANES
    return pl.pallas_call(
        functools.partial(_mixproj_kernel, steps_per_seq=seq // tm, gate_index=gate_index),
        grid=(nproj, t // tm),
        in_specs=[pl.BlockSpec((tm, d), lambda j, i: (i, 0)),
                  pl.BlockSpec((SUBLANES, d), lambda j, i: (jnp.maximum(i * rb - 1, 0), 0)),
                  pl.BlockSpec((1, 1, d), lambda j, i: (j, 0, 0)),
                  pl.BlockSpec((1, d, n), lambda j, i: (j, 0, 0))],
        out_specs=pl.BlockSpec((1, tm, n), lambda j, i: (j, i, 0)),
        out_shape=jax.ShapeDtypeStruct((nproj, t, n), F32),
        compiler_params=_cparams("parallel", "parallel"),
        name="l1_mixproj",
    )(u, u, mu4, w4)


def _lora_kernel(u_ref, up_ref, mu_ref, w1_ref, w2_ref, w0_ref, a1_ref, a2_ref, a0_ref,
                 lw_ref, a_ref, *, steps_per_seq):
    u = u_ref[...]
    prev = _shifted(u, up_ref, pl.program_id(0) % steps_per_seq == 0)
    delta = prev - u
    x_w = (u + delta * mu_ref[0:1, :]).astype(BF16)
    x_a = (u + delta * mu_ref[1:2, :]).astype(BF16)
    zw = w0_ref[...] + _mm(jnp.tanh(_mm(x_w, w1_ref[...])).astype(BF16), w2_ref[...])
    lw_ref[...] = -jnp.exp(-_softplus(-zw) - 0.5)
    za = a0_ref[...] + _mm(_mm(x_a, a1_ref[...]).astype(BF16), a2_ref[...])
    a_ref[...] = _sigmoid(za)


def _lora(u, mu2, w1, w2, w0, a1, a2, a0, seq, tm):
    t, d = u.shape
    n = w2.shape[1]
    rb = tm // SUBLANES
    full = lambda i: (0, 0)
    row = lambda i: (i, 0)
    return pl.pallas_call(
        functools.partial(_lora_kernel, steps_per_seq=seq // tm),
        grid=(t // tm,),
        in_specs=[pl.BlockSpec((tm, d), row),
                  pl.BlockSpec((SUBLANES, d), lambda i: (jnp.maximum(i * rb - 1, 0), 0)),
                  pl.BlockSpec(mu2.shape, full),
                  pl.BlockSpec(w1.shape, full), pl.BlockSpec(w2.shape, full),
                  pl.BlockSpec((1, n), full),
                  pl.BlockSpec(a1.shape, full), pl.BlockSpec(a2.shape, full),
                  pl.BlockSpec((1, n), full)],
        out_specs=[pl.BlockSpec((tm, n), row), pl.BlockSpec((tm, n), row)],
        out_shape=[jax.ShapeDtypeStruct((t, n), F32), jax.ShapeDtypeStruct((t, n), F32)],
        compiler_params=_cparams("parallel"),
        name="l1_lora",
    )(u, u, mu2, w1, w2, w0, a1, a2, a0)


def _rwkv_kernel(r_ref, k_ref, v_ref, g_ref, lw_ref, a_ref, kk_ref, ka_ref, rk_ref, lg_ref,
                 lb_ref, o_ref, st, *, pairs):
    tc = lw_ref.shape[0]
    L = CHUNK
    hd = RW_HEAD_DIM

    @pl.when(pl.program_id(2) == 0)
    def _():
        st[...] = jnp.zeros_like(st)

    row = lax.broadcasted_iota(jnp.int32, (L, LANES), 0)
    lane = lax.broadcasted_iota(jnp.int32, (L, LANES), 1)
    lane_h = lane & (hd - 1)
    strict = lane_h < row
    incl = lane_h <= row
    eye = jnp.where(lane_h == row, 1.0, 0.0).astype(F32)
    head_a = lane < hd
    row2 = lax.broadcasted_iota(jnp.int32, (2 * L, LANES), 0)
    lane2 = lax.broadcasted_iota(jnp.int32, (2 * L, LANES), 1)
    same_head = (row2 < hd) == (lane2 < hd)
    ones_bd = jnp.where(same_head, 1.0, 0.0).astype(BF16)
    ones_bd2 = jnp.concatenate([ones_bd, ones_bd], axis=0)
    ri = lax.broadcasted_iota(jnp.int32, (L, L), 0)
    ci = lax.broadcasted_iota(jnp.int32, (L, L), 1)
    tril = jnp.where(ri >= ci, 1.0, 0.0).astype(F32)

    def split(x):
        return jnp.concatenate([jnp.where(head_a, x, 0.0), jnp.where(head_a, 0.0, x)],
                               axis=0).astype(BF16)

    def headsum(x):
        hi = x.astype(BF16)
        lo = (x - hi.astype(F32)).astype(BF16)
        return _mm(jnp.concatenate([hi, lo], axis=1), ones_bd2)

    def chunk(c, carry):
        r0 = pl.multiple_of(c * L, L)
        rows = pl.ds(r0, L)
        for p in range(pairs):
            cols = slice(p * LANES, (p + 1) * LANES)
            r = r_ref[0, rows, cols]
            kraw = k_ref[0, rows, cols]
            v = v_ref[0, rows, cols]
            lw = lw_ref[rows, cols]
            a = a_ref[rows, cols]
            kkp = kraw * kk_ref[:, cols]
            kk = kkp / jnp.maximum(jnp.sqrt(headsum(kkp * kkp)), 1e-12)
            k = kraw * (1.0 + (a - 1.0) * ka_ref[:, cols])
            b = kk * a

            cum = _mm_exact(tril, lw)
            total = cum[L - 1:L, :]
            w_in = jnp.exp(cum)
            w_out = jnp.exp(-cum)
            w_end = jnp.exp(total - cum)
            r_w = r * w_in
            kk_w = kk * jnp.exp(cum - lw)
            lhs = jnp.concatenate([kk_w, r_w], axis=0).astype(BF16)
            zk = split(k * w_out)
            zb = split(b * w_out)
            zv = split(v)
            p_k = _mm_nt(lhs, zk)
            p_b = _mm_nt(lhs, zb)

            amat = jnp.where(strict, p_b[0:L], 0.0)
            x = eye - amat
            m = _mm(amat.astype(BF16), split(amat))
            n_sq = L.bit_length() - 3
            for _ in range(n_sq):
                xm = _mm(jnp.concatenate([x, m], axis=0).astype(BF16), split(m))
                x = x + xm[0:L]
                m = xm[L:2 * L]
            x = x + _mm(x.astype(BF16), split(m))

            state = st[p]
            hs = _mm_nt(lhs, state.astype(BF16))
            g_rhs = hs[0:L] + _mm(jnp.where(strict, p_k[0:L], 0.0).astype(BF16), zv)
            u = _mm(x.astype(BF16), split(g_rhs))
            y = (hs[L:2 * L]
                 + _mm(jnp.where(incl, p_k[L:2 * L], 0.0).astype(BF16), zv)
                 - _mm(jnp.where(incl, p_b[L:2 * L], 0.0).astype(BF16), split(u)))
            vu_t = jnp.concatenate([v, -u], axis=0).T.astype(BF16)
            kb_end = jnp.concatenate([k * w_end, b * w_end], axis=0).astype(BF16)
            upd = _mm(vu_t, kb_end)
            st[p] = state * jnp.exp(total) + jnp.where(same_head, upd, 0.0)

            mean = headsum(y) * (1.0 / hd)
            yc = y - mean
            var = headsum(yc * yc) * (1.0 / hd)
            yn = yc * lax.rsqrt(var + GN_EPS) * lg_ref[:, cols] + lb_ref[:, cols]
            bonus = headsum(r * k * rk_ref[:, cols]) * v
            o_ref[rows, cols] = ((yn + bonus) * g_ref[0, rows, cols]).astype(o_ref.dtype)
        return carry

    lax.fori_loop(0, tc // L, chunk, 0)


def _rwkv(rkvg, lw, a, k_k, k_a, r_k, lnx_g, lnx_b, bsz, seq, tc, pairs):
    t, width = lw.shape
    wb = pairs * LANES
    nt = seq // tc

    def proj(j):
        return pl.BlockSpec((1, tc, wb), lambda b, p, i: (j, b * nt + i, p))

    tok = pl.BlockSpec((tc, wb), lambda b, p, i: (b * nt + i, p))
    par = pl.BlockSpec((1, wb), lambda b, p, i: (0, p))
    return pl.pallas_call(
        functools.partial(_rwkv_kernel, pairs=pairs),
        grid=(bsz, width // wb, nt),
        in_specs=[proj(0), proj(1), proj(2), proj(3), tok, tok, par, par, par, par, par],
        out_specs=tok,
        out_shape=jax.ShapeDtypeStruct((t, width), BF16),
        scratch_shapes=[pltpu.VMEM((pairs, LANES, LANES), F32)],
        compiler_params=_cparams("parallel", "parallel", "arbitrary"),
        name="l1_rwkv7",
    )(rkvg, rkvg, rkvg, rkvg, lw, a, k_k, k_a, r_k, lnx_g, lnx_b)


def _outproj1_kernel(y_ref, w_ref, h_ref, g_ref, o_ref):
    h = h_ref[...] + _mm(y_ref[...], w_ref[...])
    o_ref[...] = _rms(h, g_ref[...])


def _outproj1(y, w, h, g, tm):
    t, d = h.shape
    row = lambda i: (i, 0)
    full = lambda i: (0, 0)
    return pl.pallas_call(
        _outproj1_kernel,
        grid=(t // tm,),
        in_specs=[pl.BlockSpec((tm, y.shape[1]), row), pl.BlockSpec(w.shape, full),
                  pl.BlockSpec((tm, d), row), pl.BlockSpec((1, d), full)],
        out_specs=pl.BlockSpec((tm, d), row),
        out_shape=jax.ShapeDtypeStruct((t, d), F32),
        compiler_params=_cparams("parallel"),
        name="l1_outproj_final",
    )(y, w, h, g)


def _tile(n, pref):
    t = min(n, pref)
    assert n % t == 0, (n, pref)
    return t


def _layer0(x2, bsz, seq, norm_g, w_in, conv_w, conv_b, w_a, b_a, w_x, b_x, lam, lb_logits,
            hg_g, w_out, next_norm_g):
    t, d = x2.shape
    width = conv_w.shape[1]
    row = lambda p: p.reshape(1, -1)
    z = _norm_matmul(x2, row(norm_g), w_in.astype(BF16), _tile(t, 512), _tile(w_in.shape[1], 2048))
    ya = _rglru(z, conv_w, row(conv_b), w_a.astype(BF16), row(b_a), w_x.astype(BF16), row(b_x),
                row(lam), bsz, seq, _tile(seq, 512))
    yb = _hgrn2(z, lb_logits, row(hg_g), bsz, seq, _tile(seq, 512), 2 * width)
    w_out = w_out.astype(BF16)
    return _outproj0(ya, yb, w_out[:width], w_out[width:], x2, row(next_norm_g), _tile(t, 512))


def _layer1(h, u, bsz, seq, mu, w_r, w_k, w_v, w_g, w0, w1, w2, a0, a1, a2, k_k, k_a, r_k,
            lnx_g, lnx_b, w_o, final_g):
    t, d = h.shape
    row = lambda p: p.reshape(1, -1)
    mu4 = jnp.stack([mu[0], mu[2], mu[3], mu[5]])[:, None, :]
    w4 = jnp.stack([w_r, w_k, w_v, w_g]).astype(BF16)
    tm = _tile(seq, 512)
    rkvg = _mixproj(u, mu4, w4, seq, tm, gate_index=3)
    lw, a = _lora(u, jnp.stack([mu[1], mu[4]]), w1.astype(BF16), w2.astype(BF16), row(w0),
                  a1.astype(BF16), a2.astype(BF16), row(a0), seq, tm)
    y = _rwkv(rkvg, lw, a, row(k_k), row(k_a), row(r_k), row(lnx_g), row(lnx_b), bsz, seq,
              _tile(seq, 512), pairs=4)
    return _outproj1(y, w_o.astype(BF16), h, row(final_g), _tile(t, 512))


def kernel(x, ab_norm_g, ab_w_in, rg_conv_w, rg_conv_b, rg_w_a, rg_b_a, rg_w_x, rg_b_x, rg_lambda, hg_lb_logits, hg_norm_g, ab_w_out, c_norm_g, c_mu, c_w_r, c_w_k, c_w_v, c_w_g, c_w0, c_w1, c_w2, c_a0, c_a1, c_a2, c_k_k, c_k_a, c_r_k, c_lnx_g, c_lnx_b, c_w_o, final_g):
    bsz, seq, d = x.shape
    assert ab_norm_g.shape[0] == 1 and c_norm_g.shape[0] == 1, "two-layer trunk only"
    x2 = x.reshape(bsz * seq, d)
    h, u = _layer0(x2, bsz, seq, ab_norm_g[0], ab_w_in[0], rg_conv_w[0], rg_conv_b[0], rg_w_a[0],
                   rg_b_a[0], rg_w_x[0], rg_b_x[0], rg_lambda[0], hg_lb_logits, hg_norm_g[0],
                   ab_w_out[0], c_norm_g[0])
    out = _layer1(h, u, bsz, seq, c_mu[0], c_w_r[0], c_w_k[0], c_w_v[0], c_w_g[0], c_w0[0],
                  c_w1[0], c_w2[0], c_a0[0], c_a1[0], c_a2[0], c_k_k[0], c_k_a[0],
                  c_r_k[0].reshape(-1), c_lnx_g[0], c_lnx_b[0], c_w_o[0], final_g)
    return out.reshape(bsz, seq, d)
```

```python
import functools

import jax
import jax.numpy as jnp
from jax import lax
from jax.experimental import pallas as pl
from jax.experimental.pallas import tpu as pltpu

F32 = jnp.float32
BF16 = jnp.bfloat16

RMS_EPS = 1e-6
GN_EPS = 64e-5
RG_C = 8.0
RG_BLOCKS = 8
RG_CONV = 4
HG_HEAD_DIM = 128
RW_HEAD_DIM = 64
CHUNK = 64
BUILD_CHUNKS = 4
LANES = 128
SUBLANES = 8
VMEM_LIMIT = 56 * 1024 * 1024


def _cparams(*sem):
    return pltpu.CompilerParams(dimension_semantics=sem, vmem_limit_bytes=VMEM_LIMIT)


def _mm(a, b):
    return jnp.dot(a, b, preferred_element_type=F32)


def _mm_nt(a, b):
    return lax.dot_general(a, b, (((1,), (1,)), ((), ())), preferred_element_type=F32)


def _mm_exact(a, b):
    return jnp.dot(a, b, preferred_element_type=F32, precision=lax.Precision.HIGHEST)


def _rms(x, g):
    return x * lax.rsqrt(jnp.mean(x * x, axis=-1, keepdims=True) + RMS_EPS) * g


def _sigmoid(x):
    return 1.0 / (1.0 + jnp.exp(-x))


def _silu(x):
    return x * _sigmoid(x)


def _softplus(x):
    return jnp.maximum(x, 0.0) + jnp.log(1.0 + jnp.exp(-jnp.abs(x)))


def _norm_matmul_kernel(x_ref, g_ref, w_ref, o_ref):
    xn = _rms(x_ref[...], g_ref[...]).astype(BF16)
    o_ref[...] = _mm(xn, w_ref[...])


def _norm_matmul(x, g, w, tm, tn):
    t, d = x.shape
    n = w.shape[1]
    return pl.pallas_call(
        _norm_matmul_kernel,
        grid=(n // tn, t // tm),
        in_specs=[pl.BlockSpec((tm, d), lambda j, i: (i, 0)),
                  pl.BlockSpec((1, d), lambda j, i: (0, 0)),
                  pl.BlockSpec((d, tn), lambda j, i: (0, j))],
        out_specs=pl.BlockSpec((tm, tn), lambda j, i: (i, j)),
        out_shape=jax.ShapeDtypeStruct((t, n), F32),
        compiler_params=_cparams("parallel", "parallel"),
        name="l0_norm_inproj",
    )(x, g, w)


def _rglru_kernel(xa_ref, ga_ref, cw_ref, cb_ref, wa_ref, ba_ref, wx_ref, bx_ref, lam_ref,
                  o_ref, xbuf, a_s, u_s, hcar):
    ts, width = xa_ref.shape
    bd = width // RG_BLOCKS

    @pl.when(pl.program_id(1) == 0)
    def _():
        xbuf[0:SUBLANES, :] = jnp.zeros((SUBLANES, width), F32)
        hcar[...] = jnp.zeros_like(hcar)

    xa = xa_ref[...]
    xbuf[SUBLANES:SUBLANES + ts, :] = xa
    xc = cb_ref[...] + cw_ref[RG_CONV - 1:RG_CONV, :] * xa
    for j in range(1, RG_CONV):
        xc = xc + cw_ref[RG_CONV - 1 - j:RG_CONV - j, :] * xbuf[pl.ds(SUBLANES - j, ts), :]
    xbuf[0:SUBLANES, :] = xa[ts - SUBLANES:ts, :]

    xcb = xc.astype(BF16)
    gr, gi = [], []
    for n in range(RG_BLOCKS):
        blk = xcb[:, n * bd:(n + 1) * bd]
        gr.append(_mm(blk, wa_ref[n]))
        gi.append(_mm(blk, wx_ref[n]))
    gate_r = _sigmoid(jnp.concatenate(gr, axis=-1) + ba_ref[...])
    gate_i = _sigmoid(jnp.concatenate(gi, axis=-1) + bx_ref[...])
    log_a = (-RG_C) * gate_r * _softplus(-lam_ref[...])
    a = jnp.exp(log_a)
    u = jnp.sqrt(-jnp.tanh(log_a) * (1.0 + a * a)) * (gate_i * xc)

    row = lax.broadcasted_iota(jnp.int32, (ts, width), 0) & (SUBLANES - 1)
    d = 1
    while d < SUBLANES:
        keep = row >= d
        u = jnp.where(keep, a * pltpu.roll(u, d, axis=0) + u, u)
        a = jnp.where(keep, a * pltpu.roll(a, d, axis=0), a)
        d *= 2
    a_s[...] = a
    u_s[...] = u

    def body(j, h):
        r0 = pl.multiple_of(j * SUBLANES, SUBLANES)
        hb = a_s[pl.ds(r0, SUBLANES), :] * h + u_s[pl.ds(r0, SUBLANES), :]
        u_s[pl.ds(r0, SUBLANES), :] = hb
        return hb[SUBLANES - 1:SUBLANES, :]

    hcar[0:1, :] = lax.fori_loop(0, ts // SUBLANES, body, hcar[0:1, :])
    o_ref[...] = (u_s[...] * _silu(ga_ref[...])).astype(o_ref.dtype)


def _rglru(z, conv_w, conv_b, w_a, b_a, w_x, b_x, lam, bsz, seq, ts):
    width = conv_w.shape[1]
    nt = seq // ts
    row = lambda b, i: (b * nt + i, 0)
    full2 = lambda b, i: (0, 0)
    full3 = lambda b, i: (0, 0, 0)
    return pl.pallas_call(
        _rglru_kernel,
        grid=(bsz, nt),
        in_specs=[pl.BlockSpec((ts, width), row),
                  pl.BlockSpec((ts, width), lambda b, i: (b * nt + i, 1)),
                  pl.BlockSpec(conv_w.shape, full2),
                  pl.BlockSpec((1, width), full2),
                  pl.BlockSpec(w_a.shape, full3),
                  pl.BlockSpec((1, width), full2),
                  pl.BlockSpec(w_x.shape, full3),
                  pl.BlockSpec((1, width), full2),
                  pl.BlockSpec((1, width), full2)],
        out_specs=pl.BlockSpec((ts, width), row),
        out_shape=jax.ShapeDtypeStruct((bsz * seq, width), BF16),
        scratch_shapes=[pltpu.VMEM((ts + SUBLANES, width), F32),
                        pltpu.VMEM((ts, width), F32),
                        pltpu.VMEM((ts, width), F32),
                        pltpu.VMEM((SUBLANES, width), F32)],
        compiler_params=_cparams("parallel", "arbitrary"),
        name="l0_rglru",
    )(z, z, conv_w, conv_b, w_a, b_a, w_x, b_x, lam)


def _hgrn2_kernel(q_ref, f_ref, v_ref, g_ref, lbl_ref, gn_ref, o_ref, st):
    tc = q_ref.shape[0]

    @pl.when(pl.program_id(2) == 0)
    def _():
        st[...] = jnp.zeros_like(st)

    lbl = lbl_ref[...]
    e = jnp.exp(lbl - jnp.max(lbl, axis=0, keepdims=True))
    lb = e[0:1, :] / jnp.sum(e, axis=0, keepdims=True)

    ri = lax.broadcasted_iota(jnp.int32, (CHUNK, CHUNK), 0)
    ci = lax.broadcasted_iota(jnp.int32, (CHUNK, CHUNK), 1)
    causal = ri >= ci
    tril = jnp.where(causal, 1.0, 0.0).astype(F32)

    state = st[...]
    for n in range(tc // CHUNK):
        sl = slice(n * CHUNK, (n + 1) * CHUNK)
        q, fp, v = q_ref[sl, :], f_ref[sl, :], v_ref[sl, :]
        log_f = jnp.log(lb + (1.0 - lb) * _sigmoid(fp))
        k = (1.0 - lb) * _sigmoid(-fp)
        cum = _mm_exact(tril, log_f)
        total = cum[CHUNK - 1:CHUNK, :]
        q_dec = (q * jnp.exp(cum)).astype(BF16)
        k_inv = (k * jnp.exp(-cum)).astype(BF16)
        k_end = (k * jnp.exp(total - cum)).astype(BF16)
        vb = v.astype(BF16)
        scores = jnp.where(causal, _mm_nt(q_dec, k_inv), 0.0).astype(BF16)
        o = _mm(scores, vb) + _mm_nt(q_dec, state.astype(BF16))
        state = state * jnp.exp(total) + _mm(v.T.astype(BF16), k_end)
        o = o * lax.rsqrt(jnp.mean(o * o, axis=-1, keepdims=True) + RMS_EPS) * gn_ref[...]
        o_ref[sl, :] = (o * _silu(g_ref[sl, :])).astype(o_ref.dtype)
    st[...] = state


def _hgrn2(z, lb_logits, hg_g, bsz, seq, tc, col0):
    width = lb_logits.shape[1]
    heads = width // HG_HEAD_DIM
    nt = seq // tc

    def col(k):
        return lambda b, h, i: (b * nt + i, (col0 + k * width) // HG_HEAD_DIM + h)

    return pl.pallas_call(
        _hgrn2_kernel,
        grid=(bsz, heads, nt),
        in_specs=[pl.BlockSpec((tc, HG_HEAD_DIM), col(0)),
                  pl.BlockSpec((tc, HG_HEAD_DIM), col(1)),
                  pl.BlockSpec((tc, HG_HEAD_DIM), col(2)),
                  pl.BlockSpec((tc, HG_HEAD_DIM), col(3)),
                  pl.BlockSpec((lb_logits.shape[0], HG_HEAD_DIM), lambda b, h, i: (0, h)),
                  pl.BlockSpec((1, HG_HEAD_DIM), lambda b, h, i: (0, 0))],
        out_specs=pl.BlockSpec((tc, HG_HEAD_DIM), lambda b, h, i: (b * nt + i, h)),
        out_shape=jax.ShapeDtypeStruct((bsz * seq, width), BF16),
        scratch_shapes=[pltpu.VMEM((HG_HEAD_DIM, HG_HEAD_DIM), F32)],
        compiler_params=_cparams("parallel", "parallel", "arbitrary"),
        name="l0_hgrn2",
    )(z, z, z, z, lb_logits, hg_g)


def _outproj0_kernel(ya_ref, yb_ref, wa_ref, wb_ref, x_ref, g_ref, h_ref, u_ref):
    h = x_ref[...] + _mm(ya_ref[...], wa_ref[...]) + _mm(yb_ref[...], wb_ref[...])
    h_ref[...] = h
    u_ref[...] = _rms(h, g_ref[...])


def _outproj0(ya, yb, w_a, w_b, x, g, tm):
    t, d = x.shape
    row = lambda i: (i, 0)
    full = lambda i: (0, 0)
    return pl.pallas_call(
        _outproj0_kernel,
        grid=(t // tm,),
        in_specs=[pl.BlockSpec((tm, ya.shape[1]), row),
                  pl.BlockSpec((tm, yb.shape[1]), row),
                  pl.BlockSpec(w_a.shape, full),
                  pl.BlockSpec(w_b.shape, full),
                  pl.BlockSpec((tm, d), row),
                  pl.BlockSpec((1, d), full)],
        out_specs=[pl.BlockSpec((tm, d), row), pl.BlockSpec((tm, d), row)],
        out_shape=[jax.ShapeDtypeStruct((t, d), F32), jax.ShapeDtypeStruct((t, d), F32)],
        compiler_params=_cparams("parallel"),
        name="l0_outproj",
    )(ya, yb, w_a, w_b, x, g)


def _shifted(u, up_ref, first_of_seq):
    prev_last = jnp.where(first_of_seq, 0.0, up_ref[SUBLANES - 1:SUBLANES, :])
    row = lax.broadcasted_iota(jnp.int32, u.shape, 0)
    return jnp.where(row == 0, prev_last, pltpu.roll(u, 1, axis=0))


def _mixproj_kernel(u_ref, up_ref, mu_ref, w_ref, o_ref, *, steps_per_seq, gate_index):
    u = u_ref[...]
    prev = _shifted(u, up_ref, pl.program_id(1) % steps_per_seq == 0)
    x = (u + (prev - u) * mu_ref[0]).astype(BF16)
    y = _mm(x, w_ref[0])

    @pl.when(pl.program_id(0) == gate_index)
    def _():
        o_ref[0] = _silu(y)

    @pl.when(pl.program_id(0) != gate_index)
    def _():
        o_ref[0] = y


def _mixproj(u, mu4, w4, seq, tm, gate_index):
    t, d = u.shape
    nproj, _, n = w4.shape
    rb = tm // SUBLANES
    return pl.pallas_call(
        functools.partial(_mixproj_kernel, steps_per_seq=seq // tm, gate_index=gate_index),
        grid=(nproj, t // tm),
        in_specs=[pl.BlockSpec((tm, d), lambda j, i: (i, 0)),
                  pl.BlockSpec((SUBLANES, d), lambda j, i: (jnp.maximum(i * rb - 1, 0), 0)),
                  pl.BlockSpec((1, 1, d), lambda j, i: (j, 0, 0)),
                  pl.BlockSpec((1, d, n), lambda j, i: (j, 0, 0))],
        out_specs=pl.BlockSpec((1, tm, n), lambda j, i: (j, i, 0)),
        out_shape=jax.ShapeDtypeStruct((nproj, t, n), F32),
        compiler_params=_cparams("parallel", "parallel"),
        name="l1_mixproj",
    )(u, u, mu4, w4)


def _lora_kernel(u_ref, up_ref, mu_ref, w1_ref, w2_ref, w0_ref, a1_ref, a2_ref, a0_ref,
                 lw_ref, a_ref, *, steps_per_seq):
    u = u_ref[...]
    prev = _shifted(u, up_ref, pl.program_id(0) % steps_per_seq == 0)
    delta = prev - u
    x_w = (u + delta * mu_ref[0:1, :]).astype(BF16)
    x_a = (u + delta * mu_ref[1:2, :]).astype(BF16)
    zw = w0_ref[...] + _mm(jnp.tanh(_mm(x_w, w1_ref[...])).astype(BF16), w2_ref[...])
    lw_ref[...] = -jnp.exp(-_softplus(-zw) - 0.5)
    za = a0_ref[...] + _mm(_mm(x_a, a1_ref[...]).astype(BF16), a2_ref[...])
    a_ref[...] = _sigmoid(za)


def _lora(u, mu2, w1, w2, w0, a1, a2, a0, seq, tm):
    t, d = u.shape
    n = w2.shape[1]
    rb = tm // SUBLANES
    full = lambda i: (0, 0)
    row = lambda i: (i, 0)
    return pl.pallas_call(
        functools.partial(_lora_kernel, steps_per_seq=seq // tm),
        grid=(t // tm,),
        in_specs=[pl.BlockSpec((tm, d), row),
                  pl.BlockSpec((SUBLANES, d), lambda i: (jnp.maximum(i * rb - 1, 0), 0)),
                  pl.BlockSpec(mu2.shape, full),
                  pl.BlockSpec(w1.shape, full), pl.BlockSpec(w2.shape, full),
                  pl.BlockSpec((1, n), full),
                  pl.BlockSpec(a1.shape, full), pl.BlockSpec(a2.shape, full),
                  pl.BlockSpec((1, n), full)],
        out_specs=[pl.BlockSpec((tm, n), row), pl.BlockSpec((tm, n), row)],
        out_shape=[jax.ShapeDtypeStruct((t, n), F32), jax.ShapeDtypeStruct((t, n), F32)],
        compiler_params=_cparams("parallel"),
        name="l1_lora",
    )(u, u, mu2, w1, w2, w0, a1, a2, a0)


def _rwkv_kernel(r_ref, k_ref, v_ref, g_ref, lw_ref, a_ref, kk_ref, ka_ref, rk_ref, lg_ref,
                 lb_ref, o_ref, st, rw_s, kkw_s, kt_s, bt_s, kend_s, bend_s, vb_s, bon_s, dec_s,
                 rt_s, y0_s, q_s, n_s, *, pairs):
    tc = lw_ref.shape[0]
    L = CHUNK
    hd = RW_HEAD_DIM
    nchunks = tc // L

    @pl.when(pl.program_id(2) == 0)
    def _():
        st[...] = jnp.zeros_like(st)

    row = lax.broadcasted_iota(jnp.int32, (L, LANES), 0)
    lane = lax.broadcasted_iota(jnp.int32, (L, LANES), 1)
    lane_h = lane & (hd - 1)
    strict = lane_h < row
    incl = lane_h <= row
    eye = jnp.where(lane_h == row, 1.0, 0.0).astype(F32)
    head_a = lane < hd
    row2 = lax.broadcasted_iota(jnp.int32, (2 * L, LANES), 0)
    lane2 = lax.broadcasted_iota(jnp.int32, (2 * L, LANES), 1)
    same_head = (row2 < hd) == (lane2 < hd)
    ones_bd = jnp.where(same_head, 1.0, 0.0).astype(BF16)
    ones_bd2 = jnp.concatenate([ones_bd, ones_bd], axis=0)
    ri = lax.broadcasted_iota(jnp.int32, (L, L), 0)
    ci = lax.broadcasted_iota(jnp.int32, (L, L), 1)
    tril = jnp.where(ri >= ci, 1.0, 0.0).astype(F32)

    def split(x):
        x = x.astype(F32)
        return jnp.concatenate([jnp.where(head_a, x, 0.0), jnp.where(head_a, 0.0, x)],
                               axis=0).astype(BF16)

    pcols = [slice(p * LANES, (p + 1) * LANES) for p in range(pairs)]

    def headsum(x):
        xs = jnp.concatenate([x[:, c] for c in pcols], axis=0)
        hi = xs.astype(BF16)
        lo = (xs - hi.astype(F32)).astype(BF16)
        s = _mm(jnp.concatenate([hi, lo], axis=1), ones_bd2)
        n = x.shape[0]
        return jnp.concatenate([s[p * n:(p + 1) * n] for p in range(pairs)], axis=1)

    def prep(c, carry):
        rows = pl.ds(pl.multiple_of(c * L, L), L)
        r = r_ref[0, rows, :]
        kraw = k_ref[0, rows, :]
        v = v_ref[0, rows, :]
        lw = lw_ref[rows, :]
        a = a_ref[rows, :]
        kkp = kraw * kk_ref[...]
        kk = kkp / jnp.maximum(jnp.sqrt(headsum(kkp * kkp)), 1e-12)
        k = kraw * (1.0 + (a - 1.0) * ka_ref[...])
        b = kk * a
        bon_s[rows, :] = headsum(r * k * rk_ref[...]) * v
        cum = _mm_exact(tril, lw)
        total = cum[L - 1:L, :]
        w_out = jnp.exp(-cum)
        w_end = jnp.exp(total - cum)
        rw_s[rows, :] = (r * jnp.exp(cum)).astype(BF16)
        kkw_s[rows, :] = (kk * jnp.exp(cum - lw)).astype(BF16)
        kt_s[rows, :] = (k * w_out).astype(BF16)
        bt_s[rows, :] = (b * w_out).astype(BF16)
        kend_s[rows, :] = (k * w_end).astype(BF16)
        bend_s[rows, :] = (b * w_end).astype(BF16)
        vb_s[rows, :] = v.astype(BF16)
        dec_s[c] = jnp.broadcast_to(jnp.exp(total), (SUBLANES, total.shape[1]))
        return carry

    lax.fori_loop(0, nchunks, prep, 0)

    def build(it, carry):
        jobs = []
        for cc in range(BUILD_CHUNKS):
            c = it * BUILD_CHUNKS + cc
            rows = pl.ds(pl.multiple_of(c * L, L), L)
            jobs += [(c, p, rows, pcols[p]) for p in range(pairs)]
        J = range(len(jobs))
        ld = lambda ref: [ref[rows, cols] for (_, _, rows, cols) in jobs]
        rw, kkw, vb, bend = ld(rw_s), ld(kkw_s), ld(vb_s), ld(bend_s)
        zk = [split(t) for t in ld(kt_s)]
        zb = [split(t) for t in ld(bt_s)]
        lhs = [jnp.concatenate([kkw[j], rw[j]], axis=0) for j in J]
        p_k = [_mm_nt(lhs[j], zk[j]) for j in J]
        p_b = [_mm_nt(lhs[j], zb[j]) for j in J]
        amat = [jnp.where(strict, p_b[j][0:L], 0.0) for j in J]
        x = [eye - amat[j] for j in J]
        m = [_mm(amat[j].astype(BF16), split(amat[j])) for j in J]
        for _ in range(L.bit_length() - 3):
            xm = [_mm(jnp.concatenate([x[j], m[j]], axis=0).astype(BF16), split(m[j])) for j in J]
            x = [x[j] + xm[j][0:L] for j in J]
            m = [xm[j][L:2 * L] for j in J]
        x = [(x[j] + _mm(x[j].astype(BF16), split(m[j]))).astype(BF16) for j in J]
        a_kr = [jnp.concatenate([jnp.where(strict, p_k[j][0:L], 0.0),
                                 jnp.where(incl, p_k[j][L:2 * L], 0.0)], axis=0).astype(BF16)
                for j in J]
        akv = [_mm(a_kr[j], split(vb[j])) for j in J]
        ku = [_mm(x[j], jnp.concatenate([split(kkw[j]), split(akv[j][0:L])], axis=1)) for j in J]
        a_rb = [jnp.where(incl, p_b[j][L:2 * L], 0.0).astype(BF16) for j in J]
        ru = [_mm(a_rb[j], jnp.concatenate([split(ku[j][:, 0:LANES]), split(ku[j][:, LANES:])],
                                           axis=1)) for j in J]
        for j, (_, _, rows, cols) in enumerate(jobs):
            rt_s[rows, cols] = (rw[j].astype(F32) - ru[j][:, 0:LANES]).astype(BF16)
            y0_s[rows, cols] = akv[j][L:2 * L] - ru[j][:, LANES:]
        vu_t = [jnp.concatenate([vb[j].astype(F32), -ku[j][:, LANES:]], axis=0).T.astype(BF16)
                for j in J]
        kt_t = [ku[j][:, 0:LANES].T.astype(BF16) for j in J]
        kb_end = [jnp.concatenate([t, bend[j]], axis=0) for j, t in enumerate(ld(kend_s))]
        nmat = [_mm(vu_t[j], kb_end[j]) for j in J]
        qmat = [_mm(kt_t[j], bend[j]) for j in J]
        for j, (c, p, _, _) in enumerate(jobs):
            n_s[c, p] = jnp.where(same_head, nmat[j], 0.0)
            q_s[c, p] = jnp.where(same_head, qmat[j], 0.0).astype(BF16)
        return carry

    lax.fori_loop(0, nchunks // BUILD_CHUNKS, build, 0)

    def scan(c, carry):
        rows = pl.ds(pl.multiple_of(c * L, L), L)
        dec = dec_s[c]
        for p in range(pairs):
            s = st[p]
            sb = s.astype(BF16)
            y0_s[rows, pcols[p]] = y0_s[rows, pcols[p]] + _mm_nt(rt_s[rows, pcols[p]], sb)
            st[p] = s * dec[0:1, pcols[p]] + n_s[c, p] - _mm(sb, q_s[c, p])
        return carry

    lax.fori_loop(0, nchunks, scan, 0)

    def finish(c, carry):
        rows = pl.ds(pl.multiple_of(c * L, L), L)
        y = y0_s[rows, :]
        yc = y - headsum(y) * (1.0 / hd)
        var = headsum(yc * yc) * (1.0 / hd)
        yn = yc * lax.rsqrt(var + GN_EPS) * lg_ref[...] + lb_ref[...]
        o_ref[rows, :] = ((yn + bon_s[rows, :]) * g_ref[0, rows, :]).astype(o_ref.dtype)
        return carry

    lax.fori_loop(0, nchunks, finish, 0)


def _rwkv(rkvg, lw, a, k_k, k_a, r_k, lnx_g, lnx_b, bsz, seq, tc, pairs):
    t, width = lw.shape
    wb = pairs * LANES
    nt = seq // tc
    nchunks = tc // CHUNK

    def proj(j):
        return pl.BlockSpec((1, tc, wb), lambda b, p, i: (j, b * nt + i, p))

    tok = pl.BlockSpec((tc, wb), lambda b, p, i: (b * nt + i, p))
    par = pl.BlockSpec((1, wb), lambda b, p, i: (0, p))
    act16 = pltpu.VMEM((tc, wb), BF16)
    act32 = pltpu.VMEM((tc, wb), F32)
    return pl.pallas_call(
        functools.partial(_rwkv_kernel, pairs=pairs),
        grid=(bsz, width // wb, nt),
        in_specs=[proj(0), proj(1), proj(2), proj(3), tok, tok, par, par, par, par, par],
        out_specs=tok,
        out_shape=jax.ShapeDtypeStruct((t, width), BF16),
        scratch_shapes=[pltpu.VMEM((pairs, LANES, LANES), F32),
                        act16, act16, act16, act16, act16, act16, act16,
                        act32,
                        pltpu.VMEM((nchunks, SUBLANES, wb), F32),
                        act16, act32,
                        pltpu.VMEM((nchunks, pairs, LANES, LANES), BF16),
                        pltpu.VMEM((nchunks, pairs, LANES, LANES), F32)],
        compiler_params=_cparams("parallel", "parallel", "arbitrary"),
        name="l1_rwkv7",
    )(rkvg, rkvg, rkvg, rkvg, lw, a, k_k, k_a, r_k, lnx_g, lnx_b)


def _outproj1_kernel(y_ref, w_ref, h_ref, g_ref, o_ref):
    h = h_ref[...] + _mm(y_ref[...], w_ref[...])
    o_ref[...] = _rms(h, g_ref[...])


def _outproj1(y, w, h, g, tm):
    t, d = h.shape
    row = lambda i: (i, 0)
    full = lambda i: (0, 0)
    return pl.pallas_call(
        _outproj1_kernel,
        grid=(t // tm,),
        in_specs=[pl.BlockSpec((tm, y.shape[1]), row), pl.BlockSpec(w.shape, full),
                  pl.BlockSpec((tm, d), row), pl.BlockSpec((1, d), full)],
        out_specs=pl.BlockSpec((tm, d), row),
        out_shape=jax.ShapeDtypeStruct((t, d), F32),
        compiler_params=_cparams("parallel"),
        name="l1_outproj_final",
    )(y, w, h, g)


def _tile(n, pref):
    t = min(n, pref)
    assert n % t == 0, (n, pref)
    return t


def _layer0(x2, bsz, seq, norm_g, w_in, conv_w, conv_b, w_a, b_a, w_x, b_x, lam, lb_logits,
            hg_g, w_out, next_norm_g):
    t, d = x2.shape
    width = conv_w.shape[1]
    row = lambda p: p.reshape(1, -1)
    z = _norm_matmul(x2, row(norm_g), w_in.astype(BF16), _tile(t, 512), _tile(w_in.shape[1], 2048))
    ya = _rglru(z, conv_w, row(conv_b), w_a.astype(BF16), row(b_a), w_x.astype(BF16), row(b_x),
                row(lam), bsz, seq, _tile(seq, 512))
    yb = _hgrn2(z, lb_logits, row(hg_g), bsz, seq, _tile(seq, 512), 2 * width)
    w_out = w_out.astype(BF16)
    return _outproj0(ya, yb, w_out[:width], w_out[width:], x2, row(next_norm_g), _tile(t, 512))


def _layer1(h, u, bsz, seq, mu, w_r, w_k, w_v, w_g, w0, w1, w2, a0, a1, a2, k_k, k_a, r_k,
            lnx_g, lnx_b, w_o, final_g):
    t, d = h.shape
    row = lambda p: p.reshape(1, -1)
    mu4 = jnp.stack([mu[0], mu[2], mu[3], mu[5]])[:, None, :]
    w4 = jnp.stack([w_r, w_k, w_v, w_g]).astype(BF16)
    tm = _tile(seq, 512)
    rkvg = _mixproj(u, mu4, w4, seq, tm, gate_index=3)
    lw, a = _lora(u, jnp.stack([mu[1], mu[4]]), w1.astype(BF16), w2.astype(BF16), row(w0),
                  a1.astype(BF16), a2.astype(BF16), row(a0), seq, tm)
    y = _rwkv(rkvg, lw, a, row(k_k), row(k_a), row(r_k), row(lnx_g), row(lnx_b), bsz, seq,
              _tile(seq, 512), pairs=4)
    return _outproj1(y, w_o.astype(BF16), h, row(final_g), _tile(t, 512))


def kernel(x, ab_norm_g, ab_w_in, rg_conv_w, rg_conv_b, rg_w_a, rg_b_a, rg_w_x, rg_b_x, rg_lambda, hg_lb_logits, hg_norm_g, ab_w_out, c_norm_g, c_mu, c_w_r, c_w_k, c_w_v, c_w_g, c_w0, c_w1, c_w2, c_a0, c_a1, c_a2, c_k_k, c_k_a, c_r_k, c_lnx_g, c_lnx_b, c_w_o, final_g):
    bsz, seq, d = x.shape
    assert ab_norm_g.shape[0] == 1 and c_norm_g.shape[0] == 1, "two-layer trunk only"
    x2 = x.reshape(bsz * seq, d)
    h, u = _layer0(x2, bsz, seq, ab_norm_g[0], ab_w_in[0], rg_conv_w[0], rg_conv_b[0], rg_w_a[0],
                   rg_b_a[0], rg_w_x[0], rg_b_x[0], rg_lambda[0], hg_lb_logits, hg_norm_g[0],
                   ab_w_out[0], c_norm_g[0])
    out = _layer1(h, u, bsz, seq, c_mu[0], c_w_r[0], c_w_k[0], c_w_v[0], c_w_g[0], c_w0[0],
                  c_w1[0], c_w2[0], c_a0[0], c_a1[0], c_a2[0], c_k_k[0], c_k_a[0],
                  c_r_k[0].reshape(-1), c_lnx_g[0], c_lnx_b[0], c_w_o[0], final_g)
    return out.reshape(bsz, seq, d)
```

```python
import functools

import jax
import jax.numpy as jnp
from jax import lax
from jax.experimental import pallas as pl
from jax.experimental.pallas import tpu as pltpu

F32 = jnp.float32
BF16 = jnp.bfloat16

RMS_EPS = 1e-6
GN_EPS = 64e-5
RG_C = 8.0
RG_BLOCKS = 8
RG_CONV = 4
HG_HEAD_DIM = 128
RW_HEAD_DIM = 64
CHUNK = 64
BUILD_CHUNKS = 4
LANES = 128
SUBLANES = 8
VMEM_LIMIT = 56 * 1024 * 1024


def _cparams(*sem):
    return pltpu.CompilerParams(dimension_semantics=sem, vmem_limit_bytes=VMEM_LIMIT)


def _mm(a, b):
    return jnp.dot(a, b, preferred_element_type=F32)


def _mm_nt(a, b):
    return lax.dot_general(a, b, (((1,), (1,)), ((), ())), preferred_element_type=F32)


def _mm_exact(a, b):
    return jnp.dot(a, b, preferred_element_type=F32, precision=lax.Precision.HIGHEST)


def _rms(x, g):
    return x * lax.rsqrt(jnp.mean(x * x, axis=-1, keepdims=True) + RMS_EPS) * g


def _sigmoid(x):
    return 1.0 / (1.0 + jnp.exp(-x))


def _silu(x):
    return x * _sigmoid(x)


def _softplus(x):
    return jnp.maximum(x, 0.0) + jnp.log(1.0 + jnp.exp(-jnp.abs(x)))


def _norm_matmul_kernel(x_ref, g_ref, w_ref, o_ref):
    xn = _rms(x_ref[...], g_ref[...]).astype(BF16)
    o_ref[...] = _mm(xn, w_ref[...])


def _norm_matmul(x, g, w, tm, tn):
    t, d = x.shape
    n = w.shape[1]
    return pl.pallas_call(
        _norm_matmul_kernel,
        grid=(n // tn, t // tm),
        in_specs=[pl.BlockSpec((tm, d), lambda j, i: (i, 0)),
                  pl.BlockSpec((1, d), lambda j, i: (0, 0)),
                  pl.BlockSpec((d, tn), lambda j, i: (0, j))],
        out_specs=pl.BlockSpec((tm, tn), lambda j, i: (i, j)),
        out_shape=jax.ShapeDtypeStruct((t, n), F32),
        compiler_params=_cparams("parallel", "parallel"),
        name="l0_norm_inproj",
    )(x, g, w)


def _rglru_kernel(xa_ref, ga_ref, cw_ref, cb_ref, wa_ref, ba_ref, wx_ref, bx_ref, lam_ref,
                  o_ref, xbuf, a_s, u_s, hcar):
    ts, width = xa_ref.shape
    bd = width // RG_BLOCKS

    @pl.when(pl.program_id(1) == 0)
    def _():
        xbuf[0:SUBLANES, :] = jnp.zeros((SUBLANES, width), F32)
        hcar[...] = jnp.zeros_like(hcar)

    xa = xa_ref[...]
    xbuf[SUBLANES:SUBLANES + ts, :] = xa
    xc = cb_ref[...] + cw_ref[RG_CONV - 1:RG_CONV, :] * xa
    for j in range(1, RG_CONV):
        xc = xc + cw_ref[RG_CONV - 1 - j:RG_CONV - j, :] * xbuf[pl.ds(SUBLANES - j, ts), :]
    xbuf[0:SUBLANES, :] = xa[ts - SUBLANES:ts, :]

    xcb = xc.astype(BF16)
    gr, gi = [], []
    for n in range(RG_BLOCKS):
        blk = xcb[:, n * bd:(n + 1) * bd]
        gr.append(_mm(blk, wa_ref[n]))
        gi.append(_mm(blk, wx_ref[n]))
    gate_r = _sigmoid(jnp.concatenate(gr, axis=-1) + ba_ref[...])
    gate_i = _sigmoid(jnp.concatenate(gi, axis=-1) + bx_ref[...])
    log_a = (-RG_C) * gate_r * _softplus(-lam_ref[...])
    a = jnp.exp(log_a)
    u = jnp.sqrt(-jnp.tanh(log_a) * (1.0 + a * a)) * (gate_i * xc)

    row = lax.broadcasted_iota(jnp.int32, (ts, width), 0) & (SUBLANES - 1)
    d = 1
    while d < SUBLANES:
        keep = row >= d
        u = jnp.where(keep, a * pltpu.roll(u, d, axis=0) + u, u)
        a = jnp.where(keep, a * pltpu.roll(a, d, axis=0), a)
        d *= 2
    a_s[...] = a
    u_s[...] = u

    def body(j, h):
        r0 = pl.multiple_of(j * SUBLANES, SUBLANES)
        hb = a_s[pl.ds(r0, SUBLANES), :] * h + u_s[pl.ds(r0, SUBLANES), :]
        u_s[pl.ds(r0, SUBLANES), :] = hb
        return hb[SUBLANES - 1:SUBLANES, :]

    hcar[0:1, :] = lax.fori_loop(0, ts // SUBLANES, body, hcar[0:1, :])
    o_ref[...] = (u_s[...] * _silu(ga_ref[...])).astype(o_ref.dtype)


def _rglru(z, conv_w, conv_b, w_a, b_a, w_x, b_x, lam, bsz, seq, ts):
    width = conv_w.shape[1]
    nt = seq // ts
    row = lambda b, i: (b * nt + i, 0)
    full2 = lambda b, i: (0, 0)
    full3 = lambda b, i: (0, 0, 0)
    return pl.pallas_call(
        _rglru_kernel,
        grid=(bsz, nt),
        in_specs=[pl.BlockSpec((ts, width), row),
                  pl.BlockSpec((ts, width), lambda b, i: (b * nt + i, 1)),
                  pl.BlockSpec(conv_w.shape, full2),
                  pl.BlockSpec((1, width), full2),
                  pl.BlockSpec(w_a.shape, full3),
                  pl.BlockSpec((1, width), full2),
                  pl.BlockSpec(w_x.shape, full3),
                  pl.BlockSpec((1, width), full2),
                  pl.BlockSpec((1, width), full2)],
        out_specs=pl.BlockSpec((ts, width), row),
        out_shape=jax.ShapeDtypeStruct((bsz * seq, width), BF16),
        scratch_shapes=[pltpu.VMEM((ts + SUBLANES, width), F32),
                        pltpu.VMEM((ts, width), F32),
                        pltpu.VMEM((ts, width), F32),
                        pltpu.VMEM((SUBLANES, width), F32)],
        compiler_params=_cparams("parallel", "arbitrary"),
        name="l0_rglru",
    )(z, z, conv_w, conv_b, w_a, b_a, w_x, b_x, lam)


def _hgrn2_kernel(q_ref, f_ref, v_ref, g_ref, lbl_ref, gn_ref, o_ref, st):
    tc = q_ref.shape[0]

    @pl.when(pl.program_id(2) == 0)
    def _():
        st[...] = jnp.zeros_like(st)

    lbl = lbl_ref[...]
    e = jnp.exp(lbl - jnp.max(lbl, axis=0, keepdims=True))
    lb = e[0:1, :] / jnp.sum(e, axis=0, keepdims=True)

    ri = lax.broadcasted_iota(jnp.int32, (CHUNK, CHUNK), 0)
    ci = lax.broadcasted_iota(jnp.int32, (CHUNK, CHUNK), 1)
    causal = ri >= ci
    tril = jnp.where(causal, 1.0, 0.0).astype(F32)

    C = range(tc // CHUNK)
    sl = [slice(n * CHUNK, (n + 1) * CHUNK) for n in C]
    sg = _sigmoid(f_ref[...])
    log_f = jnp.log(lb + (1.0 - lb) * sg)
    k_all = (1.0 - lb) * (1.0 - sg)
    cum_w = _mm_exact(tril, jnp.concatenate([log_f[s] for s in sl], axis=1))
    cum = [cum_w[:, n * HG_HEAD_DIM:(n + 1) * HG_HEAD_DIM] for n in C]
    total = [cum[n][CHUNK - 1:CHUNK, :] for n in C]
    k = [k_all[s] for s in sl]
    v = [v_ref[s, :] for s in sl]
    q_dec = [(q_ref[sl[n], :] * jnp.exp(cum[n])).astype(BF16) for n in C]
    k_inv = [(k[n] * jnp.exp(-cum[n])).astype(BF16) for n in C]
    k_end = [(k[n] * jnp.exp(total[n] - cum[n])).astype(BF16) for n in C]
    scores = [jnp.where(causal, _mm_nt(q_dec[n], k_inv[n]), 0.0).astype(BF16) for n in C]
    upd = [_mm(v[n].T.astype(BF16), k_end[n]) for n in C]
    o = [_mm(scores[n], v[n].astype(BF16)) for n in C]
    state = st[...]
    states = []
    for n in C:
        states.append(state.astype(BF16))
        state = state * jnp.exp(total[n]) + upd[n]
    st[...] = state
    o = jnp.concatenate([o[n] + _mm_nt(q_dec[n], states[n]) for n in C], axis=0)
    o = o * lax.rsqrt(jnp.mean(o * o, axis=-1, keepdims=True) + RMS_EPS) * gn_ref[...]
    o_ref[...] = (o * _silu(g_ref[...])).astype(o_ref.dtype)


def _hgrn2(z, lb_logits, hg_g, bsz, seq, tc, col0):
    width = lb_logits.shape[1]
    heads = width // HG_HEAD_DIM
    nt = seq // tc

    def col(k):
        return lambda b, h, i: (b * nt + i, (col0 + k * width) // HG_HEAD_DIM + h)

    return pl.pallas_call(
        _hgrn2_kernel,
        grid=(bsz, heads, nt),
        in_specs=[pl.BlockSpec((tc, HG_HEAD_DIM), col(0)),
                  pl.BlockSpec((tc, HG_HEAD_DIM), col(1)),
                  pl.BlockSpec((tc, HG_HEAD_DIM), col(2)),
                  pl.BlockSpec((tc, HG_HEAD_DIM), col(3)),
                  pl.BlockSpec((lb_logits.shape[0], HG_HEAD_DIM), lambda b, h, i: (0, h)),
                  pl.BlockSpec((1, HG_HEAD_DIM), lambda b, h, i: (0, 0))],
        out_specs=pl.BlockSpec((tc, HG_HEAD_DIM), lambda b, h, i: (b * nt + i, h)),
        out_shape=jax.ShapeDtypeStruct((bsz * seq, width), BF16),
        scratch_shapes=[pltpu.VMEM((HG_HEAD_DIM, HG_HEAD_DIM), F32)],
        compiler_params=_cparams("parallel", "parallel", "arbitrary"),
        name="l0_hgrn2",
    )(z, z, z, z, lb_logits, hg_g)


def _outproj0_kernel(ya_ref, yb_ref, wa_ref, wb_ref, x_ref, g_ref, h_ref, u_ref):
    h = x_ref[...] + _mm(ya_ref[...], wa_ref[...]) + _mm(yb_ref[...], wb_ref[...])
    h_ref[...] = h
    u_ref[...] = _rms(h, g_ref[...])


def _outproj0(ya, yb, w_a, w_b, x, g, tm):
    t, d = x.shape
    row = lambda i: (i, 0)
    full = lambda i: (0, 0)
    return pl.pallas_call(
        _outproj0_kernel,
        grid=(t // tm,),
        in_specs=[pl.BlockSpec((tm, ya.shape[1]), row),
                  pl.BlockSpec((tm, yb.shape[1]), row),
                  pl.BlockSpec(w_a.shape, full),
                  pl.BlockSpec(w_b.shape, full),
                  pl.BlockSpec((tm, d), row),
                  pl.BlockSpec((1, d), full)],
        out_specs=[pl.BlockSpec((tm, d), row), pl.BlockSpec((tm, d), row)],
        out_shape=[jax.ShapeDtypeStruct((t, d), F32), jax.ShapeDtypeStruct((t, d), F32)],
        compiler_params=_cparams("parallel"),
        name="l0_outproj",
    )(ya, yb, w_a, w_b, x, g)


def _shifted(u, up_ref, first_of_seq):
    prev_last = jnp.where(first_of_seq, 0.0, up_ref[SUBLANES - 1:SUBLANES, :])
    row = lax.broadcasted_iota(jnp.int32, u.shape, 0)
    return jnp.where(row == 0, prev_last, pltpu.roll(u, 1, axis=0))


def _mixproj_kernel(u_ref, up_ref, mu_ref, w_ref, o_ref, *, steps_per_seq, gate_index):
    u = u_ref[...]
    prev = _shifted(u, up_ref, pl.program_id(1) % steps_per_seq == 0)
    x = (u + (prev - u) * mu_ref[0]).astype(BF16)
    y = _mm(x, w_ref[0])

    @pl.when(pl.program_id(0) == gate_index)
    def _():
        o_ref[0] = _silu(y)

    @pl.when(pl.program_id(0) != gate_index)
    def _():
        o_ref[0] = y


def _mixproj(u, mu4, w4, seq, tm, gate_index):
    t, d = u.shape
    nproj, _, n = w4.shape
    rb = tm // SUBLANES
    return pl.pallas_call(
        functools.partial(_mixproj_kernel, steps_per_seq=seq // tm, gate_index=gate_index),
        grid=(nproj, t // tm),
        in_specs=[pl.BlockSpec((tm, d), lambda j, i: (i, 0)),
                  pl.BlockSpec((SUBLANES, d), lambda j, i: (jnp.maximum(i * rb - 1, 0), 0)),
                  pl.BlockSpec((1, 1, d), lambda j, i: (j, 0, 0)),
                  pl.BlockSpec((1, d, n), lambda j, i: (j, 0, 0))],
        out_specs=pl.BlockSpec((1, tm, n), lambda j, i: (j, i, 0)),
        out_shape=jax.ShapeDtypeStruct((nproj, t, n), F32),
        compiler_params=_cparams("parallel", "parallel"),
        name="l1_mixproj",
    )(u, u, mu4, w4)


def _lora_kernel(u_ref, up_ref, mu_ref, w1_ref, w2_ref, w0_ref, a1_ref, a2_ref, a0_ref,
                 lw_ref, a_ref, *, steps_per_seq):
    u = u_ref[...]
    prev = _shifted(u, up_ref, pl.program_id(0) % steps_per_seq == 0)
    delta = prev - u
    x_w = (u + delta * mu_ref[0:1, :]).astype(BF16)
    x_a = (u + delta * mu_ref[1:2, :]).astype(BF16)
    zw = w0_ref[...] + _mm(jnp.tanh(_mm(x_w, w1_ref[...])).astype(BF16), w2_ref[...])
    lw_ref[...] = -jnp.exp(-_softplus(-zw) - 0.5)
    za = a0_ref[...] + _mm(_mm(x_a, a1_ref[...]).astype(BF16), a2_ref[...])
    a_ref[...] = _sigmoid(za)


def _lora(u, mu2, w1, w2, w0, a1, a2, a0, seq, tm):
    t, d = u.shape
    n = w2.shape[1]
    rb = tm // SUBLANES
    full = lambda i: (0, 0)
    row = lambda i: (i, 0)
    return pl.pallas_call(
        functools.partial(_lora_kernel, steps_per_seq=seq // tm),
        grid=(t // tm,),
        in_specs=[pl.BlockSpec((tm, d), row),
                  pl.BlockSpec((SUBLANES, d), lambda i: (jnp.maximum(i * rb - 1, 0), 0)),
                  pl.BlockSpec(mu2.shape, full),
                  pl.BlockSpec(w1.shape, full), pl.BlockSpec(w2.shape, full),
                  pl.BlockSpec((1, n), full),
                  pl.BlockSpec(a1.shape, full), pl.BlockSpec(a2.shape, full),
                  pl.BlockSpec((1, n), full)],
        out_specs=[pl.BlockSpec((tm, n), row), pl.BlockSpec((tm, n), row)],
        out_shape=[jax.ShapeDtypeStruct((t, n), F32), jax.ShapeDtypeStruct((t, n), F32)],
        compiler_params=_cparams("parallel"),
        name="l1_lora",
    )(u, u, mu2, w1, w2, w0, a1, a2, a0)


def _rwkv_kernel(r_ref, k_ref, v_ref, g_ref, lw_ref, a_ref, kk_ref, ka_ref, rk_ref, lg_ref,
                 lb_ref, o_ref, st, rw_s, kkw_s, kt_s, bt_s, kend_s, bend_s, vb_s, bon_s, dec_s,
                 rt_s, y0_s, q_s, n_s, *, pairs):
    tc = lw_ref.shape[0]
    L = CHUNK
    hd = RW_HEAD_DIM
    nchunks = tc // L

    @pl.when(pl.program_id(2) == 0)
    def _():
        st[...] = jnp.zeros_like(st)

    row = lax.broadcasted_iota(jnp.int32, (L, LANES), 0)
    lane = lax.broadcasted_iota(jnp.int32, (L, LANES), 1)
    lane_h = lane & (hd - 1)
    strict = lane_h < row
    incl = lane_h <= row
    eye = jnp.where(lane_h == row, 1.0, 0.0).astype(F32)
    head_a = lane < hd
    row2 = lax.broadcasted_iota(jnp.int32, (2 * L, LANES), 0)
    lane2 = lax.broadcasted_iota(jnp.int32, (2 * L, LANES), 1)
    same_head = (row2 < hd) == (lane2 < hd)
    ones_bd = jnp.where(same_head, 1.0, 0.0).astype(BF16)
    ones_bd2 = jnp.concatenate([ones_bd, ones_bd], axis=0)
    ri = lax.broadcasted_iota(jnp.int32, (L, L), 0)
    ci = lax.broadcasted_iota(jnp.int32, (L, L), 1)
    tril = jnp.where(ri >= ci, 1.0, 0.0).astype(F32)

    def split(x):
        x = x.astype(F32)
        return jnp.concatenate([jnp.where(head_a, x, 0.0), jnp.where(head_a, 0.0, x)],
                               axis=0).astype(BF16)

    pcols = [slice(p * LANES, (p + 1) * LANES) for p in range(pairs)]

    def headsum(x):
        xs = jnp.concatenate([x[:, c] for c in pcols], axis=0)
        hi = xs.astype(BF16)
        lo = (xs - hi.astype(F32)).astype(BF16)
        s = _mm(jnp.concatenate([hi, lo], axis=1), ones_bd2)
        n = x.shape[0]
        return jnp.concatenate([s[p * n:(p + 1) * n] for p in range(pairs)], axis=1)

    def prep(c, carry):
        rows = pl.ds(pl.multiple_of(c * L, L), L)
        r = r_ref[0, rows, :]
        kraw = k_ref[0, rows, :]
        v = v_ref[0, rows, :]
        lw = lw_ref[rows, :]
        a = a_ref[rows, :]
        kkp = kraw * kk_ref[...]
        kk = kkp / jnp.maximum(jnp.sqrt(headsum(kkp * kkp)), 1e-12)
        k = kraw * (1.0 + (a - 1.0) * ka_ref[...])
        b = kk * a
        bon_s[rows, :] = headsum(r * k * rk_ref[...]) * v
        cum = _mm_exact(tril, lw)
        total = cum[L - 1:L, :]
        w_out = jnp.exp(-cum)
        w_end = jnp.exp(total - cum)
        rw_s[rows, :] = (r * jnp.exp(cum)).astype(BF16)
        kkw_s[rows, :] = (kk * jnp.exp(cum - lw)).astype(BF16)
        kt_s[rows, :] = (k * w_out).astype(BF16)
        bt_s[rows, :] = (b * w_out).astype(BF16)
        kend_s[rows, :] = (k * w_end).astype(BF16)
        bend_s[rows, :] = (b * w_end).astype(BF16)
        vb_s[rows, :] = v.astype(BF16)
        dec_s[c] = jnp.broadcast_to(jnp.exp(total), (SUBLANES, total.shape[1]))
        return carry

    lax.fori_loop(0, nchunks, prep, 0)

    def build(it, carry):
        jobs = []
        for cc in range(BUILD_CHUNKS):
            c = it * BUILD_CHUNKS + cc
            rows = pl.ds(pl.multiple_of(c * L, L), L)
            jobs += [(c, p, rows, pcols[p]) for p in range(pairs)]
        J = range(len(jobs))
        ld = lambda ref: [ref[rows, cols] for (_, _, rows, cols) in jobs]
        rw, kkw, vb, bend = ld(rw_s), ld(kkw_s), ld(vb_s), ld(bend_s)
        zk = [split(t) for t in ld(kt_s)]
        zb = [split(t) for t in ld(bt_s)]
        lhs = [jnp.concatenate([kkw[j], rw[j]], axis=0) for j in J]
        p_k = [_mm_nt(lhs[j], zk[j]) for j in J]
        p_b = [_mm_nt(lhs[j], zb[j]) for j in J]
        amat = [jnp.where(strict, p_b[j][0:L], 0.0) for j in J]
        x = [eye - amat[j] for j in J]
        m = [_mm(amat[j].astype(BF16), split(amat[j])) for j in J]
        for _ in range(L.bit_length() - 3):
            xm = [_mm(jnp.concatenate([x[j], m[j]], axis=0).astype(BF16), split(m[j])) for j in J]
            x = [x[j] + xm[j][0:L] for j in J]
            m = [xm[j][L:2 * L] for j in J]
        x = [(x[j] + _mm(x[j].astype(BF16), split(m[j]))).astype(BF16) for j in J]
        a_kr = [jnp.concatenate([jnp.where(strict, p_k[j][0:L], 0.0),
                                 jnp.where(incl, p_k[j][L:2 * L], 0.0)], axis=0).astype(BF16)
                for j in J]
        akv = [_mm(a_kr[j], split(vb[j])) for j in J]
        ku = [_mm(x[j], jnp.concatenate([split(kkw[j]), split(akv[j][0:L])], axis=1)) for j in J]
        a_rb = [jnp.where(incl, p_b[j][L:2 * L], 0.0).astype(BF16) for j in J]
        ru = [_mm(a_rb[j], jnp.concatenate([split(ku[j][:, 0:LANES]), split(ku[j][:, LANES:])],
                                           axis=1)) for j in J]
        for j, (_, _, rows, cols) in enumerate(jobs):
            rt_s[rows, cols] = (rw[j].astype(F32) - ru[j][:, 0:LANES]).astype(BF16)
            y0_s[rows, cols] = akv[j][L:2 * L] - ru[j][:, LANES:]
        vu_t = [jnp.concatenate([vb[j].astype(F32), -ku[j][:, LANES:]], axis=0).T.astype(BF16)
                for j in J]
        kt_t = [ku[j][:, 0:LANES].T.astype(BF16) for j in J]
        kb_end = [jnp.concatenate([t, bend[j]], axis=0) for j, t in enumerate(ld(kend_s))]
        nmat = [_mm(vu_t[j], kb_end[j]) for j in J]
        qmat = [_mm(kt_t[j], bend[j]) for j in J]
        for j, (c, p, _, _) in enumerate(jobs):
            n_s[c, p] = jnp.where(same_head, nmat[j], 0.0)
            q_s[c, p] = jnp.where(same_head, qmat[j], 0.0).astype(BF16)
        return carry

    lax.fori_loop(0, nchunks // BUILD_CHUNKS, build, 0)

    def scan(c, carry):
        rows = pl.ds(pl.multiple_of(c * L, L), L)
        dec = dec_s[c]
        for p in range(pairs):
            s = st[p]
            sb = s.astype(BF16)
            y0_s[rows, pcols[p]] = y0_s[rows, pcols[p]] + _mm_nt(rt_s[rows, pcols[p]], sb)
            st[p] = s * dec[0:1, pcols[p]] + n_s[c, p] - _mm(sb, q_s[c, p])
        return carry

    lax.fori_loop(0, nchunks, scan, 0)

    slab = BUILD_CHUNKS * L

    def finish(i, carry):
        rows = pl.ds(pl.multiple_of(i * slab, slab), slab)
        y = y0_s[rows, :]
        yc = y - headsum(y) * (1.0 / hd)
        var = headsum(yc * yc) * (1.0 / hd)
        yn = yc * lax.rsqrt(var + GN_EPS) * lg_ref[...] + lb_ref[...]
        o_ref[rows, :] = ((yn + bon_s[rows, :]) * g_ref[0, rows, :]).astype(o_ref.dtype)
        return carry

    lax.fori_loop(0, tc // slab, finish, 0)


def _rwkv(rkvg, lw, a, k_k, k_a, r_k, lnx_g, lnx_b, bsz, seq, tc, pairs):
    t, width = lw.shape
    wb = pairs * LANES
    nt = seq // tc
    nchunks = tc // CHUNK

    def proj(j):
        return pl.BlockSpec((1, tc, wb), lambda b, p, i: (j, b * nt + i, p))

    tok = pl.BlockSpec((tc, wb), lambda b, p, i: (b * nt + i, p))
    par = pl.BlockSpec((1, wb), lambda b, p, i: (0, p))
    act16 = pltpu.VMEM((tc, wb), BF16)
    act32 = pltpu.VMEM((tc, wb), F32)
    return pl.pallas_call(
        functools.partial(_rwkv_kernel, pairs=pairs),
        grid=(bsz, width // wb, nt),
        in_specs=[proj(0), proj(1), proj(2), proj(3), tok, tok, par, par, par, par, par],
        out_specs=tok,
        out_shape=jax.ShapeDtypeStruct((t, width), BF16),
        scratch_shapes=[pltpu.VMEM((pairs, LANES, LANES), F32),
                        act16, act16, act16, act16, act16, act16, act16,
                        act32,
                        pltpu.VMEM((nchunks, SUBLANES, wb), F32),
                        act16, act32,
                        pltpu.VMEM((nchunks, pairs, LANES, LANES), BF16),
                        pltpu.VMEM((nchunks, pairs, LANES, LANES), F32)],
        compiler_params=_cparams("parallel", "parallel", "arbitrary"),
        name="l1_rwkv7",
    )(rkvg, rkvg, rkvg, rkvg, lw, a, k_k, k_a, r_k, lnx_g, lnx_b)


def _outproj1_kernel(y_ref, w_ref, h_ref, g_ref, o_ref):
    h = h_ref[...] + _mm(y_ref[...], w_ref[...])
    o_ref[...] = _rms(h, g_ref[...])


def _outproj1(y, w, h, g, tm):
    t, d = h.shape
    row = lambda i: (i, 0)
    full = lambda i: (0, 0)
    return pl.pallas_call(
        _outproj1_kernel,
        grid=(t // tm,),
        in_specs=[pl.BlockSpec((tm, y.shape[1]), row), pl.BlockSpec(w.shape, full),
                  pl.BlockSpec((tm, d), row), pl.BlockSpec((1, d), full)],
        out_specs=pl.BlockSpec((tm, d), row),
        out_shape=jax.ShapeDtypeStruct((t, d), F32),
        compiler_params=_cparams("parallel"),
        name="l1_outproj_final",
    )(y, w, h, g)


def _tile(n, pref):
    t = min(n, pref)
    assert n % t == 0, (n, pref)
    return t


def _layer0(x2, bsz, seq, norm_g, w_in, conv_w, conv_b, w_a, b_a, w_x, b_x, lam, lb_logits,
            hg_g, w_out, next_norm_g):
    t, d = x2.shape
    width = conv_w.shape[1]
    row = lambda p: p.reshape(1, -1)
    z = _norm_matmul(x2, row(norm_g), w_in.astype(BF16), _tile(t, 512), _tile(w_in.shape[1], 2048))
    ya = _rglru(z, conv_w, row(conv_b), w_a.astype(BF16), row(b_a), w_x.astype(BF16), row(b_x),
                row(lam), bsz, seq, _tile(seq, 512))
    yb = _hgrn2(z, lb_logits, row(hg_g), bsz, seq, _tile(seq, 512), 2 * width)
    w_out = w_out.astype(BF16)
    return _outproj0(ya, yb, w_out[:width], w_out[width:], x2, row(next_norm_g), _tile(t, 512))


def _layer1(h, u, bsz, seq, mu, w_r, w_k, w_v, w_g, w0, w1, w2, a0, a1, a2, k_k, k_a, r_k,
            lnx_g, lnx_b, w_o, final_g):
    t, d = h.shape
    row = lambda p: p.reshape(1, -1)
    mu4 = jnp.stack([mu[0], mu[2], mu[3], mu[5]])[:, None, :]
    w4 = jnp.stack([w_r, w_k, w_v, w_g]).astype(BF16)
    tm = _tile(seq, 512)
    rkvg = _mixproj(u, mu4, w4, seq, tm, gate_index=3)
    lw, a = _lora(u, jnp.stack([mu[1], mu[4]]), w1.astype(BF16), w2.astype(BF16), row(w0),
                  a1.astype(BF16), a2.astype(BF16), row(a0), seq, tm)
    y = _rwkv(rkvg, lw, a, row(k_k), row(k_a), row(r_k), row(lnx_g), row(lnx_b), bsz, seq,
              _tile(seq, 512), pairs=4)
    return _outproj1(y, w_o.astype(BF16), h, row(final_g), _tile(t, 512))


def kernel(x, ab_norm_g, ab_w_in, rg_conv_w, rg_conv_b, rg_w_a, rg_b_a, rg_w_x, rg_b_x, rg_lambda, hg_lb_logits, hg_norm_g, ab_w_out, c_norm_g, c_mu, c_w_r, c_w_k, c_w_v, c_w_g, c_w0, c_w1, c_w2, c_a0, c_a1, c_a2, c_k_k, c_k_a, c_r_k, c_lnx_g, c_lnx_b, c_w_o, final_g):
    bsz, seq, d = x.shape
    assert ab_norm_g.shape[0] == 1 and c_norm_g.shape[0] == 1, "two-layer trunk only"
    x2 = x.reshape(bsz * seq, d)
    h, u = _layer0(x2, bsz, seq, ab_norm_g[0], ab_w_in[0], rg_conv_w[0], rg_conv_b[0], rg_w_a[0],
                   rg_b_a[0], rg_w_x[0], rg_b_x[0], rg_lambda[0], hg_lb_logits, hg_norm_g[0],
                   ab_w_out[0], c_norm_g[0])
    out = _layer1(h, u, bsz, seq, c_mu[0], c_w_r[0], c_w_k[0], c_w_v[0], c_w_g[0], c_w0[0],
                  c_w1[0], c_w2[0], c_a0[0], c_a1[0], c_a2[0], c_k_k[0], c_k_a[0],
                  c_r_k[0].reshape(-1), c_lnx_g[0], c_lnx_b[0], c_w_o[0], final_g)
    return out.reshape(bsz, seq, d)
```

```python
import functools

import jax
import jax.numpy as jnp
from jax import lax
from jax.experimental import pallas as pl
from jax.experimental.pallas import tpu as pltpu

F32 = jnp.float32
BF16 = jnp.bfloat16

RMS_EPS = 1e-6
GN_EPS = 64e-5
RG_C = 8.0
RG_BLOCKS = 8
RG_CONV = 4
HG_HEAD_DIM = 128
RW_HEAD_DIM = 64
CHUNK = 64
BUILD_CHUNKS = 4
FRONT_COLS = 512
LANES = 128
SUBLANES = 8
VMEM_LIMIT = 56 * 1024 * 1024


def _cparams(*sem):
    return pltpu.CompilerParams(dimension_semantics=sem, vmem_limit_bytes=VMEM_LIMIT)


def _mm(a, b):
    return jnp.dot(a, b, preferred_element_type=F32)


def _mm_nt(a, b):
    return lax.dot_general(a, b, (((1,), (1,)), ((), ())), preferred_element_type=F32)


def _mm_exact(a, b):
    return jnp.dot(a, b, preferred_element_type=F32, precision=lax.Precision.HIGHEST)


def _rms(x, g):
    return x * lax.rsqrt(jnp.mean(x * x, axis=-1, keepdims=True) + RMS_EPS) * g


def _sigmoid(x):
    return 1.0 / (1.0 + jnp.exp(-x))


def _silu(x):
    return x * _sigmoid(x)


def _softplus(x):
    return jnp.maximum(x, 0.0) + jnp.log(1.0 + jnp.exp(-jnp.abs(x)))


def _norm_matmul_kernel(x_ref, g_ref, w_ref, o_ref):
    xn = _rms(x_ref[...], g_ref[...]).astype(BF16)
    o_ref[...] = _mm(xn, w_ref[...])


def _norm_matmul(x, g, w, tm, tn):
    t, d = x.shape
    n = w.shape[1]
    return pl.pallas_call(
        _norm_matmul_kernel,
        grid=(n // tn, t // tm),
        in_specs=[pl.BlockSpec((tm, d), lambda j, i: (i, 0)),
                  pl.BlockSpec((1, d), lambda j, i: (0, 0)),
                  pl.BlockSpec((d, tn), lambda j, i: (0, j))],
        out_specs=pl.BlockSpec((tm, tn), lambda j, i: (i, j)),
        out_shape=jax.ShapeDtypeStruct((t, n), F32),
        compiler_params=_cparams("parallel", "parallel"),
        name="l0_norm_inproj",
    )(x, g, w)


def _rglru_kernel(xa_ref, ga_ref, cw_ref, cb_ref, wa_ref, ba_ref, wx_ref, bx_ref, lam_ref,
                  o_ref, xbuf, a_s, u_s, hcar):
    ts, width = xa_ref.shape
    bd = width // RG_BLOCKS

    @pl.when(pl.program_id(1) == 0)
    def _():
        xbuf[0:SUBLANES, :] = jnp.zeros((SUBLANES, width), F32)
        hcar[...] = jnp.zeros_like(hcar)

    xa = xa_ref[...]
    xbuf[SUBLANES:SUBLANES + ts, :] = xa
    xc = cb_ref[...] + cw_ref[RG_CONV - 1:RG_CONV, :] * xa
    for j in range(1, RG_CONV):
        xc = xc + cw_ref[RG_CONV - 1 - j:RG_CONV - j, :] * xbuf[pl.ds(SUBLANES - j, ts), :]
    xbuf[0:SUBLANES, :] = xa[ts - SUBLANES:ts, :]

    xcb = xc.astype(BF16)
    gr, gi = [], []
    for n in range(RG_BLOCKS):
        blk = xcb[:, n * bd:(n + 1) * bd]
        gr.append(_mm(blk, wa_ref[n]))
        gi.append(_mm(blk, wx_ref[n]))
    gate_r = _sigmoid(jnp.concatenate(gr, axis=-1) + ba_ref[...])
    gate_i = _sigmoid(jnp.concatenate(gi, axis=-1) + bx_ref[...])
    log_a = (-RG_C) * gate_r * _softplus(-lam_ref[...])
    a = jnp.exp(log_a)
    u = jnp.sqrt(-jnp.tanh(log_a) * (1.0 + a * a)) * (gate_i * xc)

    row = lax.broadcasted_iota(jnp.int32, (ts, width), 0) & (SUBLANES - 1)
    d = 1
    while d < SUBLANES:
        keep = row >= d
        u = jnp.where(keep, a * pltpu.roll(u, d, axis=0) + u, u)
        a = jnp.where(keep, a * pltpu.roll(a, d, axis=0), a)
        d *= 2
    a_s[...] = a
    u_s[...] = u

    def body(j, h):
        r0 = pl.multiple_of(j * SUBLANES, SUBLANES)
        hb = a_s[pl.ds(r0, SUBLANES), :] * h + u_s[pl.ds(r0, SUBLANES), :]
        u_s[pl.ds(r0, SUBLANES), :] = hb
        return hb[SUBLANES - 1:SUBLANES, :]

    hcar[0:1, :] = lax.fori_loop(0, ts // SUBLANES, body, hcar[0:1, :])
    o_ref[...] = (u_s[...] * _silu(ga_ref[...])).astype(o_ref.dtype)


def _rglru(z, conv_w, conv_b, w_a, b_a, w_x, b_x, lam, bsz, seq, ts):
    width = conv_w.shape[1]
    nt = seq // ts
    row = lambda b, i: (b * nt + i, 0)
    full2 = lambda b, i: (0, 0)
    full3 = lambda b, i: (0, 0, 0)
    return pl.pallas_call(
        _rglru_kernel,
        grid=(bsz, nt),
        in_specs=[pl.BlockSpec((ts, width), row),
                  pl.BlockSpec((ts, width), lambda b, i: (b * nt + i, 1)),
                  pl.BlockSpec(conv_w.shape, full2),
                  pl.BlockSpec((1, width), full2),
                  pl.BlockSpec(w_a.shape, full3),
                  pl.BlockSpec((1, width), full2),
                  pl.BlockSpec(w_x.shape, full3),
                  pl.BlockSpec((1, width), full2),
                  pl.BlockSpec((1, width), full2)],
        out_specs=pl.BlockSpec((ts, width), row),
        out_shape=jax.ShapeDtypeStruct((bsz * seq, width), BF16),
        scratch_shapes=[pltpu.VMEM((ts + SUBLANES, width), F32),
                        pltpu.VMEM((ts, width), F32),
                        pltpu.VMEM((ts, width), F32),
                        pltpu.VMEM((SUBLANES, width), F32)],
        compiler_params=_cparams("parallel", "arbitrary"),
        name="l0_rglru",
    )(z, z, conv_w, conv_b, w_a, b_a, w_x, b_x, lam)


def _hgrn2_kernel(q_ref, f_ref, v_ref, g_ref, lbl_ref, gn_ref, o_ref, st):
    tc = q_ref.shape[0]

    @pl.when(pl.program_id(2) == 0)
    def _():
        st[...] = jnp.zeros_like(st)

    lbl = lbl_ref[...]
    e = jnp.exp(lbl - jnp.max(lbl, axis=0, keepdims=True))
    lb = e[0:1, :] / jnp.sum(e, axis=0, keepdims=True)

    ri = lax.broadcasted_iota(jnp.int32, (CHUNK, CHUNK), 0)
    ci = lax.broadcasted_iota(jnp.int32, (CHUNK, CHUNK), 1)
    causal = ri >= ci
    tril = jnp.where(causal, 1.0, 0.0).astype(F32)

    C = range(tc // CHUNK)
    sl = [slice(n * CHUNK, (n + 1) * CHUNK) for n in C]
    sg = _sigmoid(f_ref[...])
    log_f = jnp.log(lb + (1.0 - lb) * sg)
    k_all = (1.0 - lb) * (1.0 - sg)
    cum_w = _mm_exact(tril, jnp.concatenate([log_f[s] for s in sl], axis=1))
    cum = [cum_w[:, n * HG_HEAD_DIM:(n + 1) * HG_HEAD_DIM] for n in C]
    total = [cum[n][CHUNK - 1:CHUNK, :] for n in C]
    k = [k_all[s] for s in sl]
    v = [v_ref[s, :] for s in sl]
    q_dec = [(q_ref[sl[n], :] * jnp.exp(cum[n])).astype(BF16) for n in C]
    k_inv = [(k[n] * jnp.exp(-cum[n])).astype(BF16) for n in C]
    k_end = [(k[n] * jnp.exp(total[n] - cum[n])).astype(BF16) for n in C]
    scores = [jnp.where(causal, _mm_nt(q_dec[n], k_inv[n]), 0.0).astype(BF16) for n in C]
    upd = [_mm(v[n].T.astype(BF16), k_end[n]) for n in C]
    o = [_mm(scores[n], v[n].astype(BF16)) for n in C]
    state = st[...]
    states = []
    for n in C:
        states.append(state.astype(BF16))
        state = state * jnp.exp(total[n]) + upd[n]
    st[...] = state
    o = jnp.concatenate([o[n] + _mm_nt(q_dec[n], states[n]) for n in C], axis=0)
    o = o * lax.rsqrt(jnp.mean(o * o, axis=-1, keepdims=True) + RMS_EPS) * gn_ref[...]
    o_ref[...] = (o * _silu(g_ref[...])).astype(o_ref.dtype)


def _hgrn2(z, lb_logits, hg_g, bsz, seq, tc, col0):
    width = lb_logits.shape[1]
    heads = width // HG_HEAD_DIM
    nt = seq // tc

    def col(k):
        return lambda b, h, i: (b * nt + i, (col0 + k * width) // HG_HEAD_DIM + h)

    return pl.pallas_call(
        _hgrn2_kernel,
        grid=(bsz, heads, nt),
        in_specs=[pl.BlockSpec((tc, HG_HEAD_DIM), col(0)),
                  pl.BlockSpec((tc, HG_HEAD_DIM), col(1)),
                  pl.BlockSpec((tc, HG_HEAD_DIM), col(2)),
                  pl.BlockSpec((tc, HG_HEAD_DIM), col(3)),
                  pl.BlockSpec((lb_logits.shape[0], HG_HEAD_DIM), lambda b, h, i: (0, h)),
                  pl.BlockSpec((1, HG_HEAD_DIM), lambda b, h, i: (0, 0))],
        out_specs=pl.BlockSpec((tc, HG_HEAD_DIM), lambda b, h, i: (b * nt + i, h)),
        out_shape=jax.ShapeDtypeStruct((bsz * seq, width), BF16),
        scratch_shapes=[pltpu.VMEM((HG_HEAD_DIM, HG_HEAD_DIM), F32)],
        compiler_params=_cparams("parallel", "parallel", "arbitrary"),
        name="l0_hgrn2",
    )(z, z, z, z, lb_logits, hg_g)


def _outproj0_kernel(ya_ref, yb_ref, wa_ref, wb_ref, x_ref, g_ref, h_ref, u_ref):
    h = x_ref[...] + _mm(ya_ref[...], wa_ref[...]) + _mm(yb_ref[...], wb_ref[...])
    h_ref[...] = h
    u_ref[...] = _rms(h, g_ref[...])


def _outproj0(ya, yb, w_a, w_b, x, g, tm):
    t, d = x.shape
    row = lambda i: (i, 0)
    full = lambda i: (0, 0)
    return pl.pallas_call(
        _outproj0_kernel,
        grid=(t // tm,),
        in_specs=[pl.BlockSpec((tm, ya.shape[1]), row),
                  pl.BlockSpec((tm, yb.shape[1]), row),
                  pl.BlockSpec(w_a.shape, full),
                  pl.BlockSpec(w_b.shape, full),
                  pl.BlockSpec((tm, d), row),
                  pl.BlockSpec((1, d), full)],
        out_specs=[pl.BlockSpec((tm, d), row), pl.BlockSpec((tm, d), row)],
        out_shape=[jax.ShapeDtypeStruct((t, d), F32), jax.ShapeDtypeStruct((t, d), F32)],
        compiler_params=_cparams("parallel"),
        name="l0_outproj",
    )(ya, yb, w_a, w_b, x, g)


def _shifted(u, up_ref, first_of_seq):
    prev_last = jnp.where(first_of_seq, 0.0, up_ref[SUBLANES - 1:SUBLANES, :])
    row = lax.broadcasted_iota(jnp.int32, u.shape, 0)
    return jnp.where(row == 0, prev_last, pltpu.roll(u, 1, axis=0))


def _pair_ones():
    r = lax.broadcasted_iota(jnp.int32, (2 * LANES, LANES), 0) & (LANES - 1)
    c = lax.broadcasted_iota(jnp.int32, (2 * LANES, LANES), 1)
    return jnp.where((r < RW_HEAD_DIM) == (c < RW_HEAD_DIM), 1.0, 0.0).astype(BF16)


def _headsum(x, ones2):
    n, nblk = x.shape[0], x.shape[1] // LANES
    xs = jnp.concatenate([x[:, i * LANES:(i + 1) * LANES] for i in range(nblk)], axis=0)
    hi = xs.astype(BF16)
    lo = (xs - hi.astype(F32)).astype(BF16)
    s = _mm(jnp.concatenate([hi, lo], axis=1), ones2)
    return jnp.concatenate([s[i * n:(i + 1) * n] for i in range(nblk)], axis=1)


def _front_kernel(u_ref, up_ref, mu_ref, wr_ref, wk_ref, wv_ref, wg_ref, w1_ref, w2_ref, w0_ref,
                  a1_ref, a2_ref, a0_ref, kk_ref, ka_ref, rk_ref,
                  rw_o, kkw_o, kt_o, bt_o, kend_o, bend_o, vb_o, bon_o, gate_o, dec_o,
                  *, steps_per_seq):
    tm = u_ref.shape[0]
    width = wr_ref.shape[1]
    L = CHUNK
    nch = tm // L
    u = u_ref[...]
    delta = _shifted(u, up_ref, pl.program_id(0) % steps_per_seq == 0) - u
    x_r, x_w, x_k, x_v, x_a, x_g = ((u + delta * mu_ref[i:i + 1, :]).astype(BF16)
                                    for i in range(6))
    zw = jnp.tanh(_mm(x_w, w1_ref[...])).astype(BF16)
    za = _mm(x_a, a1_ref[...]).astype(BF16)
    ones2 = _pair_ones()
    ri = lax.broadcasted_iota(jnp.int32, (L, L), 0)
    ci = lax.broadcasted_iota(jnp.int32, (L, L), 1)
    tril = jnp.where(ri >= ci, 1.0, 0.0).astype(F32)

    for g in range(width // FRONT_COLS):
        cols = slice(g * FRONT_COLS, (g + 1) * FRONT_COLS)
        r = _mm(x_r, wr_ref[:, cols])
        kraw = _mm(x_k, wk_ref[:, cols])
        v = _mm(x_v, wv_ref[:, cols])
        gate_o[:, cols] = _silu(_mm(x_g, wg_ref[:, cols]))
        lw = -jnp.exp(-_softplus(-(w0_ref[:, cols] + _mm(zw, w2_ref[:, cols]))) - 0.5)
        a = _sigmoid(a0_ref[:, cols] + _mm(za, a2_ref[:, cols]))
        kkp = kraw * kk_ref[:, cols]
        kk = kkp / jnp.maximum(jnp.sqrt(_headsum(kkp * kkp, ones2)), 1e-12)
        k = kraw * (1.0 + (a - 1.0) * ka_ref[:, cols])
        b = kk * a
        bon_o[:, cols] = _headsum(r * k * rk_ref[:, cols], ones2) * v
        cum_w = _mm_exact(tril, jnp.concatenate([lw[n * L:(n + 1) * L] for n in range(nch)], axis=1))
        cum = jnp.concatenate([cum_w[:, n * FRONT_COLS:(n + 1) * FRONT_COLS] for n in range(nch)],
                              axis=0)
        totals = [cum_w[L - 1:L, n * FRONT_COLS:(n + 1) * FRONT_COLS] for n in range(nch)]
        for n in range(nch):
            dec_o[n, :, cols] = jnp.exp(totals[n])
        total = jnp.concatenate([jnp.broadcast_to(t, (L, FRONT_COLS)) for t in totals], axis=0)
        w_out = jnp.exp(-cum)
        w_end = jnp.exp(total - cum)
        rw_o[:, cols] = (r * jnp.exp(cum)).astype(BF16)
        kkw_o[:, cols] = (kk * jnp.exp(cum - lw)).astype(BF16)
        kt_o[:, cols] = (k * w_out).astype(BF16)
        bt_o[:, cols] = (b * w_out).astype(BF16)
        kend_o[:, cols] = (k * w_end).astype(BF16)
        bend_o[:, cols] = (b * w_end).astype(BF16)
        vb_o[:, cols] = v.astype(BF16)


def _front(u, mu, w_r, w_k, w_v, w_g, w1, w2, w0, a1, a2, a0, k_k, k_a, r_k, seq, tm):
    t, d = u.shape
    n = w_r.shape[1]
    rb = tm // SUBLANES
    row = lambda i: (i, 0)

    def const(shape):
        return pl.BlockSpec(shape, lambda i: (0,) * len(shape), pipeline_mode=pl.Buffered(1))

    tok = pl.BlockSpec((tm, n), row)
    tok16 = jax.ShapeDtypeStruct((t, n), BF16)
    tok32 = jax.ShapeDtypeStruct((t, n), F32)
    return pl.pallas_call(
        functools.partial(_front_kernel, steps_per_seq=seq // tm),
        grid=(t // tm,),
        in_specs=[pl.BlockSpec((tm, d), row),
                  pl.BlockSpec((SUBLANES, d), lambda i: (jnp.maximum(i * rb - 1, 0), 0)),
                  const(mu.shape),
                  const(w_r.shape), const(w_k.shape), const(w_v.shape), const(w_g.shape),
                  const(w1.shape), const(w2.shape), const((1, n)),
                  const(a1.shape), const(a2.shape), const((1, n)),
                  const((1, n)), const((1, n)), const((1, n))],
        out_specs=[tok] * 9 + [pl.BlockSpec((tm // CHUNK, 1, n), lambda i: (i, 0, 0))],
        out_shape=[tok16] * 7 + [tok32] * 2 + [jax.ShapeDtypeStruct((t // CHUNK, 1, n), F32)],
        compiler_params=_cparams("parallel"),
        name="l1_front",
    )(u, u, mu, w_r, w_k, w_v, w_g, w1, w2, w0, a1, a2, a0, k_k, k_a, r_k)


def _rwkv_kernel(rw_ref, kkw_ref, kt_ref, bt_ref, kend_ref, bend_ref, vb_ref, bon_ref, g_ref,
                 dec_ref, lg_ref, lb_ref, o_ref, st, rt_s, y0_s, q_s, n_s, *, pairs):
    tc = rw_ref.shape[0]
    L = CHUNK
    hd = RW_HEAD_DIM
    nchunks = tc // L

    @pl.when(pl.program_id(2) == 0)
    def _():
        st[...] = jnp.zeros_like(st)

    row = lax.broadcasted_iota(jnp.int32, (L, LANES), 0)
    lane = lax.broadcasted_iota(jnp.int32, (L, LANES), 1)
    lane_h = lane & (hd - 1)
    strict = lane_h < row
    incl = lane_h <= row
    eye = jnp.where(lane_h == row, 1.0, 0.0).astype(F32)
    head_a = lane < hd
    row2 = lax.broadcasted_iota(jnp.int32, (2 * L, LANES), 0)
    lane2 = lax.broadcasted_iota(jnp.int32, (2 * L, LANES), 1)
    same_head = (row2 < hd) == (lane2 < hd)
    ones2 = _pair_ones()
    pcols = [slice(p * LANES, (p + 1) * LANES) for p in range(pairs)]

    def split(x):
        x = x.astype(F32)
        return jnp.concatenate([jnp.where(head_a, x, 0.0), jnp.where(head_a, 0.0, x)],
                               axis=0).astype(BF16)

    def build(it, carry):
        jobs = []
        for cc in range(BUILD_CHUNKS):
            c = it * BUILD_CHUNKS + cc
            rows = pl.ds(pl.multiple_of(c * L, L), L)
            jobs += [(c, p, rows, pcols[p]) for p in range(pairs)]
        J = range(len(jobs))
        ld = lambda ref: [ref[rows, cols] for (_, _, rows, cols) in jobs]
        rw, kkw, vb, bend = ld(rw_ref), ld(kkw_ref), ld(vb_ref), ld(bend_ref)
        zk = [split(t) for t in ld(kt_ref)]
        zb = [split(t) for t in ld(bt_ref)]
        lhs = [jnp.concatenate([kkw[j], rw[j]], axis=0) for j in J]
        p_k = [_mm_nt(lhs[j], zk[j]) for j in J]
        p_b = [_mm_nt(lhs[j], zb[j]) for j in J]
        amat = [jnp.where(strict, p_b[j][0:L], 0.0) for j in J]
        x = [eye - amat[j] for j in J]
        m = [_mm(amat[j].astype(BF16), split(amat[j])) for j in J]
        for _ in range(L.bit_length() - 3):
            xm = [_mm(jnp.concatenate([x[j], m[j]], axis=0).astype(BF16), split(m[j])) for j in J]
            x = [x[j] + xm[j][0:L] for j in J]
            m = [xm[j][L:2 * L] for j in J]
        x = [(x[j] + _mm(x[j].astype(BF16), split(m[j]))).astype(BF16) for j in J]
        a_kr = [jnp.concatenate([jnp.where(strict, p_k[j][0:L], 0.0),
                                 jnp.where(incl, p_k[j][L:2 * L], 0.0)], axis=0).astype(BF16)
                for j in J]
        akv = [_mm(a_kr[j], split(vb[j])) for j in J]
        ku = [_mm(x[j], jnp.concatenate([split(kkw[j]), split(akv[j][0:L])], axis=1)) for j in J]
        a_rb = [jnp.where(incl, p_b[j][L:2 * L], 0.0).astype(BF16) for j in J]
        ru = [_mm(a_rb[j], jnp.concatenate([split(ku[j][:, 0:LANES]), split(ku[j][:, LANES:])],
                                           axis=1)) for j in J]
        for j, (_, _, rows, cols) in enumerate(jobs):
            rt_s[rows, cols] = (rw[j].astype(F32) - ru[j][:, 0:LANES]).astype(BF16)
            y0_s[rows, cols] = akv[j][L:2 * L] - ru[j][:, LANES:]
        vu_t = [jnp.concatenate([vb[j].astype(F32), -ku[j][:, LANES:]], axis=0).T.astype(BF16)
                for j in J]
        kt_t = [ku[j][:, 0:LANES].T.astype(BF16) for j in J]
        kb_end = [jnp.concatenate([t, bend[j]], axis=0) for j, t in enumerate(ld(kend_ref))]
        nmat = [_mm(vu_t[j], kb_end[j]) for j in J]
        qmat = [_mm(kt_t[j], bend[j]) for j in J]
        for j, (c, p, _, _) in enumerate(jobs):
            n_s[c, p] = jnp.where(same_head, nmat[j], 0.0)
            q_s[c, p] = jnp.where(same_head, qmat[j], 0.0).astype(BF16)
        return carry

    lax.fori_loop(0, nchunks // BUILD_CHUNKS, build, 0)

    def scan(c, carry):
        rows = pl.ds(pl.multiple_of(c * L, L), L)
        dec = dec_ref[c]
        for p in range(pairs):
            s = st[p]
            sb = s.astype(BF16)
            y0_s[rows, pcols[p]] = y0_s[rows, pcols[p]] + _mm_nt(rt_s[rows, pcols[p]], sb)
            st[p] = s * dec[0:1, pcols[p]] + n_s[c, p] - _mm(sb, q_s[c, p])
        return carry

    lax.fori_loop(0, nchunks, scan, 0)

    slab = BUILD_CHUNKS * L

    def finish(i, carry):
        rows = pl.ds(pl.multiple_of(i * slab, slab), slab)
        y = y0_s[rows, :]
        yc = y - _headsum(y, ones2) * (1.0 / hd)
        var = _headsum(yc * yc, ones2) * (1.0 / hd)
        yn = yc * lax.rsqrt(var + GN_EPS) * lg_ref[...] + lb_ref[...]
        o_ref[rows, :] = ((yn + bon_ref[rows, :]) * g_ref[rows, :]).astype(o_ref.dtype)
        return carry

    lax.fori_loop(0, tc // slab, finish, 0)


def _rwkv(rw, kkw, kt, bt, kend, bend, vb, bon, gate, dec, lnx_g, lnx_b, bsz, seq, tc, pairs):
    t, width = rw.shape
    wb = pairs * LANES
    nt = seq // tc
    nchunks = tc // CHUNK
    tok = pl.BlockSpec((tc, wb), lambda b, p, i: (b * nt + i, p))
    par = pl.BlockSpec((1, wb), lambda b, p, i: (0, p))
    return pl.pallas_call(
        functools.partial(_rwkv_kernel, pairs=pairs),
        grid=(bsz, width // wb, nt),
        in_specs=[tok] * 9 + [pl.BlockSpec((nchunks, 1, wb), lambda b, p, i: (b * nt + i, 0, p)),
                              par, par],
        out_specs=tok,
        out_shape=jax.ShapeDtypeStruct((t, width), BF16),
        scratch_shapes=[pltpu.VMEM((pairs, LANES, LANES), F32),
                        pltpu.VMEM((tc, wb), BF16),
                        pltpu.VMEM((tc, wb), F32),
                        pltpu.VMEM((nchunks, pairs, LANES, LANES), BF16),
                        pltpu.VMEM((nchunks, pairs, LANES, LANES), F32)],
        compiler_params=_cparams("parallel", "parallel", "arbitrary"),
        name="l1_rwkv7",
    )(rw, kkw, kt, bt, kend, bend, vb, bon, gate, dec, lnx_g, lnx_b)


def _outproj1_kernel(y_ref, w_ref, h_ref, g_ref, o_ref):
    h = h_ref[...] + _mm(y_ref[...], w_ref[...])
    o_ref[...] = _rms(h, g_ref[...])


def _outproj1(y, w, h, g, tm):
    t, d = h.shape
    row = lambda i: (i, 0)
    full = lambda i: (0, 0)
    return pl.pallas_call(
        _outproj1_kernel,
        grid=(t // tm,),
        in_specs=[pl.BlockSpec((tm, y.shape[1]), row), pl.BlockSpec(w.shape, full),
                  pl.BlockSpec((tm, d), row), pl.BlockSpec((1, d), full)],
        out_specs=pl.BlockSpec((tm, d), row),
        out_shape=jax.ShapeDtypeStruct((t, d), F32),
        compiler_params=_cparams("parallel"),
        name="l1_outproj_final",
    )(y, w, h, g)


def _tile(n, pref):
    t = min(n, pref)
    assert n % t == 0, (n, pref)
    return t


def _layer0(x2, bsz, seq, norm_g, w_in, conv_w, conv_b, w_a, b_a, w_x, b_x, lam, lb_logits,
            hg_g, w_out, next_norm_g):
    t, d = x2.shape
    width = conv_w.shape[1]
    row = lambda p: p.reshape(1, -1)
    z = _norm_matmul(x2, row(norm_g), w_in.astype(BF16), _tile(t, 512), _tile(w_in.shape[1], 2048))
    ya = _rglru(z, conv_w, row(conv_b), w_a.astype(BF16), row(b_a), w_x.astype(BF16), row(b_x),
                row(lam), bsz, seq, _tile(seq, 512))
    yb = _hgrn2(z, lb_logits, row(hg_g), bsz, seq, _tile(seq, 512), 2 * width)
    w_out = w_out.astype(BF16)
    return _outproj0(ya, yb, w_out[:width], w_out[width:], x2, row(next_norm_g), _tile(t, 512))


def _layer1(h, u, bsz, seq, mu, w_r, w_k, w_v, w_g, w0, w1, w2, a0, a1, a2, k_k, k_a, r_k,
            lnx_g, lnx_b, w_o, final_g):
    t, d = h.shape
    row = lambda p: p.reshape(1, -1)
    bf = lambda w: w.astype(BF16)
    ops = _front(u, mu, bf(w_r), bf(w_k), bf(w_v), bf(w_g), bf(w1), bf(w2), row(w0), bf(a1),
                 bf(a2), row(a0), row(k_k), row(k_a), row(r_k), seq, _tile(seq, 256))
    y = _rwkv(*ops, row(lnx_g), row(lnx_b), bsz, seq, _tile(seq, 512), pairs=4)
    return _outproj1(y, bf(w_o), h, row(final_g), _tile(t, 512))


def kernel(x, ab_norm_g, ab_w_in, rg_conv_w, rg_conv_b, rg_w_a, rg_b_a, rg_w_x, rg_b_x, rg_lambda, hg_lb_logits, hg_norm_g, ab_w_out, c_norm_g, c_mu, c_w_r, c_w_k, c_w_v, c_w_g, c_w0, c_w1, c_w2, c_a0, c_a1, c_a2, c_k_k, c_k_a, c_r_k, c_lnx_g, c_lnx_b, c_w_o, final_g):
    bsz, seq, d = x.shape
    assert ab_norm_g.shape[0] == 1 and c_norm_g.shape[0] == 1, "two-layer trunk only"
    x2 = x.reshape(bsz * seq, d)
    h, u = _layer0(x2, bsz, seq, ab_norm_g[0], ab_w_in[0], rg_conv_w[0], rg_conv_b[0], rg_w_a[0],
                   rg_b_a[0], rg_w_x[0], rg_b_x[0], rg_lambda[0], hg_lb_logits, hg_norm_g[0],
                   ab_w_out[0], c_norm_g[0])
    out = _layer1(h, u, bsz, seq, c_mu[0], c_w_r[0], c_w_k[0], c_w_v[0], c_w_g[0], c_w0[0],
                  c_w1[0], c_w2[0], c_a0[0], c_a1[0], c_a2[0], c_k_k[0], c_k_a[0],
                  c_r_k[0].reshape(-1), c_lnx_g[0], c_lnx_b[0], c_w_o[0], final_g)
    return out.reshape(bsz, seq, d)
```

```python
import functools

import jax
import jax.numpy as jnp
from jax import lax
from jax.experimental import pallas as pl
from jax.experimental.pallas import tpu as pltpu

F32 = jnp.float32
BF16 = jnp.bfloat16

RMS_EPS = 1e-6
GN_EPS = 64e-5
DECAY_SCALE = 0.6065306597126334
RG_C = 8.0
RG_BLOCKS = 8
RG_CONV = 4
HG_HEAD_DIM = 128
RW_HEAD_DIM = 64
CHUNK = 64
BUILD_CHUNKS = 4
FRONT_COLS = 512
LANES = 128
SUBLANES = 8
VMEM_LIMIT = 56 * 1024 * 1024


def _cparams(*sem):
    return pltpu.CompilerParams(dimension_semantics=sem, vmem_limit_bytes=VMEM_LIMIT)


def _mm(a, b):
    return jnp.dot(a, b, preferred_element_type=F32)


def _mm_nt(a, b):
    return lax.dot_general(a, b, (((1,), (1,)), ((), ())), preferred_element_type=F32)


def _mm_exact(a, b):
    return jnp.dot(a, b, preferred_element_type=F32, precision=lax.Precision.HIGHEST)


def _rms(x, g):
    return x * lax.rsqrt(jnp.mean(x * x, axis=-1, keepdims=True) + RMS_EPS) * g


def _sigmoid(x):
    return 1.0 / (1.0 + jnp.exp(-x))


def _silu(x):
    return x * _sigmoid(x)


def _softplus(x):
    return jnp.maximum(x, 0.0) + jnp.log(1.0 + jnp.exp(-jnp.abs(x)))


def _norm_matmul_kernel(x_ref, g_ref, w_ref, o_ref):
    xn = _rms(x_ref[...], g_ref[...]).astype(BF16)
    o_ref[...] = _mm(xn, w_ref[...])


def _norm_matmul(x, g, w, tm, tn):
    t, d = x.shape
    n = w.shape[1]
    return pl.pallas_call(
        _norm_matmul_kernel,
        grid=(n // tn, t // tm),
        in_specs=[pl.BlockSpec((tm, d), lambda j, i: (i, 0)),
                  pl.BlockSpec((1, d), lambda j, i: (0, 0)),
                  pl.BlockSpec((d, tn), lambda j, i: (0, j))],
        out_specs=pl.BlockSpec((tm, tn), lambda j, i: (i, j)),
        out_shape=jax.ShapeDtypeStruct((t, n), F32),
        compiler_params=_cparams("parallel", "parallel"),
        name="l0_norm_inproj",
    )(x, g, w)


def _rglru_kernel(xa_ref, ga_ref, cw_ref, cb_ref, wa_ref, ba_ref, wx_ref, bx_ref, lam_ref,
                  o_ref, xbuf, a_s, u_s, hcar):
    ts, width = xa_ref.shape
    bd = width // RG_BLOCKS

    @pl.when(pl.program_id(1) == 0)
    def _():
        xbuf[0:SUBLANES, :] = jnp.zeros((SUBLANES, width), F32)
        hcar[...] = jnp.zeros_like(hcar)

    xa = xa_ref[...]
    xbuf[SUBLANES:SUBLANES + ts, :] = xa
    xc = cb_ref[...] + cw_ref[RG_CONV - 1:RG_CONV, :] * xa
    for j in range(1, RG_CONV):
        xc = xc + cw_ref[RG_CONV - 1 - j:RG_CONV - j, :] * xbuf[pl.ds(SUBLANES - j, ts), :]
    xbuf[0:SUBLANES, :] = xa[ts - SUBLANES:ts, :]

    xcb = xc.astype(BF16)
    gr, gi = [], []
    for n in range(RG_BLOCKS):
        blk = xcb[:, n * bd:(n + 1) * bd]
        gr.append(_mm(blk, wa_ref[n]))
        gi.append(_mm(blk, wx_ref[n]))
    gate_r = _sigmoid(jnp.concatenate(gr, axis=-1) + ba_ref[...])
    gate_i = _sigmoid(jnp.concatenate(gi, axis=-1) + bx_ref[...])
    log_a = (-RG_C) * gate_r * _softplus(-lam_ref[...])
    a = jnp.exp(log_a)
    u = jnp.sqrt(-jnp.tanh(log_a) * (1.0 + a * a)) * (gate_i * xc)

    row = lax.broadcasted_iota(jnp.int32, (ts, width), 0) & (SUBLANES - 1)
    d = 1
    while d < SUBLANES:
        keep = row >= d
        u = jnp.where(keep, a * pltpu.roll(u, d, axis=0) + u, u)
        a = jnp.where(keep, a * pltpu.roll(a, d, axis=0), a)
        d *= 2
    a_s[...] = a
    u_s[...] = u

    def body(j, h):
        r0 = pl.multiple_of(j * SUBLANES, SUBLANES)
        hb = a_s[pl.ds(r0, SUBLANES), :] * h + u_s[pl.ds(r0, SUBLANES), :]
        u_s[pl.ds(r0, SUBLANES), :] = hb
        return hb[SUBLANES - 1:SUBLANES, :]

    hcar[0:1, :] = lax.fori_loop(0, ts // SUBLANES, body, hcar[0:1, :])
    o_ref[...] = (u_s[...] * _silu(ga_ref[...])).astype(o_ref.dtype)


def _rglru(z, conv_w, conv_b, w_a, b_a, w_x, b_x, lam, bsz, seq, ts):
    width = conv_w.shape[1]
    nt = seq // ts
    row = lambda b, i: (b * nt + i, 0)
    full2 = lambda b, i: (0, 0)
    full3 = lambda b, i: (0, 0, 0)
    return pl.pallas_call(
        _rglru_kernel,
        grid=(bsz, nt),
        in_specs=[pl.BlockSpec((ts, width), row),
                  pl.BlockSpec((ts, width), lambda b, i: (b * nt + i, 1)),
                  pl.BlockSpec(conv_w.shape, full2),
                  pl.BlockSpec((1, width), full2),
                  pl.BlockSpec(w_a.shape, full3),
                  pl.BlockSpec((1, width), full2),
                  pl.BlockSpec(w_x.shape, full3),
                  pl.BlockSpec((1, width), full2),
                  pl.BlockSpec((1, width), full2)],
        out_specs=pl.BlockSpec((ts, width), row),
        out_shape=jax.ShapeDtypeStruct((bsz * seq, width), BF16),
        scratch_shapes=[pltpu.VMEM((ts + SUBLANES, width), F32),
                        pltpu.VMEM((ts, width), F32),
                        pltpu.VMEM((ts, width), F32),
                        pltpu.VMEM((SUBLANES, width), F32)],
        compiler_params=_cparams("parallel", "arbitrary"),
        name="l0_rglru",
    )(z, z, conv_w, conv_b, w_a, b_a, w_x, b_x, lam)


def _hgrn2_kernel(q_ref, f_ref, v_ref, g_ref, lbl_ref, gn_ref, o_ref, st):
    tc = q_ref.shape[0]

    @pl.when(pl.program_id(2) == 0)
    def _():
        st[...] = jnp.zeros_like(st)

    lbl = lbl_ref[...]
    e = jnp.exp(lbl - jnp.max(lbl, axis=0, keepdims=True))
    lb = e[0:1, :] / jnp.sum(e, axis=0, keepdims=True)

    ri = lax.broadcasted_iota(jnp.int32, (CHUNK, CHUNK), 0)
    ci = lax.broadcasted_iota(jnp.int32, (CHUNK, CHUNK), 1)
    causal = ri >= ci
    tril = jnp.where(causal, 1.0, 0.0).astype(F32)

    C = range(tc // CHUNK)
    sl = [slice(n * CHUNK, (n + 1) * CHUNK) for n in C]
    sg = _sigmoid(f_ref[...])
    log_f = jnp.log(lb + (1.0 - lb) * sg)
    k_all = (1.0 - lb) * (1.0 - sg)
    cum_w = _mm_exact(tril, jnp.concatenate([log_f[s] for s in sl], axis=1))
    cum = [cum_w[:, n * HG_HEAD_DIM:(n + 1) * HG_HEAD_DIM] for n in C]
    total = [cum[n][CHUNK - 1:CHUNK, :] for n in C]
    k = [k_all[s] for s in sl]
    v = [v_ref[s, :] for s in sl]
    q_dec = [(q_ref[sl[n], :] * jnp.exp(cum[n])).astype(BF16) for n in C]
    k_inv = [(k[n] * jnp.exp(-cum[n])).astype(BF16) for n in C]
    k_end = [(k[n] * jnp.exp(total[n] - cum[n])).astype(BF16) for n in C]
    scores = [jnp.where(causal, _mm_nt(q_dec[n], k_inv[n]), 0.0).astype(BF16) for n in C]
    upd = [_mm(v[n].T.astype(BF16), k_end[n]) for n in C]
    o = [_mm(scores[n], v[n].astype(BF16)) for n in C]
    state = st[...]
    states = []
    for n in C:
        states.append(state.astype(BF16))
        state = state * jnp.exp(total[n]) + upd[n]
    st[...] = state
    o = jnp.concatenate([o[n] + _mm_nt(q_dec[n], states[n]) for n in C], axis=0)
    o = o * lax.rsqrt(jnp.mean(o * o, axis=-1, keepdims=True) + RMS_EPS) * gn_ref[...]
    o_ref[...] = (o * _silu(g_ref[...])).astype(o_ref.dtype)


def _hgrn2(z, lb_logits, hg_g, bsz, seq, tc, col0):
    width = lb_logits.shape[1]
    heads = width // HG_HEAD_DIM
    nt = seq // tc

    def col(k):
        return lambda b, h, i: (b * nt + i, (col0 + k * width) // HG_HEAD_DIM + h)

    return pl.pallas_call(
        _hgrn2_kernel,
        grid=(bsz, heads, nt),
        in_specs=[pl.BlockSpec((tc, HG_HEAD_DIM), col(0)),
                  pl.BlockSpec((tc, HG_HEAD_DIM), col(1)),
                  pl.BlockSpec((tc, HG_HEAD_DIM), col(2)),
                  pl.BlockSpec((tc, HG_HEAD_DIM), col(3)),
                  pl.BlockSpec((lb_logits.shape[0], HG_HEAD_DIM), lambda b, h, i: (0, h)),
                  pl.BlockSpec((1, HG_HEAD_DIM), lambda b, h, i: (0, 0))],
        out_specs=pl.BlockSpec((tc, HG_HEAD_DIM), lambda b, h, i: (b * nt + i, h)),
        out_shape=jax.ShapeDtypeStruct((bsz * seq, width), BF16),
        scratch_shapes=[pltpu.VMEM((HG_HEAD_DIM, HG_HEAD_DIM), F32)],
        compiler_params=_cparams("parallel", "parallel", "arbitrary"),
        name="l0_hgrn2",
    )(z, z, z, z, lb_logits, hg_g)


def _outproj0_kernel(ya_ref, yb_ref, wa_ref, wb_ref, x_ref, g_ref, h_ref, u_ref):
    h = x_ref[...] + _mm(ya_ref[...], wa_ref[...]) + _mm(yb_ref[...], wb_ref[...])
    h_ref[...] = h
    u_ref[...] = _rms(h, g_ref[...])


def _outproj0(ya, yb, w_a, w_b, x, g, tm):
    t, d = x.shape
    row = lambda i: (i, 0)
    full = lambda i: (0, 0)
    return pl.pallas_call(
        _outproj0_kernel,
        grid=(t // tm,),
        in_specs=[pl.BlockSpec((tm, ya.shape[1]), row),
                  pl.BlockSpec((tm, yb.shape[1]), row),
                  pl.BlockSpec(w_a.shape, full),
                  pl.BlockSpec(w_b.shape, full),
                  pl.BlockSpec((tm, d), row),
                  pl.BlockSpec((1, d), full)],
        out_specs=[pl.BlockSpec((tm, d), row), pl.BlockSpec((tm, d), row)],
        out_shape=[jax.ShapeDtypeStruct((t, d), F32), jax.ShapeDtypeStruct((t, d), F32)],
        compiler_params=_cparams("parallel"),
        name="l0_outproj",
    )(ya, yb, w_a, w_b, x, g)


def _shifted(u, up_ref, first_of_seq):
    prev_last = jnp.where(first_of_seq, 0.0, up_ref[SUBLANES - 1:SUBLANES, :])
    row = lax.broadcasted_iota(jnp.int32, u.shape, 0)
    return jnp.where(row == 0, prev_last, pltpu.roll(u, 1, axis=0))


def _pair_ones():
    r = lax.broadcasted_iota(jnp.int32, (2 * LANES, LANES), 0) & (LANES - 1)
    c = lax.broadcasted_iota(jnp.int32, (2 * LANES, LANES), 1)
    return jnp.where((r < RW_HEAD_DIM) == (c < RW_HEAD_DIM), 1.0, 0.0).astype(BF16)


def _hi_lo(x):
    hi = x.astype(BF16)
    return hi, (x - hi.astype(F32)).astype(BF16)


def _headsum(x, ones2, split=True):
    n, nblk = x.shape[0], x.shape[1] // LANES
    xs = jnp.concatenate([x[:, i * LANES:(i + 1) * LANES] for i in range(nblk)], axis=0)
    if split:
        s = _mm(jnp.concatenate(_hi_lo(xs), axis=1), ones2)
    else:
        s = _mm(xs.astype(BF16), ones2[0:LANES])
    return jnp.concatenate([s[i * n:(i + 1) * n] for i in range(nblk)], axis=1)


def _front_kernel(u_ref, up_ref, mu_ref, wr_ref, wk_ref, wv_ref, wg_ref, w1_ref, w2_ref, w0_ref,
                  a1_ref, a2_ref, a0_ref, kk_ref, ka_ref, rk_ref,
                  rw_o, kkw_o, kt_o, bt_o, kend_o, bend_o, vb_o, bon_o, gate_o, dec_o,
                  *, steps_per_seq):
    tm = u_ref.shape[0]
    width = w0_ref.shape[1]
    L = CHUNK
    nch = tm // L
    u = u_ref[...]
    delta = _shifted(u, up_ref, pl.program_id(0) % steps_per_seq == 0) - u
    x_r, x_w, x_k, x_v, x_a, x_g = ((u + delta * mu_ref[i:i + 1, :]).astype(BF16)
                                    for i in range(6))
    zw = jnp.tanh(_mm(x_w, w1_ref[...])).astype(BF16)
    za = _mm(x_a, a1_ref[...]).astype(BF16)
    ones2 = _pair_ones()
    ri = lax.broadcasted_iota(jnp.int32, (L, L), 0)
    ci = lax.broadcasted_iota(jnp.int32, (L, L), 1)
    tril = jnp.where(ri >= ci, 1.0, 0.0).astype(BF16)
    tril2 = jnp.concatenate([tril, tril], axis=1)

    for g in range(width // FRONT_COLS):
        cols = slice(g * FRONT_COLS, (g + 1) * FRONT_COLS)
        r = _mm(x_r, wr_ref[g])
        kraw = _mm(x_k, wk_ref[g])
        v = _mm(x_v, wv_ref[g])
        gate_o[:, cols] = _silu(_mm(x_g, wg_ref[g]))
        lw = -DECAY_SCALE * _sigmoid(w0_ref[:, cols] + _mm(zw, w2_ref[:, cols]))
        a = _sigmoid(a0_ref[:, cols] + _mm(za, a2_ref[:, cols]))
        kkp = kraw * kk_ref[:, cols]
        kk = kkp * lax.rsqrt(jnp.maximum(_headsum(kkp * kkp, ones2, split=False), 1e-24))
        k = kraw * (1.0 + (a - 1.0) * ka_ref[:, cols])
        b = kk * a
        bon_o[:, cols] = _headsum(r * k * rk_ref[:, cols], ones2, split=False) * v
        lw_hi, lw_lo = _hi_lo(jnp.concatenate([lw[n * L:(n + 1) * L] for n in range(nch)], axis=1))
        cum_w = _mm(tril2, jnp.concatenate([lw_hi, lw_lo], axis=0))
        cum = jnp.concatenate([cum_w[:, n * FRONT_COLS:(n + 1) * FRONT_COLS] for n in range(nch)],
                              axis=0)
        totals = [cum_w[L - 1:L, n * FRONT_COLS:(n + 1) * FRONT_COLS] for n in range(nch)]
        decs = [jnp.exp(t) for t in totals]
        for n in range(nch):
            dec_o[n, :, cols] = decs[n]
        w_out = jnp.exp(-cum)
        w_end = w_out * jnp.concatenate([jnp.broadcast_to(t, (L, FRONT_COLS)) for t in decs], axis=0)
        rw_o[:, cols] = (r * jnp.exp(cum)).astype(BF16)
        kkw_o[:, cols] = (kk * jnp.exp(cum - lw)).astype(BF16)
        kt_o[:, cols] = (k * w_out).astype(BF16)
        bt_o[:, cols] = (b * w_out).astype(BF16)
        kend_o[:, cols] = (k * w_end).astype(BF16)
        bend_o[:, cols] = (b * w_end).astype(BF16)
        vb_o[:, cols] = v.astype(BF16)


def _front(u, mu, w_r, w_k, w_v, w_g, w1, w2, w0, a1, a2, a0, k_k, k_a, r_k, seq, tm):
    t, d = u.shape
    n = w_r.shape[1]
    rb = tm // SUBLANES
    row = lambda i: (i, 0)
    w_r, w_k, w_v, w_g = (w.reshape(d, n // FRONT_COLS, FRONT_COLS).transpose(1, 0, 2)
                          for w in (w_r, w_k, w_v, w_g))

    def const(shape):
        return pl.BlockSpec(shape, lambda i: (0,) * len(shape), pipeline_mode=pl.Buffered(1))

    tok = pl.BlockSpec((tm, n), row)
    tok16 = jax.ShapeDtypeStruct((t, n), BF16)
    tok32 = jax.ShapeDtypeStruct((t, n), F32)
    return pl.pallas_call(
        functools.partial(_front_kernel, steps_per_seq=seq // tm),
        grid=(t // tm,),
        in_specs=[pl.BlockSpec((tm, d), row),
                  pl.BlockSpec((SUBLANES, d), lambda i: (jnp.maximum(i * rb - 1, 0), 0)),
                  const(mu.shape),
                  const(w_r.shape), const(w_k.shape), const(w_v.shape), const(w_g.shape),
                  const(w1.shape), const(w2.shape), const((1, n)),
                  const(a1.shape), const(a2.shape), const((1, n)),
                  const((1, n)), const((1, n)), const((1, n))],
        out_specs=[tok] * 9 + [pl.BlockSpec((tm // CHUNK, 1, n), lambda i: (i, 0, 0))],
        out_shape=[tok16] * 7 + [tok32] * 2 + [jax.ShapeDtypeStruct((t // CHUNK, 1, n), F32)],
        compiler_params=_cparams("parallel"),
        name="l1_front",
    )(u, u, mu, w_r, w_k, w_v, w_g, w1, w2, w0, a1, a2, a0, k_k, k_a, r_k)


def _rwkv_kernel(rw_ref, kkw_ref, kt_ref, bt_ref, kend_ref, bend_ref, vb_ref, bon_ref, g_ref,
                 dec_ref, lg_ref, lb_ref, o_ref, st, rt_s, y0_s, q_s, n_s, *, pairs):
    tc = rw_ref.shape[0]
    L = CHUNK
    hd = RW_HEAD_DIM
    nchunks = tc // L

    @pl.when(pl.program_id(2) == 0)
    def _():
        st[...] = jnp.zeros_like(st)

    row = lax.broadcasted_iota(jnp.int32, (L, LANES), 0)
    lane = lax.broadcasted_iota(jnp.int32, (L, LANES), 1)
    lane_h = lane & (hd - 1)
    strict = lane_h < row
    incl = lane_h <= row
    eye = jnp.where(lane_h == row, 1.0, 0.0).astype(F32)
    head_a = lane < hd
    row2 = lax.broadcasted_iota(jnp.int32, (2 * L, LANES), 0)
    lane2 = lax.broadcasted_iota(jnp.int32, (2 * L, LANES), 1)
    same_head = (row2 < hd) == (lane2 < hd)
    ones2 = _pair_ones()
    pcols = [slice(p * LANES, (p + 1) * LANES) for p in range(pairs)]

    def split(x):
        x = x.astype(F32)
        return jnp.concatenate([jnp.where(head_a, x, 0.0), jnp.where(head_a, 0.0, x)],
                               axis=0).astype(BF16)

    def build(it, carry):
        jobs = []
        for cc in range(BUILD_CHUNKS):
            c = it * BUILD_CHUNKS + cc
            rows = pl.ds(pl.multiple_of(c * L, L), L)
            jobs += [(c, p, rows, pcols[p]) for p in range(pairs)]
        J = range(len(jobs))
        ld = lambda ref: [ref[rows, cols] for (_, _, rows, cols) in jobs]
        rw, kkw, vb, bend = ld(rw_ref), ld(kkw_ref), ld(vb_ref), ld(bend_ref)
        zk = [split(t) for t in ld(kt_ref)]
        zb = [split(t) for t in ld(bt_ref)]
        lhs = [jnp.concatenate([kkw[j], rw[j]], axis=0) for j in J]
        p_k = [_mm_nt(lhs[j], zk[j]) for j in J]
        p_b = [_mm_nt(lhs[j], zb[j]) for j in J]
        amat = [jnp.where(strict, p_b[j][0:L], 0.0) for j in J]
        x = [eye - amat[j] for j in J]
        m = [_mm(amat[j].astype(BF16), split(amat[j])) for j in J]
        for _ in range(L.bit_length() - 3):
            xm = [_mm(jnp.concatenate([x[j], m[j]], axis=0).astype(BF16), split(m[j])) for j in J]
            x = [x[j] + xm[j][0:L] for j in J]
            m = [xm[j][L:2 * L] for j in J]
        x = [(x[j] + _mm(x[j].astype(BF16), split(m[j]))).astype(BF16) for j in J]
        a_kr = [jnp.concatenate([jnp.where(strict, p_k[j][0:L], 0.0),
                                 jnp.where(incl, p_k[j][L:2 * L], 0.0)], axis=0).astype(BF16)
                for j in J]
        akv = [_mm(a_kr[j], split(vb[j])) for j in J]
        ku = [_mm(x[j], jnp.concatenate([split(kkw[j]), split(akv[j][0:L])], axis=1)) for j in J]
        a_rb = [jnp.where(incl, p_b[j][L:2 * L], 0.0).astype(BF16) for j in J]
        ru = [_mm(a_rb[j], jnp.concatenate([split(ku[j][:, 0:LANES]), split(ku[j][:, LANES:])],
                                           axis=1)) for j in J]
        for j, (_, _, rows, cols) in enumerate(jobs):
            rt_s[rows, cols] = (rw[j].astype(F32) - ru[j][:, 0:LANES]).astype(BF16)
            y0_s[rows, cols] = akv[j][L:2 * L] - ru[j][:, LANES:]
        vu_t = [jnp.concatenate([vb[j].astype(F32), -ku[j][:, LANES:]], axis=0).T.astype(BF16)
                for j in J]
        kt_t = [ku[j][:, 0:LANES].T.astype(BF16) for j in J]
        kb_end = [jnp.concatenate([t, bend[j]], axis=0) for j, t in enumerate(ld(kend_ref))]
        nmat = [_mm(vu_t[j], kb_end[j]) for j in J]
        qmat = [_mm(kt_t[j], bend[j]) for j in J]
        for j, (c, p, _, _) in enumerate(jobs):
            n_s[c, p] = jnp.where(same_head, nmat[j], 0.0)
            q_s[c, p] = jnp.where(same_head, qmat[j], 0.0).astype(BF16)
        return carry

    lax.fori_loop(0, nchunks // BUILD_CHUNKS, build, 0)

    def scan(c, carry):
        rows = pl.ds(pl.multiple_of(c * L, L), L)
        dec = dec_ref[c]
        for p in range(pairs):
            s = st[p]
            sb = s.astype(BF16)
            y0_s[rows, pcols[p]] = y0_s[rows, pcols[p]] + _mm_nt(rt_s[rows, pcols[p]], sb)
            st[p] = s * dec[0:1, pcols[p]] + n_s[c, p] - _mm(sb, q_s[c, p])
        return carry

    lax.fori_loop(0, nchunks, scan, 0)

    slab = BUILD_CHUNKS * L

    def finish(i, carry):
        rows = pl.ds(pl.multiple_of(i * slab, slab), slab)
        y = y0_s[rows, :]
        yc = y - _headsum(y, ones2) * (1.0 / hd)
        var = _headsum(yc * yc, ones2) * (1.0 / hd)
        yn = yc * lax.rsqrt(var + GN_EPS) * lg_ref[...] + lb_ref[...]
        o_ref[rows, :] = ((yn + bon_ref[rows, :]) * g_ref[rows, :]).astype(o_ref.dtype)
        return carry

    lax.fori_loop(0, tc // slab, finish, 0)


def _rwkv(rw, kkw, kt, bt, kend, bend, vb, bon, gate, dec, lnx_g, lnx_b, bsz, seq, tc, pairs):
    t, width = rw.shape
    wb = pairs * LANES
    nt = seq // tc
    nchunks = tc // CHUNK
    tok = pl.BlockSpec((tc, wb), lambda b, p, i: (b * nt + i, p))
    par = pl.BlockSpec((1, wb), lambda b, p, i: (0, p))
    return pl.pallas_call(
        functools.partial(_rwkv_kernel, pairs=pairs),
        grid=(bsz, width // wb, nt),
        in_specs=[tok] * 9 + [pl.BlockSpec((nchunks, 1, wb), lambda b, p, i: (b * nt + i, 0, p)),
                              par, par],
        out_specs=tok,
        out_shape=jax.ShapeDtypeStruct((t, width), BF16),
        scratch_shapes=[pltpu.VMEM((pairs, LANES, LANES), F32),
                        pltpu.VMEM((tc, wb), BF16),
                        pltpu.VMEM((tc, wb), F32),
                        pltpu.VMEM((nchunks, pairs, LANES, LANES), BF16),
                        pltpu.VMEM((nchunks, pairs, LANES, LANES), F32)],
        compiler_params=_cparams("parallel", "parallel", "arbitrary"),
        name="l1_rwkv7",
    )(rw, kkw, kt, bt, kend, bend, vb, bon, gate, dec, lnx_g, lnx_b)


def _outproj1_kernel(y_ref, w_ref, h_ref, g_ref, o_ref):
    h = h_ref[...] + _mm(y_ref[...], w_ref[...])
    o_ref[...] = _rms(h, g_ref[...])


def _outproj1(y, w, h, g, tm):
    t, d = h.shape
    row = lambda i: (i, 0)
    full = lambda i: (0, 0)
    return pl.pallas_call(
        _outproj1_kernel,
        grid=(t // tm,),
        in_specs=[pl.BlockSpec((tm, y.shape[1]), row), pl.BlockSpec(w.shape, full),
                  pl.BlockSpec((tm, d), row), pl.BlockSpec((1, d), full)],
        out_specs=pl.BlockSpec((tm, d), row),
        out_shape=jax.ShapeDtypeStruct((t, d), F32),
        compiler_params=_cparams("parallel"),
        name="l1_outproj_final",
    )(y, w, h, g)


def _tile(n, pref):
    t = min(n, pref)
    assert n % t == 0, (n, pref)
    return t


def _layer0(x2, bsz, seq, norm_g, w_in, conv_w, conv_b, w_a, b_a, w_x, b_x, lam, lb_logits,
            hg_g, w_out, next_norm_g):
    t, d = x2.shape
    width = conv_w.shape[1]
    row = lambda p: p.reshape(1, -1)
    z = _norm_matmul(x2, row(norm_g), w_in.astype(BF16), _tile(t, 512), _tile(w_in.shape[1], 2048))
    ya = _rglru(z, conv_w, row(conv_b), w_a.astype(BF16), row(b_a), w_x.astype(BF16), row(b_x),
                row(lam), bsz, seq, _tile(seq, 512))
    yb = _hgrn2(z, lb_logits, row(hg_g), bsz, seq, _tile(seq, 512), 2 * width)
    w_out = w_out.astype(BF16)
    return _outproj0(ya, yb, w_out[:width], w_out[width:], x2, row(next_norm_g), _tile(t, 512))


def _layer1(h, u, bsz, seq, mu, w_r, w_k, w_v, w_g, w0, w1, w2, a0, a1, a2, k_k, k_a, r_k,
            lnx_g, lnx_b, w_o, final_g):
    t, d = h.shape
    row = lambda p: p.reshape(1, -1)
    bf = lambda w: w.astype(BF16)
    ops = _front(u, mu, bf(w_r), bf(w_k), bf(w_v), bf(w_g), bf(w1), bf(w2), row(w0), bf(a1),
                 bf(a2), row(a0), row(k_k), row(k_a), row(r_k), seq, _tile(seq, 256))
    y = _rwkv(*ops, row(lnx_g), row(lnx_b), bsz, seq, _tile(seq, 512), pairs=4)
    return _outproj1(y, bf(w_o), h, row(final_g), _tile(t, 512))


def kernel(x, ab_norm_g, ab_w_in, rg_conv_w, rg_conv_b, rg_w_a, rg_b_a, rg_w_x, rg_b_x, rg_lambda, hg_lb_logits, hg_norm_g, ab_w_out, c_norm_g, c_mu, c_w_r, c_w_k, c_w_v, c_w_g, c_w0, c_w1, c_w2, c_a0, c_a1, c_a2, c_k_k, c_k_a, c_r_k, c_lnx_g, c_lnx_b, c_w_o, final_g):
    bsz, seq, d = x.shape
    assert ab_norm_g.shape[0] == 1 and c_norm_g.shape[0] == 1, "two-layer trunk only"
    x2 = x.reshape(bsz * seq, d)
    h, u = _layer0(x2, bsz, seq, ab_norm_g[0], ab_w_in[0], rg_conv_w[0], rg_conv_b[0], rg_w_a[0],
                   rg_b_a[0], rg_w_x[0], rg_b_x[0], rg_lambda[0], hg_lb_logits, hg_norm_g[0],
                   ab_w_out[0], c_norm_g[0])
    out = _layer1(h, u, bsz, seq, c_mu[0], c_w_r[0], c_w_k[0], c_w_v[0], c_w_g[0], c_w0[0],
                  c_w1[0], c_w2[0], c_a0[0], c_a1[0], c_a2[0], c_k_k[0], c_k_a[0],
                  c_r_k[0].reshape(-1), c_lnx_g[0], c_lnx_b[0], c_w_o[0], final_g)
    return out.reshape(bsz, seq, d)
```

```python
import functools
import itertools

import jax
import jax.numpy as jnp
from jax import lax
from jax.experimental import pallas as pl
from jax.experimental.pallas import tpu as pltpu

F32 = jnp.float32
BF16 = jnp.bfloat16

RMS_EPS = 1e-6
GN_EPS = 64e-5
DECAY_SCALE = 0.6065306597126334
RG_C = 8.0
RG_BLOCKS = 8
RG_CONV = 4
HG_HEAD_DIM = 128
RW_HEAD_DIM = 64
CHUNK = 64
BUILD_CHUNKS = 4
BUILD_STAGES = 12
FRONT_COLS = 512
LANES = 128
SUBLANES = 8
VMEM_LIMIT = 56 * 1024 * 1024


def _cparams(*sem):
    return pltpu.CompilerParams(dimension_semantics=sem, vmem_limit_bytes=VMEM_LIMIT)


def _mm(a, b):
    return jnp.dot(a, b, preferred_element_type=F32)


def _mm_nt(a, b):
    return lax.dot_general(a, b, (((1,), (1,)), ((), ())), preferred_element_type=F32)


def _mm_exact(a, b):
    return jnp.dot(a, b, preferred_element_type=F32, precision=lax.Precision.HIGHEST)


def _rms(x, g):
    return x * lax.rsqrt(jnp.mean(x * x, axis=-1, keepdims=True) + RMS_EPS) * g


def _sigmoid(x):
    return 1.0 / (1.0 + jnp.exp(-x))


def _silu(x):
    return x * _sigmoid(x)


def _softplus(x):
    return jnp.maximum(x, 0.0) + jnp.log(1.0 + jnp.exp(-jnp.abs(x)))


def _norm_matmul_kernel(x_ref, g_ref, w_ref, o_ref):
    xn = _rms(x_ref[...], g_ref[...]).astype(BF16)
    o_ref[...] = _mm(xn, w_ref[...])


def _norm_matmul(x, g, w, tm, tn):
    t, d = x.shape
    n = w.shape[1]
    return pl.pallas_call(
        _norm_matmul_kernel,
        grid=(n // tn, t // tm),
        in_specs=[pl.BlockSpec((tm, d), lambda j, i: (i, 0)),
                  pl.BlockSpec((1, d), lambda j, i: (0, 0)),
                  pl.BlockSpec((d, tn), lambda j, i: (0, j))],
        out_specs=pl.BlockSpec((tm, tn), lambda j, i: (i, j)),
        out_shape=jax.ShapeDtypeStruct((t, n), F32),
        compiler_params=_cparams("parallel", "parallel"),
        name="l0_norm_inproj",
    )(x, g, w)


def _rglru_kernel(xa_ref, ga_ref, cw_ref, cb_ref, wa_ref, ba_ref, wx_ref, bx_ref, lam_ref,
                  o_ref, xbuf, a_s, u_s, hcar):
    ts, width = xa_ref.shape
    bd = width // RG_BLOCKS

    @pl.when(pl.program_id(1) == 0)
    def _():
        xbuf[0:SUBLANES, :] = jnp.zeros((SUBLANES, width), F32)
        hcar[...] = jnp.zeros_like(hcar)

    xa = xa_ref[...]
    xbuf[SUBLANES:SUBLANES + ts, :] = xa
    xc = cb_ref[...] + cw_ref[RG_CONV - 1:RG_CONV, :] * xa
    for j in range(1, RG_CONV):
        xc = xc + cw_ref[RG_CONV - 1 - j:RG_CONV - j, :] * xbuf[pl.ds(SUBLANES - j, ts), :]
    xbuf[0:SUBLANES, :] = xa[ts - SUBLANES:ts, :]

    xcb = xc.astype(BF16)
    gr, gi = [], []
    for n in range(RG_BLOCKS):
        blk = xcb[:, n * bd:(n + 1) * bd]
        gr.append(_mm(blk, wa_ref[n]))
        gi.append(_mm(blk, wx_ref[n]))
    gate_r = _sigmoid(jnp.concatenate(gr, axis=-1) + ba_ref[...])
    gate_i = _sigmoid(jnp.concatenate(gi, axis=-1) + bx_ref[...])
    log_a = (-RG_C) * gate_r * _softplus(-lam_ref[...])
    a = jnp.exp(log_a)
    u = jnp.sqrt(-jnp.tanh(log_a) * (1.0 + a * a)) * (gate_i * xc)

    row = lax.broadcasted_iota(jnp.int32, (ts, width), 0) & (SUBLANES - 1)
    d = 1
    while d < SUBLANES:
        keep = row >= d
        u = jnp.where(keep, a * pltpu.roll(u, d, axis=0) + u, u)
        a = jnp.where(keep, a * pltpu.roll(a, d, axis=0), a)
        d *= 2
    a_s[...] = a
    u_s[...] = u

    def body(j, h):
        r0 = pl.multiple_of(j * SUBLANES, SUBLANES)
        hb = a_s[pl.ds(r0, SUBLANES), :] * h + u_s[pl.ds(r0, SUBLANES), :]
        u_s[pl.ds(r0, SUBLANES), :] = hb
        return hb[SUBLANES - 1:SUBLANES, :]

    hcar[0:1, :] = lax.fori_loop(0, ts // SUBLANES, body, hcar[0:1, :])
    o_ref[...] = (u_s[...] * _silu(ga_ref[...])).astype(o_ref.dtype)


def _rglru(z, conv_w, conv_b, w_a, b_a, w_x, b_x, lam, bsz, seq, ts):
    width = conv_w.shape[1]
    nt = seq // ts
    row = lambda b, i: (b * nt + i, 0)
    full2 = lambda b, i: (0, 0)
    full3 = lambda b, i: (0, 0, 0)
    return pl.pallas_call(
        _rglru_kernel,
        grid=(bsz, nt),
        in_specs=[pl.BlockSpec((ts, width), row),
                  pl.BlockSpec((ts, width), lambda b, i: (b * nt + i, 1)),
                  pl.BlockSpec(conv_w.shape, full2),
                  pl.BlockSpec((1, width), full2),
                  pl.BlockSpec(w_a.shape, full3),
                  pl.BlockSpec((1, width), full2),
                  pl.BlockSpec(w_x.shape, full3),
                  pl.BlockSpec((1, width), full2),
                  pl.BlockSpec((1, width), full2)],
        out_specs=pl.BlockSpec((ts, width), row),
        out_shape=jax.ShapeDtypeStruct((bsz * seq, width), BF16),
        scratch_shapes=[pltpu.VMEM((ts + SUBLANES, width), F32),
                        pltpu.VMEM((ts, width), F32),
                        pltpu.VMEM((ts, width), F32),
                        pltpu.VMEM((SUBLANES, width), F32)],
        compiler_params=_cparams("parallel", "arbitrary"),
        name="l0_rglru",
    )(z, z, conv_w, conv_b, w_a, b_a, w_x, b_x, lam)


def _hgrn2_kernel(q_ref, f_ref, v_ref, g_ref, lbl_ref, gn_ref, o_ref, st):
    tc = q_ref.shape[0]

    @pl.when(pl.program_id(2) == 0)
    def _():
        st[...] = jnp.zeros_like(st)

    lbl = lbl_ref[...]
    e = jnp.exp(lbl - jnp.max(lbl, axis=0, keepdims=True))
    lb = e[0:1, :] / jnp.sum(e, axis=0, keepdims=True)

    ri = lax.broadcasted_iota(jnp.int32, (CHUNK, CHUNK), 0)
    ci = lax.broadcasted_iota(jnp.int32, (CHUNK, CHUNK), 1)
    causal = ri >= ci
    tril = jnp.where(causal, 1.0, 0.0).astype(F32)

    C = range(tc // CHUNK)
    sl = [slice(n * CHUNK, (n + 1) * CHUNK) for n in C]
    sg = _sigmoid(f_ref[...])
    log_f = jnp.log(lb + (1.0 - lb) * sg)
    k_all = (1.0 - lb) * (1.0 - sg)
    cum_w = _mm_exact(tril, jnp.concatenate([log_f[s] for s in sl], axis=1))
    cum = [cum_w[:, n * HG_HEAD_DIM:(n + 1) * HG_HEAD_DIM] for n in C]
    total = [cum[n][CHUNK - 1:CHUNK, :] for n in C]
    k = [k_all[s] for s in sl]
    v = [v_ref[s, :] for s in sl]
    q_dec = [(q_ref[sl[n], :] * jnp.exp(cum[n])).astype(BF16) for n in C]
    k_inv = [(k[n] * jnp.exp(-cum[n])).astype(BF16) for n in C]
    k_end = [(k[n] * jnp.exp(total[n] - cum[n])).astype(BF16) for n in C]
    scores = [jnp.where(causal, _mm_nt(q_dec[n], k_inv[n]), 0.0).astype(BF16) for n in C]
    upd = [_mm(v[n].T.astype(BF16), k_end[n]) for n in C]
    o = [_mm(scores[n], v[n].astype(BF16)) for n in C]
    state = st[...]
    states = []
    for n in C:
        states.append(state.astype(BF16))
        state = state * jnp.exp(total[n]) + upd[n]
    st[...] = state
    o = jnp.concatenate([o[n] + _mm_nt(q_dec[n], states[n]) for n in C], axis=0)
    o = o * lax.rsqrt(jnp.mean(o * o, axis=-1, keepdims=True) + RMS_EPS) * gn_ref[...]
    o_ref[...] = (o * _silu(g_ref[...])).astype(o_ref.dtype)


def _hgrn2(z, lb_logits, hg_g, bsz, seq, tc, col0):
    width = lb_logits.shape[1]
    heads = width // HG_HEAD_DIM
    nt = seq // tc

    def col(k):
        return lambda b, h, i: (b * nt + i, (col0 + k * width) // HG_HEAD_DIM + h)

    return pl.pallas_call(
        _hgrn2_kernel,
        grid=(bsz, heads, nt),
        in_specs=[pl.BlockSpec((tc, HG_HEAD_DIM), col(0)),
                  pl.BlockSpec((tc, HG_HEAD_DIM), col(1)),
                  pl.BlockSpec((tc, HG_HEAD_DIM), col(2)),
                  pl.BlockSpec((tc, HG_HEAD_DIM), col(3)),
                  pl.BlockSpec((lb_logits.shape[0], HG_HEAD_DIM), lambda b, h, i: (0, h)),
                  pl.BlockSpec((1, HG_HEAD_DIM), lambda b, h, i: (0, 0))],
        out_specs=pl.BlockSpec((tc, HG_HEAD_DIM), lambda b, h, i: (b * nt + i, h)),
        out_shape=jax.ShapeDtypeStruct((bsz * seq, width), BF16),
        scratch_shapes=[pltpu.VMEM((HG_HEAD_DIM, HG_HEAD_DIM), F32)],
        compiler_params=_cparams("parallel", "parallel", "arbitrary"),
        name="l0_hgrn2",
    )(z, z, z, z, lb_logits, hg_g)


def _outproj0_kernel(ya_ref, yb_ref, wa_ref, wb_ref, x_ref, g_ref, h_ref, u_ref):
    h = x_ref[...] + _mm(ya_ref[...], wa_ref[...]) + _mm(yb_ref[...], wb_ref[...])
    h_ref[...] = h
    u_ref[...] = _rms(h, g_ref[...])


def _outproj0(ya, yb, w_a, w_b, x, g, tm):
    t, d = x.shape
    row = lambda i: (i, 0)
    full = lambda i: (0, 0)
    return pl.pallas_call(
        _outproj0_kernel,
        grid=(t // tm,),
        in_specs=[pl.BlockSpec((tm, ya.shape[1]), row),
                  pl.BlockSpec((tm, yb.shape[1]), row),
                  pl.BlockSpec(w_a.shape, full),
                  pl.BlockSpec(w_b.shape, full),
                  pl.BlockSpec((tm, d), row),
                  pl.BlockSpec((1, d), full)],
        out_specs=[pl.BlockSpec((tm, d), row), pl.BlockSpec((tm, d), row)],
        out_shape=[jax.ShapeDtypeStruct((t, d), F32), jax.ShapeDtypeStruct((t, d), F32)],
        compiler_params=_cparams("parallel"),
        name="l0_outproj",
    )(ya, yb, w_a, w_b, x, g)


def _shifted(u, up_ref, first_of_seq):
    prev_last = jnp.where(first_of_seq, 0.0, up_ref[SUBLANES - 1:SUBLANES, :])
    row = lax.broadcasted_iota(jnp.int32, u.shape, 0)
    return jnp.where(row == 0, prev_last, pltpu.roll(u, 1, axis=0))


def _pair_ones():
    r = lax.broadcasted_iota(jnp.int32, (2 * LANES, LANES), 0) & (LANES - 1)
    c = lax.broadcasted_iota(jnp.int32, (2 * LANES, LANES), 1)
    return jnp.where((r < RW_HEAD_DIM) == (c < RW_HEAD_DIM), 1.0, 0.0).astype(BF16)


def _hi_lo(x):
    hi = x.astype(BF16)
    return hi, (x - hi.astype(F32)).astype(BF16)


def _headsum(x, ones2, split=True):
    n, nblk = x.shape[0], x.shape[1] // LANES
    xs = jnp.concatenate([x[:, i * LANES:(i + 1) * LANES] for i in range(nblk)], axis=0)
    if split:
        s = _mm(jnp.concatenate(_hi_lo(xs), axis=1), ones2)
    else:
        s = _mm(xs.astype(BF16), ones2[0:LANES])
    return jnp.concatenate([s[i * n:(i + 1) * n] for i in range(nblk)], axis=1)


def _front_kernel(u_ref, up_ref, mu_ref, wr_ref, wk_ref, wv_ref, wg_ref, w1_ref, w2_ref, w0_ref,
                  a1_ref, a2_ref, a0_ref, kk_ref, ka_ref, rk_ref,
                  rw_o, kkw_o, kt_o, bt_o, kend_o, bend_o, vb_o, bon_o, gate_o, dec_o,
                  *, steps_per_seq):
    tm = u_ref.shape[0]
    width = w0_ref.shape[1]
    L = CHUNK
    nch = tm // L
    u = u_ref[...]
    delta = _shifted(u, up_ref, pl.program_id(0) % steps_per_seq == 0) - u
    x_r, x_w, x_k, x_v, x_a, x_g = ((u + delta * mu_ref[i:i + 1, :]).astype(BF16)
                                    for i in range(6))
    zw = jnp.tanh(_mm(x_w, w1_ref[...])).astype(BF16)
    za = _mm(x_a, a1_ref[...]).astype(BF16)
    ones2 = _pair_ones()
    ri = lax.broadcasted_iota(jnp.int32, (L, L), 0)
    ci = lax.broadcasted_iota(jnp.int32, (L, L), 1)
    tril = jnp.where(ri >= ci, 1.0, 0.0).astype(BF16)
    tril2 = jnp.concatenate([tril, tril], axis=1)

    for g in range(width // FRONT_COLS):
        cols = slice(g * FRONT_COLS, (g + 1) * FRONT_COLS)
        r = _mm(x_r, wr_ref[g])
        kraw = _mm(x_k, wk_ref[g])
        v = _mm(x_v, wv_ref[g])
        gate_o[:, cols] = _silu(_mm(x_g, wg_ref[g]))
        lw = -DECAY_SCALE * _sigmoid(w0_ref[:, cols] + _mm(zw, w2_ref[:, cols]))
        a = _sigmoid(a0_ref[:, cols] + _mm(za, a2_ref[:, cols]))
        kkp = kraw * kk_ref[:, cols]
        kk = kkp * lax.rsqrt(jnp.maximum(_headsum(kkp * kkp, ones2, split=False), 1e-24))
        k = kraw * (1.0 + (a - 1.0) * ka_ref[:, cols])
        b = kk * a
        bon_o[:, cols] = _headsum(r * k * rk_ref[:, cols], ones2, split=False) * v
        lw_hi, lw_lo = _hi_lo(jnp.concatenate([lw[n * L:(n + 1) * L] for n in range(nch)], axis=1))
        cum_w = _mm(tril2, jnp.concatenate([lw_hi, lw_lo], axis=0))
        cum = jnp.concatenate([cum_w[:, n * FRONT_COLS:(n + 1) * FRONT_COLS] for n in range(nch)],
                              axis=0)
        totals = [cum_w[L - 1:L, n * FRONT_COLS:(n + 1) * FRONT_COLS] for n in range(nch)]
        decs = [jnp.exp(t) for t in totals]
        for n in range(nch):
            dec_o[n, :, cols] = decs[n]
        w_out = jnp.exp(-cum)
        w_end = w_out * jnp.concatenate([jnp.broadcast_to(t, (L, FRONT_COLS)) for t in decs], axis=0)
        rw_o[:, cols] = (r * jnp.exp(cum)).astype(BF16)
        kkw_o[:, cols] = (kk * jnp.exp(cum - lw)).astype(BF16)
        kt_o[:, cols] = (k * w_out).astype(BF16)
        bt_o[:, cols] = (b * w_out).astype(BF16)
        kend_o[:, cols] = (k * w_end).astype(BF16)
        bend_o[:, cols] = (b * w_end).astype(BF16)
        vb_o[:, cols] = v.astype(BF16)


def _front(u, mu, w_r, w_k, w_v, w_g, w1, w2, w0, a1, a2, a0, k_k, k_a, r_k, seq, tm):
    t, d = u.shape
    n = w_r.shape[1]
    rb = tm // SUBLANES
    row = lambda i: (i, 0)
    w_r, w_k, w_v, w_g = (w.reshape(d, n // FRONT_COLS, FRONT_COLS).transpose(1, 0, 2)
                          for w in (w_r, w_k, w_v, w_g))

    def const(shape):
        return pl.BlockSpec(shape, lambda i: (0,) * len(shape), pipeline_mode=pl.Buffered(1))

    tok = pl.BlockSpec((tm, n), row)
    tok16 = jax.ShapeDtypeStruct((t, n), BF16)
    tok32 = jax.ShapeDtypeStruct((t, n), F32)
    return pl.pallas_call(
        functools.partial(_front_kernel, steps_per_seq=seq // tm),
        grid=(t // tm,),
        in_specs=[pl.BlockSpec((tm, d), row),
                  pl.BlockSpec((SUBLANES, d), lambda i: (jnp.maximum(i * rb - 1, 0), 0)),
                  const(mu.shape),
                  const(w_r.shape), const(w_k.shape), const(w_v.shape), const(w_g.shape),
                  const(w1.shape), const(w2.shape), const((1, n)),
                  const(a1.shape), const(a2.shape), const((1, n)),
                  const((1, n)), const((1, n)), const((1, n))],
        out_specs=[tok] * 9 + [pl.BlockSpec((tm // CHUNK, 1, n), lambda i: (i, 0, 0))],
        out_shape=[tok16] * 7 + [tok32] * 2 + [jax.ShapeDtypeStruct((t // CHUNK, 1, n), F32)],
        compiler_params=_cparams("parallel"),
        name="l1_front",
    )(u, u, mu, w_r, w_k, w_v, w_g, w1, w2, w0, a1, a2, a0, k_k, k_a, r_k)


def _rwkv_kernel(rw_ref, kkw_ref, kt_ref, bt_ref, kend_ref, bend_ref, vb_ref, bon_ref, g_ref,
                 dec_ref, lg_ref, lb_ref, o_ref, st, rt_s, y0_s, q_s, n_s, *, pairs):
    tc = rw_ref.shape[0]
    L = CHUNK
    hd = RW_HEAD_DIM
    nchunks = tc // L

    @pl.when(pl.program_id(2) == 0)
    def _():
        st[...] = jnp.zeros_like(st)

    row = lax.broadcasted_iota(jnp.int32, (L, LANES), 0)
    lane = lax.broadcasted_iota(jnp.int32, (L, LANES), 1)
    lane_h = lane & (hd - 1)
    strict = lane_h < row
    incl = lane_h <= row
    eye = jnp.where(lane_h == row, 1.0, 0.0).astype(F32)
    head_a = lane < hd
    row2 = lax.broadcasted_iota(jnp.int32, (2 * L, LANES), 0)
    lane2 = lax.broadcasted_iota(jnp.int32, (2 * L, LANES), 1)
    same_head = (row2 < hd) == (lane2 < hd)
    ones2 = _pair_ones()
    pcols = [slice(p * LANES, (p + 1) * LANES) for p in range(pairs)]

    def split(x):
        x = x.astype(F32)
        return jnp.concatenate([jnp.where(head_a, x, 0.0), jnp.where(head_a, 0.0, x)],
                               axis=0).astype(BF16)

    def build(g):
        jobs = [(c, p, slice(c * L, (c + 1) * L), pcols[p])
                for c in range(g * BUILD_CHUNKS, (g + 1) * BUILD_CHUNKS) for p in range(pairs)]
        J = range(len(jobs))
        ld = lambda ref: [ref[rows, cols] for (_, _, rows, cols) in jobs]
        rw, kkw, vb, bend = ld(rw_ref), ld(kkw_ref), ld(vb_ref), ld(bend_ref)
        zkb = [jnp.concatenate([split(tk), split(tb)], axis=0)
               for tk, tb in zip(ld(kt_ref), ld(bt_ref))]
        lhs = [jnp.concatenate([kkw[j], rw[j]], axis=0) for j in J]
        p_kb = [_mm_nt(lhs[j], zkb[j]) for j in J]
        p_k = [t[:, 0:LANES] for t in p_kb]
        p_b = [t[:, LANES:] for t in p_kb]
        yield
        amat = [jnp.where(strict, p_b[j][0:L], 0.0) for j in J]
        x = [eye - amat[j] for j in J]
        m = [_mm(amat[j].astype(BF16), split(amat[j])) for j in J]
        yield
        for _ in range(L.bit_length() - 3):
            xm = [_mm(jnp.concatenate([x[j], m[j]], axis=0).astype(BF16), split(m[j])) for j in J]
            x = [x[j] + xm[j][0:L] for j in J]
            m = [xm[j][L:2 * L] for j in J]
            yield
        x = [(x[j] + _mm(x[j].astype(BF16), split(m[j]))).astype(BF16) for j in J]
        yield
        a_kr = [jnp.concatenate([jnp.where(strict, p_k[j][0:L], 0.0),
                                 jnp.where(incl, p_k[j][L:2 * L], 0.0)], axis=0).astype(BF16)
                for j in J]
        akv = [_mm(a_kr[j], split(vb[j])) for j in J]
        yield
        ku = [_mm(x[j], jnp.concatenate([split(kkw[j]), split(akv[j][0:L])], axis=1)) for j in J]
        yield
        a_rb = [jnp.where(incl, p_b[j][L:2 * L], 0.0).astype(BF16) for j in J]
        ru = [_mm(a_rb[j], jnp.concatenate([split(ku[j][:, 0:LANES]), split(ku[j][:, LANES:])],
                                           axis=1)) for j in J]
        for j, (_, _, rows, cols) in enumerate(jobs):
            rt_s[rows, cols] = (rw[j].astype(F32) - ru[j][:, 0:LANES]).astype(BF16)
            y0_s[rows, cols] = akv[j][L:2 * L] - ru[j][:, LANES:]
        yield
        vu_t = [jnp.concatenate([vb[j].astype(F32), -ku[j][:, LANES:]], axis=0).T.astype(BF16)
                for j in J]
        kt_t = [ku[j][:, 0:LANES].T.astype(BF16) for j in J]
        kb_end = [jnp.concatenate([t, bend[j]], axis=0) for j, t in enumerate(ld(kend_ref))]
        nmat = [_mm(vu_t[j], kb_end[j]) for j in J]
        yield
        qmat = [_mm(kt_t[j], bend[j]) for j in J]
        for j, (c, p, _, _) in enumerate(jobs):
            n_s[c, p] = jnp.where(same_head, nmat[j], 0.0)
            q_s[c, p] = jnp.where(same_head, qmat[j], 0.0).astype(BF16)
        yield

    def scan_step(c):
        rows = slice(c * L, (c + 1) * L)
        dec = dec_ref[c]
        for p in range(pairs):
            s = st[p]
            sb = s.astype(BF16)
            y0_s[rows, pcols[p]] = y0_s[rows, pcols[p]] + _mm_nt(rt_s[rows, pcols[p]], sb)
            st[p] = s * dec[0:1, pcols[p]] + n_s[c, p] - _mm(sb, q_s[c, p])

    def finish(rows):
        y = y0_s[rows, :]
        yc = y - _headsum(y, ones2) * (1.0 / hd)
        var = _headsum(yc * yc, ones2) * (1.0 / hd)
        yn = yc * lax.rsqrt(var + GN_EPS) * lg_ref[...] + lb_ref[...]
        o_ref[rows, :] = ((yn + bon_ref[rows, :]) * g_ref[rows, :]).astype(o_ref.dtype)

    groups = nchunks // BUILD_CHUNKS
    half = BUILD_CHUNKS * L // 2
    for blk in range(groups + 2):
        scans, fins = [], []
        if 1 <= blk <= groups:
            scans = [functools.partial(scan_step, c)
                     for c in range((blk - 1) * BUILD_CHUNKS, blk * BUILD_CHUNKS)]
        if blk >= 2:
            r0 = (blk - 2) * BUILD_CHUNKS * L
            fins = [functools.partial(finish, slice(r0 + i * half, r0 + (i + 1) * half))
                    for i in range(2)]
        side = [t for pair in itertools.zip_longest(scans, fins) for t in pair if t is not None]
        if blk < groups:
            stride = max(1, BUILD_STAGES // (len(side) + 1))
            n_stages = 0
            for i, _ in enumerate(build(blk)):
                n_stages += 1
                if side and i % stride == stride - 1:
                    side.pop(0)()
            assert n_stages == BUILD_STAGES
        for thunk in side:
            thunk()


def _rwkv(rw, kkw, kt, bt, kend, bend, vb, bon, gate, dec, lnx_g, lnx_b, bsz, seq, tc, pairs):
    t, width = rw.shape
    wb = pairs * LANES
    nt = seq // tc
    nchunks = tc // CHUNK
    tok = pl.BlockSpec((tc, wb), lambda b, p, i: (b * nt + i, p))
    par = pl.BlockSpec((1, wb), lambda b, p, i: (0, p))
    return pl.pallas_call(
        functools.partial(_rwkv_kernel, pairs=pairs),
        grid=(bsz, width // wb, nt),
        in_specs=[tok] * 9 + [pl.BlockSpec((nchunks, 1, wb), lambda b, p, i: (b * nt + i, 0, p)),
                              par, par],
        out_specs=tok,
        out_shape=jax.ShapeDtypeStruct((t, width), BF16),
        scratch_shapes=[pltpu.VMEM((pairs, LANES, LANES), F32),
                        pltpu.VMEM((tc, wb), BF16),
                        pltpu.VMEM((tc, wb), F32),
                        pltpu.VMEM((nchunks, pairs, LANES, LANES), BF16),
                        pltpu.VMEM((nchunks, pairs, LANES, LANES), F32)],
        compiler_params=_cparams("parallel", "parallel", "arbitrary"),
        name="l1_rwkv7",
    )(rw, kkw, kt, bt, kend, bend, vb, bon, gate, dec, lnx_g, lnx_b)


def _outproj1_kernel(y_ref, w_ref, h_ref, g_ref, o_ref):
    h = h_ref[...] + _mm(y_ref[...], w_ref[...])
    o_ref[...] = _rms(h, g_ref[...])


def _outproj1(y, w, h, g, tm):
    t, d = h.shape
    row = lambda i: (i, 0)
    full = lambda i: (0, 0)
    return pl.pallas_call(
        _outproj1_kernel,
        grid=(t // tm,),
        in_specs=[pl.BlockSpec((tm, y.shape[1]), row), pl.BlockSpec(w.shape, full),
                  pl.BlockSpec((tm, d), row), pl.BlockSpec((1, d), full)],
        out_specs=pl.BlockSpec((tm, d), row),
        out_shape=jax.ShapeDtypeStruct((t, d), F32),
        compiler_params=_cparams("parallel"),
        name="l1_outproj_final",
    )(y, w, h, g)


def _tile(n, pref):
    t = min(n, pref)
    assert n % t == 0, (n, pref)
    return t


def _layer0(x2, bsz, seq, norm_g, w_in, conv_w, conv_b, w_a, b_a, w_x, b_x, lam, lb_logits,
            hg_g, w_out, next_norm_g):
    t, d = x2.shape
    width = conv_w.shape[1]
    row = lambda p: p.reshape(1, -1)
    z = _norm_matmul(x2, row(norm_g), w_in.astype(BF16), _tile(t, 512), _tile(w_in.shape[1], 2048))
    ya = _rglru(z, conv_w, row(conv_b), w_a.astype(BF16), row(b_a), w_x.astype(BF16), row(b_x),
                row(lam), bsz, seq, _tile(seq, 512))
    yb = _hgrn2(z, lb_logits, row(hg_g), bsz, seq, _tile(seq, 1024), 2 * width)
    w_out = w_out.astype(BF16)
    return _outproj0(ya, yb, w_out[:width], w_out[width:], x2, row(next_norm_g), _tile(t, 512))


def _layer1(h, u, bsz, seq, mu, w_r, w_k, w_v, w_g, w0, w1, w2, a0, a1, a2, k_k, k_a, r_k,
            lnx_g, lnx_b, w_o, final_g):
    t, d = h.shape
    row = lambda p: p.reshape(1, -1)
    bf = lambda w: w.astype(BF16)
    ops = _front(u, mu, bf(w_r), bf(w_k), bf(w_v), bf(w_g), bf(w1), bf(w2), row(w0), bf(a1),
                 bf(a2), row(a0), row(k_k), row(k_a), row(r_k), seq, _tile(seq, 256))
    y = _rwkv(*ops, row(lnx_g), row(lnx_b), bsz, seq, _tile(seq, 1024), pairs=4)
    return _outproj1(y, bf(w_o), h, row(final_g), _tile(t, 512))


def kernel(x, ab_norm_g, ab_w_in, rg_conv_w, rg_conv_b, rg_w_a, rg_b_a, rg_w_x, rg_b_x, rg_lambda, hg_lb_logits, hg_norm_g, ab_w_out, c_norm_g, c_mu, c_w_r, c_w_k, c_w_v, c_w_g, c_w0, c_w1, c_w2, c_a0, c_a1, c_a2, c_k_k, c_k_a, c_r_k, c_lnx_g, c_lnx_b, c_w_o, final_g):
    bsz, seq, d = x.shape
    assert ab_norm_g.shape[0] == 1 and c_norm_g.shape[0] == 1, "two-layer trunk only"
    x2 = x.reshape(bsz * seq, d)
    h, u = _layer0(x2, bsz, seq, ab_norm_g[0], ab_w_in[0], rg_conv_w[0], rg_conv_b[0], rg_w_a[0],
                   rg_b_a[0], rg_w_x[0], rg_b_x[0], rg_lambda[0], hg_lb_logits, hg_norm_g[0],
                   ab_w_out[0], c_norm_g[0])
    out = _layer1(h, u, bsz, seq, c_mu[0], c_w_r[0], c_w_k[0], c_w_v[0], c_w_g[0], c_w0[0],
                  c_w1[0], c_w2[0], c_a0[0], c_a1[0], c_a2[0], c_k_k[0], c_k_a[0],
                  c_r_k[0].reshape(-1), c_lnx_g[0], c_lnx_b[0], c_w_o[0], final_g)
    return out.reshape(bsz, seq, d)
```

```python
import functools
import itertools

import jax
import jax.numpy as jnp
from jax import lax
from jax.experimental import pallas as pl
from jax.experimental.pallas import tpu as pltpu

F32 = jnp.float32
BF16 = jnp.bfloat16

RMS_EPS = 1e-6
GN_EPS = 64e-5
DECAY_SCALE = 0.6065306597126334
RG_C = 8.0
RG_BLOCKS = 8
RG_CONV = 4
HG_HEAD_DIM = 128
RW_HEAD_DIM = 64
CHUNK = 64
BUILD_CHUNKS = 4
BUILD_STAGES = 12
FRONT_COLS = 512
LANES = 128
SUBLANES = 8
VMEM_LIMIT = 56 * 1024 * 1024


def _cparams(*sem):
    return pltpu.CompilerParams(dimension_semantics=sem, vmem_limit_bytes=VMEM_LIMIT)


def _mm(a, b):
    return jnp.dot(a, b, preferred_element_type=F32)


def _mm_nt(a, b):
    return lax.dot_general(a, b, (((1,), (1,)), ((), ())), preferred_element_type=F32)


def _hi_lo(x):
    hi = x.astype(BF16)
    return hi, (x - hi.astype(F32)).astype(BF16)


def _rms(x, g):
    return x * lax.rsqrt(jnp.mean(x * x, axis=-1, keepdims=True) + RMS_EPS) * g


def _sigmoid(x):
    return 1.0 / (1.0 + jnp.exp(-x))


def _silu(x):
    return x * _sigmoid(x)


def _softplus(x):
    return jnp.maximum(x, 0.0) + jnp.log(1.0 + jnp.exp(-jnp.abs(x)))


def _norm_matmul_kernel(x_ref, g_ref, w_ref, o_ref):
    xn = _rms(x_ref[...], g_ref[...]).astype(BF16)
    o_ref[...] = _mm(xn, w_ref[...])


def _norm_matmul(x, g, w, tm, tn):
    t, d = x.shape
    n = w.shape[1]
    return pl.pallas_call(
        _norm_matmul_kernel,
        grid=(n // tn, t // tm),
        in_specs=[pl.BlockSpec((tm, d), lambda j, i: (i, 0)),
                  pl.BlockSpec((1, d), lambda j, i: (0, 0)),
                  pl.BlockSpec((d, tn), lambda j, i: (0, j))],
        out_specs=pl.BlockSpec((tm, tn), lambda j, i: (i, j)),
        out_shape=jax.ShapeDtypeStruct((t, n), F32),
        compiler_params=_cparams("parallel", "parallel"),
        name="l0_norm_inproj",
    )(x, g, w)


def _rglru_kernel(xa_ref, ga_ref, cw_ref, cb_ref, wa_ref, ba_ref, wx_ref, bx_ref, lam_ref,
                  o_ref, xbuf, a_s, u_s, hcar):
    ts, width = xa_ref.shape
    bd = width // RG_BLOCKS

    @pl.when(pl.program_id(1) == 0)
    def _():
        xbuf[0:SUBLANES, :] = jnp.zeros((SUBLANES, width), F32)
        hcar[...] = jnp.zeros_like(hcar)

    xa = xa_ref[...]
    xbuf[SUBLANES:SUBLANES + ts, :] = xa
    xc = cb_ref[...] + cw_ref[RG_CONV - 1:RG_CONV, :] * xa
    for j in range(1, RG_CONV):
        xc = xc + cw_ref[RG_CONV - 1 - j:RG_CONV - j, :] * xbuf[pl.ds(SUBLANES - j, ts), :]
    xbuf[0:SUBLANES, :] = xa[ts - SUBLANES:ts, :]

    xcb = xc.astype(BF16)
    gr, gi = [], []
    for n in range(RG_BLOCKS):
        blk = xcb[:, n * bd:(n + 1) * bd]
        gr.append(_mm(blk, wa_ref[n]))
        gi.append(_mm(blk, wx_ref[n]))
    gate_r = _sigmoid(jnp.concatenate(gr, axis=-1) + ba_ref[...])
    gate_i = _sigmoid(jnp.concatenate(gi, axis=-1) + bx_ref[...])
    log_a = (-RG_C) * gate_r * _softplus(-lam_ref[...])
    a = jnp.exp(log_a)
    u = jnp.sqrt(-jnp.tanh(log_a) * (1.0 + a * a)) * (gate_i * xc)

    a = a.reshape(ts // SUBLANES, SUBLANES, width)
    u = u.reshape(ts // SUBLANES, SUBLANES, width)
    row = lax.broadcasted_iota(jnp.int32, a.shape, 1)
    d = 1
    while d < SUBLANES:
        keep = row >= d
        u = jnp.where(keep, a * pltpu.roll(u, d, axis=1) + u, u)
        a = jnp.where(keep, a * pltpu.roll(a, d, axis=1), a)
        d *= 2
    a_s[...] = a.reshape(ts, width)
    u_s[...] = u.reshape(ts, width)

    def body(j, h):
        r0 = pl.multiple_of(j * SUBLANES, SUBLANES)
        hb = a_s[pl.ds(r0, SUBLANES), :] * h + u_s[pl.ds(r0, SUBLANES), :]
        u_s[pl.ds(r0, SUBLANES), :] = hb
        return hb[SUBLANES - 1:SUBLANES, :]

    hcar[0:1, :] = lax.fori_loop(0, ts // SUBLANES, body, hcar[0:1, :])
    o_ref[...] = (u_s[...] * _silu(ga_ref[...])).astype(o_ref.dtype)


def _rglru(z, conv_w, conv_b, w_a, b_a, w_x, b_x, lam, bsz, seq, ts):
    width = conv_w.shape[1]
    nt = seq // ts
    row = lambda b, i: (b * nt + i, 0)
    full2 = lambda b, i: (0, 0)
    full3 = lambda b, i: (0, 0, 0)
    return pl.pallas_call(
        _rglru_kernel,
        grid=(bsz, nt),
        in_specs=[pl.BlockSpec((ts, width), row),
                  pl.BlockSpec((ts, width), lambda b, i: (b * nt + i, 1)),
                  pl.BlockSpec(conv_w.shape, full2),
                  pl.BlockSpec((1, width), full2),
                  pl.BlockSpec(w_a.shape, full3),
                  pl.BlockSpec((1, width), full2),
                  pl.BlockSpec(w_x.shape, full3),
                  pl.BlockSpec((1, width), full2),
                  pl.BlockSpec((1, width), full2)],
        out_specs=pl.BlockSpec((ts, width), row),
        out_shape=jax.ShapeDtypeStruct((bsz * seq, width), BF16),
        scratch_shapes=[pltpu.VMEM((ts + SUBLANES, width), F32),
                        pltpu.VMEM((ts, width), F32),
                        pltpu.VMEM((ts, width), F32),
                        pltpu.VMEM((SUBLANES, width), F32)],
        compiler_params=_cparams("parallel", "arbitrary"),
        name="l0_rglru",
    )(z, z, conv_w, conv_b, w_a, b_a, w_x, b_x, lam)


def _hgrn2_kernel(q_ref, f_ref, v_ref, g_ref, lbl_ref, gn_ref, o_ref, st):
    tc = q_ref.shape[0]

    @pl.when(pl.program_id(2) == 0)
    def _():
        st[...] = jnp.zeros_like(st)

    lbl = lbl_ref[...]
    e = jnp.exp(lbl - jnp.max(lbl, axis=0, keepdims=True))
    lb = e[0:1, :] / jnp.sum(e, axis=0, keepdims=True)

    ri = lax.broadcasted_iota(jnp.int32, (CHUNK, CHUNK), 0)
    ci = lax.broadcasted_iota(jnp.int32, (CHUNK, CHUNK), 1)
    causal = ri >= ci
    tril = jnp.where(causal, 1.0, 0.0).astype(BF16)
    tril2 = jnp.concatenate([tril, tril], axis=1)

    C = range(tc // CHUNK)
    sl = [slice(n * CHUNK, (n + 1) * CHUNK) for n in C]
    sg = _sigmoid(f_ref[...])
    log_f = jnp.log(lb + (1.0 - lb) * sg)
    k_all = (1.0 - lb) * (1.0 - sg)
    lf_hi, lf_lo = _hi_lo(jnp.concatenate([log_f[s] for s in sl], axis=1))
    cum_w = _mm(tril2, jnp.concatenate([lf_hi, lf_lo], axis=0))
    cum = [cum_w[:, n * HG_HEAD_DIM:(n + 1) * HG_HEAD_DIM] for n in C]
    dec = [jnp.exp(cum[n][CHUNK - 1:CHUNK, :]) for n in C]
    k = [k_all[s] for s in sl]
    v = [v_ref[s, :] for s in sl]
    q_dec = [(q_ref[sl[n], :] * jnp.exp(cum[n])).astype(BF16) for n in C]
    k_inv = [k[n] * jnp.exp(-cum[n]) for n in C]
    k_end = [(k_inv[n] * dec[n]).astype(BF16) for n in C]
    k_inv = [t.astype(BF16) for t in k_inv]
    scores = [jnp.where(causal, _mm_nt(q_dec[n], k_inv[n]), 0.0).astype(BF16) for n in C]
    upd = [_mm(v[n].T.astype(BF16), k_end[n]) for n in C]
    o = [_mm(scores[n], v[n].astype(BF16)) for n in C]
    state = st[...]
    states = []
    for n in C:
        states.append(state.astype(BF16))
        state = state * dec[n] + upd[n]
    st[...] = state
    o = jnp.concatenate([o[n] + _mm_nt(q_dec[n], states[n]) for n in C], axis=0)
    o = o * lax.rsqrt(jnp.mean(o * o, axis=-1, keepdims=True) + RMS_EPS) * gn_ref[...]
    o_ref[...] = (o * _silu(g_ref[...])).astype(o_ref.dtype)


def _hgrn2(z, lb_logits, hg_g, bsz, seq, tc, col0):
    width = lb_logits.shape[1]
    heads = width // HG_HEAD_DIM
    nt = seq // tc

    def col(k):
        return lambda b, h, i: (b * nt + i, (col0 + k * width) // HG_HEAD_DIM + h)

    return pl.pallas_call(
        _hgrn2_kernel,
        grid=(bsz, heads, nt),
        in_specs=[pl.BlockSpec((tc, HG_HEAD_DIM), col(0)),
                  pl.BlockSpec((tc, HG_HEAD_DIM), col(1)),
                  pl.BlockSpec((tc, HG_HEAD_DIM), col(2)),
                  pl.BlockSpec((tc, HG_HEAD_DIM), col(3)),
                  pl.BlockSpec((lb_logits.shape[0], HG_HEAD_DIM), lambda b, h, i: (0, h)),
                  pl.BlockSpec((1, HG_HEAD_DIM), lambda b, h, i: (0, 0))],
        out_specs=pl.BlockSpec((tc, HG_HEAD_DIM), lambda b, h, i: (b * nt + i, h)),
        out_shape=jax.ShapeDtypeStruct((bsz * seq, width), BF16),
        scratch_shapes=[pltpu.VMEM((HG_HEAD_DIM, HG_HEAD_DIM), F32)],
        compiler_params=_cparams("parallel", "parallel", "arbitrary"),
        name="l0_hgrn2",
    )(z, z, z, z, lb_logits, hg_g)


def _outproj0_kernel(ya_ref, yb_ref, wa_ref, wb_ref, x_ref, g_ref, h_ref, u_ref):
    h = x_ref[...] + _mm(ya_ref[...], wa_ref[...]) + _mm(yb_ref[...], wb_ref[...])
    h_ref[...] = h
    u_ref[...] = _rms(h, g_ref[...])


def _outproj0(ya, yb, w_a, w_b, x, g, tm):
    t, d = x.shape
    row = lambda i: (i, 0)
    full = lambda i: (0, 0)
    return pl.pallas_call(
        _outproj0_kernel,
        grid=(t // tm,),
        in_specs=[pl.BlockSpec((tm, ya.shape[1]), row),
                  pl.BlockSpec((tm, yb.shape[1]), row),
                  pl.BlockSpec(w_a.shape, full),
                  pl.BlockSpec(w_b.shape, full),
                  pl.BlockSpec((tm, d), row),
                  pl.BlockSpec((1, d), full)],
        out_specs=[pl.BlockSpec((tm, d), row), pl.BlockSpec((tm, d), row)],
        out_shape=[jax.ShapeDtypeStruct((t, d), F32), jax.ShapeDtypeStruct((t, d), F32)],
        compiler_params=_cparams("parallel"),
        name="l0_outproj",
    )(ya, yb, w_a, w_b, x, g)


def _shifted(u, up_ref, first_of_seq):
    prev_last = jnp.where(first_of_seq, 0.0, up_ref[SUBLANES - 1:SUBLANES, :])
    row = lax.broadcasted_iota(jnp.int32, u.shape, 0)
    return jnp.where(row == 0, prev_last, pltpu.roll(u, 1, axis=0))


def _pair_ones():
    r = lax.broadcasted_iota(jnp.int32, (2 * LANES, LANES), 0) & (LANES - 1)
    c = lax.broadcasted_iota(jnp.int32, (2 * LANES, LANES), 1)
    return jnp.where((r < RW_HEAD_DIM) == (c < RW_HEAD_DIM), 1.0, 0.0).astype(BF16)


def _headsum(x, ones2, split=True):
    n, nblk = x.shape[0], x.shape[1] // LANES
    xs = jnp.concatenate([x[:, i * LANES:(i + 1) * LANES] for i in range(nblk)], axis=0)
    if split:
        s = _mm(jnp.concatenate(_hi_lo(xs), axis=1), ones2)
    else:
        s = _mm(xs.astype(BF16), ones2[0:LANES])
    return jnp.concatenate([s[i * n:(i + 1) * n] for i in range(nblk)], axis=1)


def _front_kernel(u_ref, up_ref, mu_ref, wr_ref, wk_ref, wv_ref, wg_ref, w1_ref, w2_ref, w0_ref,
                  a1_ref, a2_ref, a0_ref, kk_ref, ka_ref, rk_ref,
                  rw_o, kkw_o, kt_o, bt_o, kend_o, bend_o, vb_o, bon_o, gate_o, dec_o,
                  *, steps_per_seq):
    tm = u_ref.shape[0]
    width = w0_ref.shape[1]
    L = CHUNK
    nch = tm // L
    u = u_ref[...]
    delta = _shifted(u, up_ref, pl.program_id(0) % steps_per_seq == 0) - u
    x_r, x_w, x_k, x_v, x_a, x_g = ((u + delta * mu_ref[i:i + 1, :]).astype(BF16)
                                    for i in range(6))
    zw = jnp.tanh(_mm(x_w, w1_ref[...])).astype(BF16)
    za = _mm(x_a, a1_ref[...]).astype(BF16)
    ones2 = _pair_ones()
    ri = lax.broadcasted_iota(jnp.int32, (L, L), 0)
    ci = lax.broadcasted_iota(jnp.int32, (L, L), 1)
    tril = jnp.where(ri >= ci, 1.0, 0.0).astype(BF16)
    tril2 = jnp.concatenate([tril, tril], axis=1)

    def project(g):
        cols = slice(g * FRONT_COLS, (g + 1) * FRONT_COLS)
        gate_o[:, cols] = _silu(_mm(x_g, wg_ref[g]))
        return (_mm(x_r, wr_ref[g]), _mm(x_k, wk_ref[g]), _mm(x_v, wv_ref[g]),
                _mm(zw, w2_ref[:, cols]), _mm(za, a2_ref[:, cols]))

    def per_token(g, proj):
        cols = slice(g * FRONT_COLS, (g + 1) * FRONT_COLS)
        r_all, kraw_all, v_all, zw_all, za_all = proj
        for n in range(nch):
            rows = slice(n * L, (n + 1) * L)
            r, kraw, v = r_all[rows], kraw_all[rows], v_all[rows]
            lw = -DECAY_SCALE * _sigmoid(w0_ref[:, cols] + zw_all[rows])
            a = _sigmoid(a0_ref[:, cols] + za_all[rows])
            kkp = kraw * kk_ref[:, cols]
            kk = kkp * lax.rsqrt(jnp.maximum(_headsum(kkp * kkp, ones2, split=False), 1e-24))
            k = kraw * (1.0 + (a - 1.0) * ka_ref[:, cols])
            b = kk * a
            bon_o[rows, cols] = _headsum(r * k * rk_ref[:, cols], ones2, split=False) * v
            cum = _mm(tril2, jnp.concatenate(_hi_lo(lw), axis=0))
            dec = jnp.exp(cum[L - 1:L, :])
            dec_o[n, :, cols] = dec
            w_out = jnp.exp(-cum)
            w_end = w_out * dec
            rw_o[rows, cols] = (r * jnp.exp(cum)).astype(BF16)
            kkw_o[rows, cols] = (kk * jnp.exp(cum - lw)).astype(BF16)
            kt_o[rows, cols] = (k * w_out).astype(BF16)
            bt_o[rows, cols] = (b * w_out).astype(BF16)
            kend_o[rows, cols] = (k * w_end).astype(BF16)
            bend_o[rows, cols] = (b * w_end).astype(BF16)
            vb_o[rows, cols] = v.astype(BF16)

    groups = width // FRONT_COLS
    proj = project(0)
    for g in range(groups):
        nxt = project(g + 1) if g + 1 < groups else None
        per_token(g, proj)
        proj = nxt


def _front(u, mu, w_r, w_k, w_v, w_g, w1, w2, w0, a1, a2, a0, k_k, k_a, r_k, seq, tm):
    t, d = u.shape
    n = w_r.shape[1]
    rb = tm // SUBLANES
    row = lambda i: (i, 0)
    w_r, w_k, w_v, w_g = (w.reshape(d, n // FRONT_COLS, FRONT_COLS).transpose(1, 0, 2)
                          for w in (w_r, w_k, w_v, w_g))

    def const(shape):
        return pl.BlockSpec(shape, lambda i: (0,) * len(shape), pipeline_mode=pl.Buffered(1))

    tok = pl.BlockSpec((tm, n), row)
    tok16 = jax.ShapeDtypeStruct((t, n), BF16)
    tok32 = jax.ShapeDtypeStruct((t, n), F32)
    return pl.pallas_call(
        functools.partial(_front_kernel, steps_per_seq=seq // tm),
        grid=(t // tm,),
        in_specs=[pl.BlockSpec((tm, d), row),
                  pl.BlockSpec((SUBLANES, d), lambda i: (jnp.maximum(i * rb - 1, 0), 0)),
                  const(mu.shape),
                  const(w_r.shape), const(w_k.shape), const(w_v.shape), const(w_g.shape),
                  const(w1.shape), const(w2.shape), const((1, n)),
                  const(a1.shape), const(a2.shape), const((1, n)),
                  const((1, n)), const((1, n)), const((1, n))],
        out_specs=[tok] * 9 + [pl.BlockSpec((tm // CHUNK, 1, n), lambda i: (i, 0, 0))],
        out_shape=[tok16] * 7 + [tok32] * 2 + [jax.ShapeDtypeStruct((t // CHUNK, 1, n), F32)],
        compiler_params=_cparams("parallel"),
        name="l1_front",
    )(u, u, mu, w_r, w_k, w_v, w_g, w1, w2, w0, a1, a2, a0, k_k, k_a, r_k)


def _rwkv_kernel(rw_ref, kkw_ref, kt_ref, bt_ref, kend_ref, bend_ref, vb_ref, bon_ref, g_ref,
                 dec_ref, lg_ref, lb_ref, o_ref, st, rt_s, y0_s, q_s, n_s, *, pairs):
    tc = rw_ref.shape[0]
    L = CHUNK
    hd = RW_HEAD_DIM
    nchunks = tc // L

    @pl.when(pl.program_id(2) == 0)
    def _():
        st[...] = jnp.zeros_like(st)

    row = lax.broadcasted_iota(jnp.int32, (L, LANES), 0)
    lane = lax.broadcasted_iota(jnp.int32, (L, LANES), 1)
    lane_h = lane & (hd - 1)
    strict = lane_h < row
    incl = lane_h <= row
    eye = jnp.where(lane_h == row, 1.0, 0.0).astype(F32)
    head_a = lane < hd
    row2 = lax.broadcasted_iota(jnp.int32, (2 * L, LANES), 0)
    lane2 = lax.broadcasted_iota(jnp.int32, (2 * L, LANES), 1)
    same_head = (row2 < hd) == (lane2 < hd)
    ones2 = _pair_ones()
    pcols = [slice(p * LANES, (p + 1) * LANES) for p in range(pairs)]

    def split(x):
        x = x.astype(F32)
        return jnp.concatenate([jnp.where(head_a, x, 0.0), jnp.where(head_a, 0.0, x)],
                               axis=0).astype(BF16)

    def build(g):
        jobs = [(c, p, slice(c * L, (c + 1) * L), pcols[p])
                for c in range(g * BUILD_CHUNKS, (g + 1) * BUILD_CHUNKS) for p in range(pairs)]
        J = range(len(jobs))
        ld = lambda ref: [ref[rows, cols] for (_, _, rows, cols) in jobs]
        rw, kkw, vb, bend = ld(rw_ref), ld(kkw_ref), ld(vb_ref), ld(bend_ref)
        zkb = [jnp.concatenate([split(tk), split(tb)], axis=0)
               for tk, tb in zip(ld(kt_ref), ld(bt_ref))]
        lhs = [jnp.concatenate([kkw[j], rw[j]], axis=0) for j in J]
        p_kb = [_mm_nt(lhs[j], zkb[j]) for j in J]
        p_k = [t[:, 0:LANES] for t in p_kb]
        p_b = [t[:, LANES:] for t in p_kb]
        yield
        amat = [jnp.where(strict, p_b[j][0:L], 0.0) for j in J]
        x = [eye - amat[j] for j in J]
        m = [_mm(amat[j].astype(BF16), split(amat[j])) for j in J]
        yield
        for _ in range(L.bit_length() - 3):
            xm = [_mm(jnp.concatenate([x[j], m[j]], axis=0).astype(BF16), split(m[j])) for j in J]
            x = [x[j] + xm[j][0:L] for j in J]
            m = [xm[j][L:2 * L] for j in J]
            yield
        x = [(x[j] + _mm(x[j].astype(BF16), split(m[j]))).astype(BF16) for j in J]
        yield
        a_kr = [jnp.concatenate([jnp.where(strict, p_k[j][0:L], 0.0),
                                 jnp.where(incl, p_k[j][L:2 * L], 0.0)], axis=0).astype(BF16)
                for j in J]
        akv = [_mm(a_kr[j], split(vb[j])) for j in J]
        yield
        ku = [_mm(x[j], jnp.concatenate([split(kkw[j]), split(akv[j][0:L])], axis=1)) for j in J]
        yield
        a_rb = [jnp.where(incl, p_b[j][L:2 * L], 0.0).astype(BF16) for j in J]
        ru = [_mm(a_rb[j], jnp.concatenate([split(ku[j][:, 0:LANES]), split(ku[j][:, LANES:])],
                                           axis=1)) for j in J]
        for j, (_, _, rows, cols) in enumerate(jobs):
            rt_s[rows, cols] = (rw[j].astype(F32) - ru[j][:, 0:LANES]).astype(BF16)
            y0_s[rows, cols] = akv[j][L:2 * L] - ru[j][:, LANES:]
        yield
        vu_t = [jnp.concatenate([vb[j].astype(F32), -ku[j][:, LANES:]], axis=0).T.astype(BF16)
                for j in J]
        kt_t = [ku[j][:, 0:LANES].T.astype(BF16) for j in J]
        kb_end = [jnp.concatenate([t, bend[j]], axis=0) for j, t in enumerate(ld(kend_ref))]
        nmat = [_mm(vu_t[j], kb_end[j]) for j in J]
        yield
        qmat = [_mm(kt_t[j], bend[j]) for j in J]
        for j, (c, p, _, _) in enumerate(jobs):
            n_s[c, p] = jnp.where(same_head, nmat[j], 0.0)
            q_s[c, p] = jnp.where(same_head, qmat[j], 0.0).astype(BF16)
        yield

    def scan_step(c):
        rows = slice(c * L, (c + 1) * L)
        dec = dec_ref[c]
        for p in range(pairs):
            s = st[p]
            sb = s.astype(BF16)
            y0_s[rows, pcols[p]] = y0_s[rows, pcols[p]] + _mm_nt(rt_s[rows, pcols[p]], sb)
            st[p] = s * dec[0:1, pcols[p]] + n_s[c, p] - _mm(sb, q_s[c, p])

    def finish(rows):
        y = y0_s[rows, :]
        yc = y - _headsum(y, ones2) * (1.0 / hd)
        var = _headsum(yc * yc, ones2) * (1.0 / hd)
        yn = yc * lax.rsqrt(var + GN_EPS) * lg_ref[...] + lb_ref[...]
        o_ref[rows, :] = ((yn + bon_ref[rows, :]) * g_ref[rows, :]).astype(o_ref.dtype)

    groups = nchunks // BUILD_CHUNKS
    half = BUILD_CHUNKS * L // 2
    for blk in range(groups + 2):
        scans, fins = [], []
        if 1 <= blk <= groups:
            scans = [functools.partial(scan_step, c)
                     for c in range((blk - 1) * BUILD_CHUNKS, blk * BUILD_CHUNKS)]
        if blk >= 2:
            r0 = (blk - 2) * BUILD_CHUNKS * L
            fins = [functools.partial(finish, slice(r0 + i * half, r0 + (i + 1) * half))
                    for i in range(2)]
        side = [t for pair in itertools.zip_longest(scans, fins) for t in pair if t is not None]
        if blk < groups:
            stride = max(1, BUILD_STAGES // (len(side) + 1))
            n_stages = 0
            for i, _ in enumerate(build(blk)):
                n_stages += 1
                if side and i % stride == stride - 1:
                    side.pop(0)()
            assert n_stages == BUILD_STAGES
        for thunk in side:
            thunk()


def _rwkv(rw, kkw, kt, bt, kend, bend, vb, bon, gate, dec, lnx_g, lnx_b, bsz, seq, tc, pairs):
    t, width = rw.shape
    wb = pairs * LANES
    nt = seq // tc
    nchunks = tc // CHUNK
    tok = pl.BlockSpec((tc, wb), lambda b, p, i: (b * nt + i, p))
    par = pl.BlockSpec((1, wb), lambda b, p, i: (0, p))
    return pl.pallas_call(
        functools.partial(_rwkv_kernel, pairs=pairs),
        grid=(bsz, width // wb, nt),
        in_specs=[tok] * 9 + [pl.BlockSpec((nchunks, 1, wb), lambda b, p, i: (b * nt + i, 0, p)),
                              par, par],
        out_specs=tok,
        out_shape=jax.ShapeDtypeStruct((t, width), BF16),
        scratch_shapes=[pltpu.VMEM((pairs, LANES, LANES), F32),
                        pltpu.VMEM((tc, wb), BF16),
                        pltpu.VMEM((tc, wb), F32),
                        pltpu.VMEM((nchunks, pairs, LANES, LANES), BF16),
                        pltpu.VMEM((nchunks, pairs, LANES, LANES), F32)],
        compiler_params=_cparams("parallel", "parallel", "arbitrary"),
        name="l1_rwkv7",
    )(rw, kkw, kt, bt, kend, bend, vb, bon, gate, dec, lnx_g, lnx_b)


def _outproj1_kernel(y_ref, w_ref, h_ref, g_ref, o_ref):
    h = h_ref[...] + _mm(y_ref[...], w_ref[...])
    o_ref[...] = _rms(h, g_ref[...])


def _outproj1(y, w, h, g, tm):
    t, d = h.shape
    row = lambda i: (i, 0)
    full = lambda i: (0, 0)
    return pl.pallas_call(
        _outproj1_kernel,
        grid=(t // tm,),
        in_specs=[pl.BlockSpec((tm, y.shape[1]), row), pl.BlockSpec(w.shape, full),
                  pl.BlockSpec((tm, d), row), pl.BlockSpec((1, d), full)],
        out_specs=pl.BlockSpec((tm, d), row),
        out_shape=jax.ShapeDtypeStruct((t, d), F32),
        compiler_params=_cparams("parallel"),
        name="l1_outproj_final",
    )(y, w, h, g)


def _tile(n, pref):
    t = min(n, pref)
    assert n % t == 0, (n, pref)
    return t


def _layer0(x2, bsz, seq, norm_g, w_in, conv_w, conv_b, w_a, b_a, w_x, b_x, lam, lb_logits,
            hg_g, w_out, next_norm_g):
    t, d = x2.shape
    width = conv_w.shape[1]
    row = lambda p: p.reshape(1, -1)
    z = _norm_matmul(x2, row(norm_g), w_in.astype(BF16), _tile(t, 1024), _tile(w_in.shape[1], 2048))
    ya = _rglru(z, conv_w, row(conv_b), w_a.astype(BF16), row(b_a), w_x.astype(BF16), row(b_x),
                row(lam), bsz, seq, _tile(seq, 512))
    yb = _hgrn2(z, lb_logits, row(hg_g), bsz, seq, _tile(seq, 1024), 2 * width)
    w_out = w_out.astype(BF16)
    return _outproj0(ya, yb, w_out[:width], w_out[width:], x2, row(next_norm_g), _tile(t, 1024))


def _layer1(h, u, bsz, seq, mu, w_r, w_k, w_v, w_g, w0, w1, w2, a0, a1, a2, k_k, k_a, r_k,
            lnx_g, lnx_b, w_o, final_g):
    t, d = h.shape
    row = lambda p: p.reshape(1, -1)
    bf = lambda w: w.astype(BF16)
    ops = _front(u, mu, bf(w_r), bf(w_k), bf(w_v), bf(w_g), bf(w1), bf(w2), row(w0), bf(a1),
                 bf(a2), row(a0), row(k_k), row(k_a), row(r_k), seq, _tile(seq, 256))
    y = _rwkv(*ops, row(lnx_g), row(lnx_b), bsz, seq, _tile(seq, 1024), pairs=4)
    return _outproj1(y, bf(w_o), h, row(final_g), _tile(t, 1024))


def kernel(x, ab_norm_g, ab_w_in, rg_conv_w, rg_conv_b, rg_w_a, rg_b_a, rg_w_x, rg_b_x, rg_lambda, hg_lb_logits, hg_norm_g, ab_w_out, c_norm_g, c_mu, c_w_r, c_w_k, c_w_v, c_w_g, c_w0, c_w1, c_w2, c_a0, c_a1, c_a2, c_k_k, c_k_a, c_r_k, c_lnx_g, c_lnx_b, c_w_o, final_g):
    bsz, seq, d = x.shape
    assert ab_norm_g.shape[0] == 1 and c_norm_g.shape[0] == 1, "two-layer trunk only"
    x2 = x.reshape(bsz * seq, d)
    h, u = _layer0(x2, bsz, seq, ab_norm_g[0], ab_w_in[0], rg_conv_w[0], rg_conv_b[0], rg_w_a[0],
                   rg_b_a[0], rg_w_x[0], rg_b_x[0], rg_lambda[0], hg_lb_logits, hg_norm_g[0],
                   ab_w_out[0], c_norm_g[0])
    out = _layer1(h, u, bsz, seq, c_mu[0], c_w_r[0], c_w_k[0], c_w_v[0], c_w_g[0], c_w0[0],
                  c_w1[0], c_w2[0], c_a0[0], c_a1[0], c_a2[0], c_k_k[0], c_k_a[0],
                  c_r_k[0].reshape(-1), c_lnx_g[0], c_lnx_b[0], c_w_o[0], final_g)
    return out.reshape(bsz, seq, d)
```

```python
import functools
import itertools

import jax
import jax.numpy as jnp
from jax import lax
from jax.experimental import pallas as pl
from jax.experimental.pallas import tpu as pltpu

F32 = jnp.float32
BF16 = jnp.bfloat16

RMS_EPS = 1e-6
GN_EPS = 64e-5
DECAY_SCALE = 0.6065306597126334
RG_C = 8.0
RG_BLOCKS = 8
RG_CONV = 4
HG_HEAD_DIM = 128
RW_HEAD_DIM = 64
CHUNK = 64
BUILD_CHUNKS = 4
BUILD_STAGES = 12
FRONT_COLS = 256
LANES = 128
SUBLANES = 8
VMEM_LIMIT = 56 * 1024 * 1024


def _cparams(*sem):
    return pltpu.CompilerParams(dimension_semantics=sem, vmem_limit_bytes=VMEM_LIMIT)


def _mm(a, b):
    return jnp.dot(a, b, preferred_element_type=F32)


def _mm_nt(a, b):
    return lax.dot_general(a, b, (((1,), (1,)), ((), ())), preferred_element_type=F32)


def _hi_lo(x):
    hi = x.astype(BF16)
    return hi, (x - hi.astype(F32)).astype(BF16)


def _rms(x, g):
    return x * lax.rsqrt(jnp.mean(x * x, axis=-1, keepdims=True) + RMS_EPS) * g


def _sigmoid(x):
    return 1.0 / (1.0 + jnp.exp(-x))


def _silu(x):
    return x * _sigmoid(x)


def _softplus(x):
    return jnp.maximum(x, 0.0) + jnp.log(1.0 + jnp.exp(-jnp.abs(x)))


def _norm_matmul_kernel(x_ref, g_ref, w_ref, o_ref):
    xn = _rms(x_ref[...], g_ref[...]).astype(BF16)
    o_ref[...] = _mm(xn, w_ref[...])


def _norm_matmul(x, g, w, tm, tn):
    t, d = x.shape
    n = w.shape[1]
    return pl.pallas_call(
        _norm_matmul_kernel,
        grid=(n // tn, t // tm),
        in_specs=[pl.BlockSpec((tm, d), lambda j, i: (i, 0)),
                  pl.BlockSpec((1, d), lambda j, i: (0, 0)),
                  pl.BlockSpec((d, tn), lambda j, i: (0, j))],
        out_specs=pl.BlockSpec((tm, tn), lambda j, i: (i, j)),
        out_shape=jax.ShapeDtypeStruct((t, n), F32),
        compiler_params=_cparams("parallel", "parallel"),
        name="l0_norm_inproj",
    )(x, g, w)


def _rglru_kernel(xa_ref, ga_ref, cw_ref, cb_ref, wa_ref, ba_ref, wx_ref, bx_ref, lam_ref,
                  o_ref, xbuf, a_s, u_s, hcar):
    ts, width = xa_ref.shape
    bd = width // RG_BLOCKS

    @pl.when(pl.program_id(1) == 0)
    def _():
        xbuf[0:SUBLANES, :] = jnp.zeros((SUBLANES, width), F32)
        hcar[...] = jnp.zeros_like(hcar)

    xa = xa_ref[...]
    xbuf[SUBLANES:SUBLANES + ts, :] = xa
    xc = cb_ref[...] + cw_ref[RG_CONV - 1:RG_CONV, :] * xa
    for j in range(1, RG_CONV):
        xc = xc + cw_ref[RG_CONV - 1 - j:RG_CONV - j, :] * xbuf[pl.ds(SUBLANES - j, ts), :]
    xbuf[0:SUBLANES, :] = xa[ts - SUBLANES:ts, :]

    xcb = xc.astype(BF16)
    gr, gi = [], []
    for n in range(RG_BLOCKS):
        blk = xcb[:, n * bd:(n + 1) * bd]
        gr.append(_mm(blk, wa_ref[n]))
        gi.append(_mm(blk, wx_ref[n]))
    gate_r = _sigmoid(jnp.concatenate(gr, axis=-1) + ba_ref[...])
    gate_i = _sigmoid(jnp.concatenate(gi, axis=-1) + bx_ref[...])
    log_a = (-RG_C) * gate_r * _softplus(-lam_ref[...])
    a = jnp.exp(log_a)
    u = jnp.sqrt(-jnp.tanh(log_a) * (1.0 + a * a)) * (gate_i * xc)

    a = a.reshape(ts // SUBLANES, SUBLANES, width)
    u = u.reshape(ts // SUBLANES, SUBLANES, width)
    row = lax.broadcasted_iota(jnp.int32, a.shape, 1)
    d = 1
    while d < SUBLANES:
        keep = row >= d
        u = jnp.where(keep, a * pltpu.roll(u, d, axis=1) + u, u)
        a = jnp.where(keep, a * pltpu.roll(a, d, axis=1), a)
        d *= 2
    a_s[...] = a.reshape(ts, width)
    u_s[...] = u.reshape(ts, width)

    def body(j, h):
        r0 = pl.multiple_of(j * SUBLANES, SUBLANES)
        hb = a_s[pl.ds(r0, SUBLANES), :] * h + u_s[pl.ds(r0, SUBLANES), :]
        u_s[pl.ds(r0, SUBLANES), :] = hb
        return hb[SUBLANES - 1:SUBLANES, :]

    hcar[0:1, :] = lax.fori_loop(0, ts // SUBLANES, body, hcar[0:1, :])
    o_ref[...] = (u_s[...] * _silu(ga_ref[...])).astype(o_ref.dtype)


def _rglru(z, conv_w, conv_b, w_a, b_a, w_x, b_x, lam, bsz, seq, ts):
    width = conv_w.shape[1]
    nt = seq // ts
    row = lambda b, i: (b * nt + i, 0)
    full2 = lambda b, i: (0, 0)
    full3 = lambda b, i: (0, 0, 0)
    return pl.pallas_call(
        _rglru_kernel,
        grid=(bsz, nt),
        in_specs=[pl.BlockSpec((ts, width), row),
                  pl.BlockSpec((ts, width), lambda b, i: (b * nt + i, 1)),
                  pl.BlockSpec(conv_w.shape, full2),
                  pl.BlockSpec((1, width), full2),
                  pl.BlockSpec(w_a.shape, full3),
                  pl.BlockSpec((1, width), full2),
                  pl.BlockSpec(w_x.shape, full3),
                  pl.BlockSpec((1, width), full2),
                  pl.BlockSpec((1, width), full2)],
        out_specs=pl.BlockSpec((ts, width), row),
        out_shape=jax.ShapeDtypeStruct((bsz * seq, width), BF16),
        scratch_shapes=[pltpu.VMEM((ts + SUBLANES, width), F32),
                        pltpu.VMEM((ts, width), F32),
                        pltpu.VMEM((ts, width), F32),
                        pltpu.VMEM((SUBLANES, width), F32)],
        compiler_params=_cparams("parallel", "arbitrary"),
        name="l0_rglru",
    )(z, z, conv_w, conv_b, w_a, b_a, w_x, b_x, lam)


def _hgrn2_kernel(q_ref, f_ref, v_ref, g_ref, lbl_ref, gn_ref, o_ref, st):
    tc = q_ref.shape[0]

    @pl.when(pl.program_id(2) == 0)
    def _():
        st[...] = jnp.zeros_like(st)

    ri = lax.broadcasted_iota(jnp.int32, (CHUNK, CHUNK), 0)
    ci = lax.broadcasted_iota(jnp.int32, (CHUNK, CHUNK), 1)
    causal = ri >= ci
    tril = jnp.where(causal, 1.0, 0.0).astype(BF16)
    tril2 = jnp.concatenate([tril, tril], axis=1)
    C = range(tc // CHUNK)
    sl = [slice(n * CHUNK, (n + 1) * CHUNK) for n in C]

    for h in range(st.shape[0]):
        hc = slice(h * HG_HEAD_DIM, (h + 1) * HG_HEAD_DIM)
        lbl = lbl_ref[:, hc]
        e = jnp.exp(lbl - jnp.max(lbl, axis=0, keepdims=True))
        lb = e[0:1, :] / jnp.sum(e, axis=0, keepdims=True)

        sg = _sigmoid(f_ref[:, hc])
        log_f = jnp.log(lb + (1.0 - lb) * sg)
        k_all = (1.0 - lb) * (1.0 - sg)
        lf_hi, lf_lo = _hi_lo(jnp.concatenate([log_f[s] for s in sl], axis=1))
        cum_w = _mm(tril2, jnp.concatenate([lf_hi, lf_lo], axis=0))
        cum = [cum_w[:, n * HG_HEAD_DIM:(n + 1) * HG_HEAD_DIM] for n in C]
        dec = [jnp.exp(cum[n][CHUNK - 1:CHUNK, :]) for n in C]
        k = [k_all[s] for s in sl]
        v = [v_ref[s, hc] for s in sl]
        q_dec = [(q_ref[sl[n], hc] * jnp.exp(cum[n])).astype(BF16) for n in C]
        k_inv = [k[n] * jnp.exp(-cum[n]) for n in C]
        k_end = [(k_inv[n] * dec[n]).astype(BF16) for n in C]
        k_inv = [t.astype(BF16) for t in k_inv]
        scores = [jnp.where(causal, _mm_nt(q_dec[n], k_inv[n]), 0.0).astype(BF16) for n in C]
        upd = [_mm(v[n].T.astype(BF16), k_end[n]) for n in C]
        o = [_mm(scores[n], v[n].astype(BF16)) for n in C]
        state = st[h]
        states = []
        for n in C:
            states.append(state.astype(BF16))
            state = state * dec[n] + upd[n]
        st[h] = state
        o = jnp.concatenate([o[n] + _mm_nt(q_dec[n], states[n]) for n in C], axis=0)
        o = o * lax.rsqrt(jnp.mean(o * o, axis=-1, keepdims=True) + RMS_EPS) * gn_ref[...]
        o_ref[:, hc] = (o * _silu(g_ref[:, hc])).astype(o_ref.dtype)


def _hgrn2(z, lb_logits, hg_g, bsz, seq, tc, col0, hps):
    width = lb_logits.shape[1]
    wb = hps * HG_HEAD_DIM
    nt = seq // tc

    def col(k):
        return lambda b, h, i: (b * nt + i, (col0 + k * width) // wb + h)

    return pl.pallas_call(
        _hgrn2_kernel,
        grid=(bsz, width // wb, nt),
        in_specs=[pl.BlockSpec((tc, wb), col(0)),
                  pl.BlockSpec((tc, wb), col(1)),
                  pl.BlockSpec((tc, wb), col(2)),
                  pl.BlockSpec((tc, wb), col(3)),
                  pl.BlockSpec((lb_logits.shape[0], wb), lambda b, h, i: (0, h)),
                  pl.BlockSpec((1, HG_HEAD_DIM), lambda b, h, i: (0, 0))],
        out_specs=pl.BlockSpec((tc, wb), lambda b, h, i: (b * nt + i, h)),
        out_shape=jax.ShapeDtypeStruct((bsz * seq, width), BF16),
        scratch_shapes=[pltpu.VMEM((hps, HG_HEAD_DIM, HG_HEAD_DIM), F32)],
        compiler_params=_cparams("parallel", "parallel", "arbitrary"),
        name="l0_hgrn2",
    )(z, z, z, z, lb_logits, hg_g)


def _outproj0_kernel(ya_ref, yb_ref, wa_ref, wb_ref, x_ref, g_ref, h_ref, u_ref):
    h = x_ref[...] + _mm(ya_ref[...], wa_ref[...]) + _mm(yb_ref[...], wb_ref[...])
    h_ref[...] = h
    u_ref[...] = _rms(h, g_ref[...])


def _outproj0(ya, yb, w_a, w_b, x, g, tm):
    t, d = x.shape
    row = lambda i: (i, 0)
    full = lambda i: (0, 0)
    return pl.pallas_call(
        _outproj0_kernel,
        grid=(t // tm,),
        in_specs=[pl.BlockSpec((tm, ya.shape[1]), row),
                  pl.BlockSpec((tm, yb.shape[1]), row),
                  pl.BlockSpec(w_a.shape, full),
                  pl.BlockSpec(w_b.shape, full),
                  pl.BlockSpec((tm, d), row),
                  pl.BlockSpec((1, d), full)],
        out_specs=[pl.BlockSpec((tm, d), row), pl.BlockSpec((tm, d), row)],
        out_shape=[jax.ShapeDtypeStruct((t, d), F32), jax.ShapeDtypeStruct((t, d), F32)],
        compiler_params=_cparams("parallel"),
        name="l0_outproj",
    )(ya, yb, w_a, w_b, x, g)


def _shifted(u, up_ref, first_of_seq):
    prev_last = jnp.where(first_of_seq, 0.0, up_ref[SUBLANES - 1:SUBLANES, :])
    row = lax.broadcasted_iota(jnp.int32, u.shape, 0)
    return jnp.where(row == 0, prev_last, pltpu.roll(u, 1, axis=0))


def _pair_ones():
    r = lax.broadcasted_iota(jnp.int32, (2 * LANES, LANES), 0) & (LANES - 1)
    c = lax.broadcasted_iota(jnp.int32, (2 * LANES, LANES), 1)
    return jnp.where((r < RW_HEAD_DIM) == (c < RW_HEAD_DIM), 1.0, 0.0).astype(BF16)


def _headsum(x, ones2, split=True):
    n, nblk = x.shape[0], x.shape[1] // LANES
    xs = jnp.concatenate([x[:, i * LANES:(i + 1) * LANES] for i in range(nblk)], axis=0)
    if split:
        s = _mm(jnp.concatenate(_hi_lo(xs), axis=1), ones2)
    else:
        s = _mm(xs.astype(BF16), ones2[0:LANES])
    return jnp.concatenate([s[i * n:(i + 1) * n] for i in range(nblk)], axis=1)


def _front_kernel(u_ref, up_ref, mu_ref, wr_ref, wk_ref, wv_ref, wg_ref, w1_ref, w2_ref, w0_ref,
                  a1_ref, a2_ref, a0_ref, kk_ref, ka_ref, rk_ref,
                  rw_o, kkw_o, kt_o, bt_o, kend_o, bend_o, vb_o, bon_o, gate_o, dec_o,
                  *, steps_per_seq):
    tm = u_ref.shape[0]
    width = w0_ref.shape[1]
    L = CHUNK
    nch = tm // L
    u = u_ref[...]
    delta = _shifted(u, up_ref, pl.program_id(0) % steps_per_seq == 0) - u
    x_r, x_w, x_k, x_v, x_a, x_g = ((u + delta * mu_ref[i:i + 1, :]).astype(BF16)
                                    for i in range(6))
    zw = jnp.tanh(_mm(x_w, w1_ref[...])).astype(BF16)
    za = _mm(x_a, a1_ref[...]).astype(BF16)
    ones2 = _pair_ones()
    ri = lax.broadcasted_iota(jnp.int32, (L, L), 0)
    ci = lax.broadcasted_iota(jnp.int32, (L, L), 1)
    tril = jnp.where(ri >= ci, 1.0, 0.0).astype(BF16)
    tril2 = jnp.concatenate([tril, tril], axis=1)

    def project(g, out):
        cols = slice(g * FRONT_COLS, (g + 1) * FRONT_COLS)
        out.append(_mm(x_r, wr_ref[g]))
        yield
        out.append(_mm(x_k, wk_ref[g]))
        yield
        out.append(_mm(x_v, wv_ref[g]))
        out.append(_mm(zw, w2_ref[:, cols]))
        out.append(_mm(za, a2_ref[:, cols]))
        yield
        gate_o[:, cols] = _silu(_mm(x_g, wg_ref[g]))
        yield

    def per_token(g, proj):
        cols = slice(g * FRONT_COLS, (g + 1) * FRONT_COLS)
        r_all, kraw_all, v_all, zw_all, za_all = proj
        for n in range(nch):
            rows = slice(n * L, (n + 1) * L)
            r, kraw, v = r_all[rows], kraw_all[rows], v_all[rows]
            lw = -DECAY_SCALE * _sigmoid(w0_ref[:, cols] + zw_all[rows])
            a = _sigmoid(a0_ref[:, cols] + za_all[rows])
            kkp = kraw * kk_ref[:, cols]
            kk = kkp * lax.rsqrt(jnp.maximum(_headsum(kkp * kkp, ones2, split=False), 1e-24))
            k = kraw * (1.0 + (a - 1.0) * ka_ref[:, cols])
            b = kk * a
            bon_o[rows, cols] = _headsum(r * k * rk_ref[:, cols], ones2, split=False) * v
            cum = _mm(tril2, jnp.concatenate(_hi_lo(lw), axis=0))
            dec = jnp.exp(cum[L - 1:L, :])
            dec_o[n, :, cols] = dec
            w_out = jnp.exp(-cum)
            w_end = w_out * dec
            rw_o[rows, cols] = (r * jnp.exp(cum)).astype(BF16)
            kkw_o[rows, cols] = (kk * jnp.exp(cum - lw)).astype(BF16)
            kt_o[rows, cols] = (k * w_out).astype(BF16)
            bt_o[rows, cols] = (b * w_out).astype(BF16)
            kend_o[rows, cols] = (k * w_end).astype(BF16)
            bend_o[rows, cols] = (b * w_end).astype(BF16)
            vb_o[rows, cols] = v.astype(BF16)
            yield

    groups = width // FRONT_COLS
    proj = []
    for _ in project(0, proj):
        pass
    for g in range(groups):
        nxt = []
        ahead = project(g + 1, nxt) if g + 1 < groups else iter(())
        for _ in itertools.zip_longest(ahead, per_token(g, proj)):
            pass
        proj = nxt


def _front(u, mu, w_r, w_k, w_v, w_g, w1, w2, w0, a1, a2, a0, k_k, k_a, r_k, seq, tm):
    t, d = u.shape
    n = w_r.shape[1]
    rb = tm // SUBLANES
    row = lambda i: (i, 0)
    w_r, w_k, w_v, w_g = (w.reshape(d, n // FRONT_COLS, FRONT_COLS).transpose(1, 0, 2).astype(BF16)
                          for w in (w_r, w_k, w_v, w_g))

    def const(shape):
        return pl.BlockSpec(shape, lambda i: (0,) * len(shape), pipeline_mode=pl.Buffered(1))

    tok = pl.BlockSpec((tm, n), row)
    tok16 = jax.ShapeDtypeStruct((t, n), BF16)
    tok32 = jax.ShapeDtypeStruct((t, n), F32)
    return pl.pallas_call(
        functools.partial(_front_kernel, steps_per_seq=seq // tm),
        grid=(t // tm,),
        in_specs=[pl.BlockSpec((tm, d), row),
                  pl.BlockSpec((SUBLANES, d), lambda i: (jnp.maximum(i * rb - 1, 0), 0)),
                  const(mu.shape),
                  const(w_r.shape), const(w_k.shape), const(w_v.shape), const(w_g.shape),
                  const(w1.shape), const(w2.shape), const((1, n)),
                  const(a1.shape), const(a2.shape), const((1, n)),
                  const((1, n)), const((1, n)), const((1, n))],
        out_specs=[tok] * 9 + [pl.BlockSpec((tm // CHUNK, 1, n), lambda i: (i, 0, 0))],
        out_shape=[tok16] * 7 + [tok32] * 2 + [jax.ShapeDtypeStruct((t // CHUNK, 1, n), F32)],
        compiler_params=_cparams("parallel"),
        name="l1_front",
    )(u, u, mu, w_r, w_k, w_v, w_g, w1, w2, w0, a1, a2, a0, k_k, k_a, r_k)


def _rwkv_kernel(rw_ref, kkw_ref, kt_ref, bt_ref, kend_ref, bend_ref, vb_ref, bon_ref, g_ref,
                 dec_ref, lg_ref, lb_ref, o_ref, st, rt_s, y0_s, q_s, n_s, *, pairs):
    tc = rw_ref.shape[0]
    L = CHUNK
    hd = RW_HEAD_DIM
    nchunks = tc // L

    @pl.when(pl.program_id(2) == 0)
    def _():
        st[...] = jnp.zeros_like(st)

    row = lax.broadcasted_iota(jnp.int32, (L, LANES), 0)
    lane = lax.broadcasted_iota(jnp.int32, (L, LANES), 1)
    lane_h = lane & (hd - 1)
    strict = lane_h < row
    incl = lane_h <= row
    eye = jnp.where(lane_h == row, 1.0, 0.0).astype(F32)
    head_a = lane < hd
    row2 = lax.broadcasted_iota(jnp.int32, (2 * L, LANES), 0)
    lane2 = lax.broadcasted_iota(jnp.int32, (2 * L, LANES), 1)
    same_head = (row2 < hd) == (lane2 < hd)
    ones2 = _pair_ones()
    pcols = [slice(p * LANES, (p + 1) * LANES) for p in range(pairs)]

    def split(x):
        x = x.astype(F32)
        return jnp.concatenate([jnp.where(head_a, x, 0.0), jnp.where(head_a, 0.0, x)],
                               axis=0).astype(BF16)

    def build(g):
        jobs = [(c, p, slice(c * L, (c + 1) * L), pcols[p])
                for c in range(g * BUILD_CHUNKS, (g + 1) * BUILD_CHUNKS) for p in range(pairs)]
        J = range(len(jobs))
        ld = lambda ref: [ref[rows, cols] for (_, _, rows, cols) in jobs]
        rw, kkw, vb, bend = ld(rw_ref), ld(kkw_ref), ld(vb_ref), ld(bend_ref)
        zkb = [jnp.concatenate([split(tk), split(tb)], axis=0)
               for tk, tb in zip(ld(kt_ref), ld(bt_ref))]
        lhs = [jnp.concatenate([kkw[j], rw[j]], axis=0) for j in J]
        p_kb = [_mm_nt(lhs[j], zkb[j]) for j in J]
        p_k = [t[:, 0:LANES] for t in p_kb]
        p_b = [t[:, LANES:] for t in p_kb]
        yield
        amat = [jnp.where(strict, p_b[j][0:L], 0.0) for j in J]
        x = [eye - amat[j] for j in J]
        m = [_mm(amat[j].astype(BF16), split(amat[j])) for j in J]
        yield
        for _ in range(L.bit_length() - 3):
            xm = [_mm(jnp.concatenate([x[j], m[j]], axis=0).astype(BF16), split(m[j])) for j in J]
            x = [x[j] + xm[j][0:L] for j in J]
            m = [xm[j][L:2 * L] for j in J]
            yield
        x = [(x[j] + _mm(x[j].astype(BF16), split(m[j]))).astype(BF16) for j in J]
        yield
        a_kr = [jnp.concatenate([jnp.where(strict, p_k[j][0:L], 0.0),
                                 jnp.where(incl, p_k[j][L:2 * L], 0.0)], axis=0).astype(BF16)
                for j in J]
        akv = [_mm(a_kr[j], split(vb[j])) for j in J]
        yield
        ku = [_mm(x[j], jnp.concatenate([split(kkw[j]), split(akv[j][0:L])], axis=1)) for j in J]
        yield
        a_rb = [jnp.where(incl, p_b[j][L:2 * L], 0.0).astype(BF16) for j in J]
        ru = [_mm(a_rb[j], jnp.concatenate([split(ku[j][:, 0:LANES]), split(ku[j][:, LANES:])],
                                           axis=1)) for j in J]
        for j, (_, _, rows, cols) in enumerate(jobs):
            rt_s[rows, cols] = (rw[j].astype(F32) - ru[j][:, 0:LANES]).astype(BF16)
            y0_s[rows, cols] = akv[j][L:2 * L] - ru[j][:, LANES:]
        yield
        vu_t = [jnp.concatenate([vb[j].astype(F32), -ku[j][:, LANES:]], axis=0).T.astype(BF16)
                for j in J]
        kt_t = [ku[j][:, 0:LANES].T.astype(BF16) for j in J]
        kb_end = [jnp.concatenate([t, bend[j]], axis=0) for j, t in enumerate(ld(kend_ref))]
        nmat = [_mm(vu_t[j], kb_end[j]) for j in J]
        yield
        qmat = [_mm(kt_t[j], bend[j]) for j in J]
        for j, (c, p, _, _) in enumerate(jobs):
            n_s[c, p] = jnp.where(same_head, nmat[j], 0.0)
            q_s[c, p] = jnp.where(same_head, qmat[j], 0.0).astype(BF16)
        yield

    def scan_step(c):
        rows = slice(c * L, (c + 1) * L)
        dec = dec_ref[c]
        for p in range(pairs):
            s = st[p]
            sb = s.astype(BF16)
            y0_s[rows, pcols[p]] = y0_s[rows, pcols[p]] + _mm_nt(rt_s[rows, pcols[p]], sb)
            st[p] = s * dec[0:1, pcols[p]] + n_s[c, p] - _mm(sb, q_s[c, p])

    def finish(rows):
        y = y0_s[rows, :]
        yc = y - _headsum(y, ones2, split=False) * (1.0 / hd)
        var = _headsum(yc * yc, ones2, split=False) * (1.0 / hd)
        yn = yc * lax.rsqrt(var + GN_EPS) * lg_ref[...] + lb_ref[...]
        o_ref[rows, :] = ((yn + bon_ref[rows, :]) * g_ref[rows, :]).astype(o_ref.dtype)

    groups = nchunks // BUILD_CHUNKS
    half = BUILD_CHUNKS * L // 2
    for blk in range(groups + 2):
        scans, fins = [], []
        if 1 <= blk <= groups:
            scans = [functools.partial(scan_step, c)
                     for c in range((blk - 1) * BUILD_CHUNKS, blk * BUILD_CHUNKS)]
        if blk >= 2:
            r0 = (blk - 2) * BUILD_CHUNKS * L
            fins = [functools.partial(finish, slice(r0 + i * half, r0 + (i + 1) * half))
                    for i in range(2)]
        side = [t for pair in itertools.zip_longest(scans, fins) for t in pair if t is not None]
        if blk < groups:
            stride = max(1, BUILD_STAGES // (len(side) + 1))
            n_stages = 0
            for i, _ in enumerate(build(blk)):
                n_stages += 1
                if side and i % stride == stride - 1:
                    side.pop(0)()
            assert n_stages == BUILD_STAGES
        for thunk in side:
            thunk()


def _rwkv(rw, kkw, kt, bt, kend, bend, vb, bon, gate, dec, lnx_g, lnx_b, bsz, seq, tc, pairs):
    t, width = rw.shape
    wb = pairs * LANES
    nt = seq // tc
    nchunks = tc // CHUNK
    tok = pl.BlockSpec((tc, wb), lambda b, p, i: (b * nt + i, p))
    par = pl.BlockSpec((1, wb), lambda b, p, i: (0, p))
    return pl.pallas_call(
        functools.partial(_rwkv_kernel, pairs=pairs),
        grid=(bsz, width // wb, nt),
        in_specs=[tok] * 9 + [pl.BlockSpec((nchunks, 1, wb), lambda b, p, i: (b * nt + i, 0, p)),
                              par, par],
        out_specs=tok,
        out_shape=jax.ShapeDtypeStruct((t, width), BF16),
        scratch_shapes=[pltpu.VMEM((pairs, LANES, LANES), F32),
                        pltpu.VMEM((tc, wb), BF16),
                        pltpu.VMEM((tc, wb), F32),
                        pltpu.VMEM((nchunks, pairs, LANES, LANES), BF16),
                        pltpu.VMEM((nchunks, pairs, LANES, LANES), F32)],
        compiler_params=_cparams("parallel", "parallel", "arbitrary"),
        name="l1_rwkv7",
    )(rw, kkw, kt, bt, kend, bend, vb, bon, gate, dec, lnx_g, lnx_b)


def _outproj1_kernel(y_ref, w_ref, h_ref, g_ref, o_ref):
    h = h_ref[...] + _mm(y_ref[...], w_ref[...])
    o_ref[...] = _rms(h, g_ref[...])


def _outproj1(y, w, h, g, tm):
    t, d = h.shape
    row = lambda i: (i, 0)
    full = lambda i: (0, 0)
    return pl.pallas_call(
        _outproj1_kernel,
        grid=(t // tm,),
        in_specs=[pl.BlockSpec((tm, y.shape[1]), row), pl.BlockSpec(w.shape, full),
                  pl.BlockSpec((tm, d), row), pl.BlockSpec((1, d), full)],
        out_specs=pl.BlockSpec((tm, d), row),
        out_shape=jax.ShapeDtypeStruct((t, d), F32),
        compiler_params=_cparams("parallel"),
        name="l1_outproj_final",
    )(y, w, h, g)


def _tile(n, pref):
    t = min(n, pref)
    assert n % t == 0, (n, pref)
    return t


def _layer0(x2, bsz, seq, norm_g, w_in, conv_w, conv_b, w_a, b_a, w_x, b_x, lam, lb_logits,
            hg_g, w_out, next_norm_g):
    t, d = x2.shape
    width = conv_w.shape[1]
    row = lambda p: p.reshape(1, -1)
    z = _norm_matmul(x2, row(norm_g), w_in.astype(BF16), _tile(t, 1024), _tile(w_in.shape[1], 2048))
    ya = _rglru(z, conv_w, row(conv_b), w_a.astype(BF16), row(b_a), w_x.astype(BF16), row(b_x),
                row(lam), bsz, seq, _tile(seq, 512))
    yb = _hgrn2(z, lb_logits, row(hg_g), bsz, seq, _tile(seq, 1024), 2 * width, hps=2)
    w_out = w_out.astype(BF16)
    return _outproj0(ya, yb, w_out[:width], w_out[width:], x2, row(next_norm_g), _tile(t, 1024))


def _layer1(h, u, bsz, seq, mu, w_r, w_k, w_v, w_g, w0, w1, w2, a0, a1, a2, k_k, k_a, r_k,
            lnx_g, lnx_b, w_o, final_g):
    t, d = h.shape
    row = lambda p: p.reshape(1, -1)
    bf = lambda w: w.astype(BF16)
    ops = _front(u, mu, w_r, w_k, w_v, w_g, bf(w1), bf(w2), row(w0), bf(a1),
                 bf(a2), row(a0), row(k_k), row(k_a), row(r_k), seq, _tile(seq, 256))
    y = _rwkv(*ops, row(lnx_g), row(lnx_b), bsz, seq, _tile(seq, 1024), pairs=4)
    return _outproj1(y, bf(w_o), h, row(final_g), _tile(t, 1024))


def kernel(x, ab_norm_g, ab_w_in, rg_conv_w, rg_conv_b, rg_w_a, rg_b_a, rg_w_x, rg_b_x, rg_lambda, hg_lb_logits, hg_norm_g, ab_w_out, c_norm_g, c_mu, c_w_r, c_w_k, c_w_v, c_w_g, c_w0, c_w1, c_w2, c_a0, c_a1, c_a2, c_k_k, c_k_a, c_r_k, c_lnx_g, c_lnx_b, c_w_o, final_g):
    bsz, seq, d = x.shape
    assert ab_norm_g.shape[0] == 1 and c_norm_g.shape[0] == 1, "two-layer trunk only"
    x2 = x.reshape(bsz * seq, d)
    h, u = _layer0(x2, bsz, seq, ab_norm_g[0], ab_w_in[0], rg_conv_w[0], rg_conv_b[0], rg_w_a[0],
                   rg_b_a[0], rg_w_x[0], rg_b_x[0], rg_lambda[0], hg_lb_logits, hg_norm_g[0],
                   ab_w_out[0], c_norm_g[0])
    out = _layer1(h, u, bsz, seq, c_mu[0], c_w_r[0], c_w_k[0], c_w_v[0], c_w_g[0], c_w0[0],
                  c_w1[0], c_w2[0], c_a0[0], c_a1[0], c_a2[0], c_k_k[0], c_k_a[0],
                  c_r_k[0].reshape(-1), c_lnx_g[0], c_lnx_b[0], c_w_o[0], final_g)
    return out.reshape(bsz, seq, d)
```

```python
import functools
import itertools

import jax
import jax.numpy as jnp
from jax import lax
from jax.experimental import pallas as pl
from jax.experimental.pallas import tpu as pltpu

F32 = jnp.float32
BF16 = jnp.bfloat16

RMS_EPS = 1e-6
GN_EPS = 64e-5
DECAY_SCALE = 0.6065306597126334
RG_C = 8.0
RG_BLOCKS = 8
RG_CONV = 4
HG_HEAD_DIM = 128
RW_HEAD_DIM = 64
CHUNK = 64
BUILD_CHUNKS = 4
BUILD_STAGES = 12
FRONT_COLS = 256
LANES = 128
SUBLANES = 8
VMEM_LIMIT = 56 * 1024 * 1024


def _cparams(*sem):
    return pltpu.CompilerParams(dimension_semantics=sem, vmem_limit_bytes=VMEM_LIMIT)


def _mm(a, b):
    return jnp.dot(a, b, preferred_element_type=F32)


def _mm_nt(a, b):
    return lax.dot_general(a, b, (((1,), (1,)), ((), ())), preferred_element_type=F32)


def _hi_lo(x):
    hi = x.astype(BF16)
    return hi, (x - hi.astype(F32)).astype(BF16)


def _rms(x, g):
    return x * lax.rsqrt(jnp.mean(x * x, axis=-1, keepdims=True) + RMS_EPS) * g


def _sigmoid(x):
    return 1.0 / (1.0 + jnp.exp(-x))


def _silu(x):
    return x * _sigmoid(x)


def _softplus(x):
    return jnp.maximum(x, 0.0) + jnp.log(1.0 + jnp.exp(-jnp.abs(x)))


def _inproj_rglru_kernel(x_ref, g_ref, *refs, nslab):
    w_refs = refs[:nslab]
    (cw_ref, cb_ref, wa_ref, ba_ref, wx_ref, bx_ref, lam_ref,
     ya_ref, zb_ref, xbuf, a_s, u_s, hcar) = refs[nslab:]
    ts, width = ya_ref.shape
    bd = width // RG_BLOCKS
    ngrp = ts // SUBLANES

    @pl.when(pl.program_id(1) == 0)
    def _():
        xbuf[0:SUBLANES, :] = jnp.zeros((SUBLANES, width), F32)
        hcar[...] = jnp.zeros_like(hcar)

    xn = _rms(x_ref[...], g_ref[...]).astype(BF16)

    def project_rest():
        piece = width // 4
        for k in range(2, nslab):
            for q in range(4):
                c0 = (k - 2) * width + q * piece
                zb_ref[:, c0:c0 + piece] = _mm(xn, w_refs[k][:, q * piece:(q + 1) * piece])
                yield

    def rglru():
        xa = _mm(xn, w_refs[0][...])
        ga = _mm(xn, w_refs[1][...])
        yield
        xbuf[SUBLANES:SUBLANES + ts, :] = xa
        xc = cb_ref[...] + cw_ref[RG_CONV - 1:RG_CONV, :] * xa
        for j in range(1, RG_CONV):
            xc = xc + cw_ref[RG_CONV - 1 - j:RG_CONV - j, :] * xbuf[pl.ds(SUBLANES - j, ts), :]
        xbuf[0:SUBLANES, :] = xa[ts - SUBLANES:ts, :]
        xcb = xc.astype(BF16)
        gr, gi = [], []
        for n in range(RG_BLOCKS):
            blk = xcb[:, n * bd:(n + 1) * bd]
            gr.append(_mm(blk, wa_ref[n]))
            gi.append(_mm(blk, wx_ref[n]))
            if n % 2 == 1:
                yield
        gate_r = _sigmoid(jnp.concatenate(gr, axis=-1) + ba_ref[...])
        gate_i = _sigmoid(jnp.concatenate(gi, axis=-1) + bx_ref[...])
        yield
        log_a = (-RG_C) * gate_r * _softplus(-lam_ref[...])
        a = jnp.exp(log_a)
        u = jnp.sqrt(-jnp.tanh(log_a) * (1.0 + a * a)) * (gate_i * xc)
        a = a.reshape(ngrp, SUBLANES, width)
        u = u.reshape(ngrp, SUBLANES, width)
        row = lax.broadcasted_iota(jnp.int32, a.shape, 1)
        d = 1
        while d < SUBLANES:
            keep = row >= d
            u = jnp.where(keep, a * pltpu.roll(u, d, axis=1) + u, u)
            a = jnp.where(keep, a * pltpu.roll(a, d, axis=1), a)
            d *= 2
            yield
        a_s[...] = a.reshape(ts, width)
        u_s[...] = u.reshape(ts, width)
        h = hcar[0:1, :]
        for j in range(ngrp):
            rows = slice(j * SUBLANES, (j + 1) * SUBLANES)
            hb = a_s[rows, :] * h + u_s[rows, :]
            u_s[rows, :] = hb
            h = hb[SUBLANES - 1:SUBLANES, :]
            if j % (ngrp // 8) == ngrp // 8 - 1:
                yield
        hcar[0:1, :] = h
        ya_ref[...] = (u_s[...] * _silu(ga)).astype(ya_ref.dtype)
        yield

    for _ in itertools.zip_longest(rglru(), project_rest()):
        pass


def _inproj_rglru(x, g, w_in, conv_w, conv_b, w_a, b_a, w_x, b_x, lam, bsz, seq, ts):
    t, d = x.shape
    width = conv_w.shape[1]
    nslab = w_in.shape[1] // width
    nt = seq // ts
    row = lambda b, i: (b * nt + i, 0)

    def const(shape):
        return pl.BlockSpec(shape, lambda b, i: (0,) * len(shape), pipeline_mode=pl.Buffered(1))

    def slab(k):
        return pl.BlockSpec((d, width), lambda b, i: (0, k), pipeline_mode=pl.Buffered(1))

    return pl.pallas_call(
        functools.partial(_inproj_rglru_kernel, nslab=nslab),
        grid=(bsz, nt),
        in_specs=[pl.BlockSpec((ts, d), row), const((1, d))] + [slab(k) for k in range(nslab)]
        + [const(conv_w.shape), const((1, width)), const(w_a.shape), const((1, width)),
           const(w_x.shape), const((1, width)), const((1, width))],
        out_specs=[pl.BlockSpec((ts, width), row), pl.BlockSpec((ts, (nslab - 2) * width), row)],
        out_shape=[jax.ShapeDtypeStruct((t, width), BF16),
                   jax.ShapeDtypeStruct((t, (nslab - 2) * width), F32)],
        scratch_shapes=[pltpu.VMEM((ts + SUBLANES, width), F32),
                        pltpu.VMEM((ts, width), F32),
                        pltpu.VMEM((ts, width), F32),
                        pltpu.VMEM((SUBLANES, width), F32)],
        compiler_params=_cparams("parallel", "arbitrary"),
        name="l0_inproj_rglru",
    )(x, g, *([w_in] * nslab), conv_w, conv_b, w_a, b_a, w_x, b_x, lam)


def _hgrn2_kernel(q_ref, f_ref, v_ref, g_ref, lbl_ref, gn_ref, o_ref, st):
    tc = q_ref.shape[0]

    @pl.when(pl.program_id(2) == 0)
    def _():
        st[...] = jnp.zeros_like(st)

    ri = lax.broadcasted_iota(jnp.int32, (CHUNK, CHUNK), 0)
    ci = lax.broadcasted_iota(jnp.int32, (CHUNK, CHUNK), 1)
    causal = ri >= ci
    tril = jnp.where(causal, 1.0, 0.0).astype(BF16)
    tril2 = jnp.concatenate([tril, tril], axis=1)
    C = range(tc // CHUNK)
    sl = [slice(n * CHUNK, (n + 1) * CHUNK) for n in C]

    for h in range(st.shape[0]):
        hc = slice(h * HG_HEAD_DIM, (h + 1) * HG_HEAD_DIM)
        lbl = lbl_ref[:, hc]
        e = jnp.exp(lbl - jnp.max(lbl, axis=0, keepdims=True))
        lb = e[0:1, :] / jnp.sum(e, axis=0, keepdims=True)

        sg = _sigmoid(f_ref[:, hc])
        log_f = jnp.log(lb + (1.0 - lb) * sg)
        k_all = (1.0 - lb) * (1.0 - sg)
        lf_hi, lf_lo = _hi_lo(jnp.concatenate([log_f[s] for s in sl], axis=1))
        cum_w = _mm(tril2, jnp.concatenate([lf_hi, lf_lo], axis=0))
        cum = [cum_w[:, n * HG_HEAD_DIM:(n + 1) * HG_HEAD_DIM] for n in C]
        dec = [jnp.exp(cum[n][CHUNK - 1:CHUNK, :]) for n in C]
        k = [k_all[s] for s in sl]
        v = [v_ref[s, hc] for s in sl]
        q_dec = [(q_ref[sl[n], hc] * jnp.exp(cum[n])).astype(BF16) for n in C]
        k_inv = [k[n] * jnp.exp(-cum[n]) for n in C]
        k_end = [(k_inv[n] * dec[n]).astype(BF16) for n in C]
        k_inv = [t.astype(BF16) for t in k_inv]
        scores = [jnp.where(causal, _mm_nt(q_dec[n], k_inv[n]), 0.0).astype(BF16) for n in C]
        upd = [_mm(v[n].T.astype(BF16), k_end[n]) for n in C]
        o = [_mm(scores[n], v[n].astype(BF16)) for n in C]
        state = st[h]
        states = []
        for n in C:
            states.append(state.astype(BF16))
            state = state * dec[n] + upd[n]
        st[h] = state
        o = jnp.concatenate([o[n] + _mm_nt(q_dec[n], states[n]) for n in C], axis=0)
        o = o * lax.rsqrt(jnp.mean(o * o, axis=-1, keepdims=True) + RMS_EPS) * gn_ref[...]
        o_ref[:, hc] = (o * _silu(g_ref[:, hc])).astype(o_ref.dtype)


def _hgrn2(z, lb_logits, hg_g, bsz, seq, tc, col0, hps):
    width = lb_logits.shape[1]
    wb = hps * HG_HEAD_DIM
    nt = seq // tc

    def col(k):
        return lambda b, h, i: (b * nt + i, (col0 + k * width) // wb + h)

    return pl.pallas_call(
        _hgrn2_kernel,
        grid=(bsz, width // wb, nt),
        in_specs=[pl.BlockSpec((tc, wb), col(0)),
                  pl.BlockSpec((tc, wb), col(1)),
                  pl.BlockSpec((tc, wb), col(2)),
                  pl.BlockSpec((tc, wb), col(3)),
                  pl.BlockSpec((lb_logits.shape[0], wb), lambda b, h, i: (0, h)),
                  pl.BlockSpec((1, HG_HEAD_DIM), lambda b, h, i: (0, 0))],
        out_specs=pl.BlockSpec((tc, wb), lambda b, h, i: (b * nt + i, h)),
        out_shape=jax.ShapeDtypeStruct((bsz * seq, width), BF16),
        scratch_shapes=[pltpu.VMEM((hps, HG_HEAD_DIM, HG_HEAD_DIM), F32)],
        compiler_params=_cparams("parallel", "parallel", "arbitrary"),
        name="l0_hgrn2",
    )(z, z, z, z, lb_logits, hg_g)


def _outproj0_kernel(ya_ref, yb_ref, wa_ref, wb_ref, x_ref, g_ref, h_ref, u_ref):
    h = x_ref[...] + _mm(ya_ref[...], wa_ref[...]) + _mm(yb_ref[...], wb_ref[...])
    h_ref[...] = h
    u_ref[...] = _rms(h, g_ref[...])


def _outproj0(ya, yb, w_a, w_b, x, g, tm):
    t, d = x.shape
    row = lambda i: (i, 0)
    full = lambda i: (0, 0)
    return pl.pallas_call(
        _outproj0_kernel,
        grid=(t // tm,),
        in_specs=[pl.BlockSpec((tm, ya.shape[1]), row),
                  pl.BlockSpec((tm, yb.shape[1]), row),
                  pl.BlockSpec(w_a.shape, full),
                  pl.BlockSpec(w_b.shape, full),
                  pl.BlockSpec((tm, d), row),
                  pl.BlockSpec((1, d), full)],
        out_specs=[pl.BlockSpec((tm, d), row), pl.BlockSpec((tm, d), row)],
        out_shape=[jax.ShapeDtypeStruct((t, d), F32), jax.ShapeDtypeStruct((t, d), F32)],
        compiler_params=_cparams("parallel"),
        name="l0_outproj",
    )(ya, yb, w_a, w_b, x, g)


def _shifted(u, up_ref, first_of_seq):
    prev_last = jnp.where(first_of_seq, 0.0, up_ref[SUBLANES - 1:SUBLANES, :])
    row = lax.broadcasted_iota(jnp.int32, u.shape, 0)
    return jnp.where(row == 0, prev_last, pltpu.roll(u, 1, axis=0))


def _pair_ones():
    r = lax.broadcasted_iota(jnp.int32, (2 * LANES, LANES), 0) & (LANES - 1)
    c = lax.broadcasted_iota(jnp.int32, (2 * LANES, LANES), 1)
    return jnp.where((r < RW_HEAD_DIM) == (c < RW_HEAD_DIM), 1.0, 0.0).astype(BF16)


def _headsum(x, ones2, split=True):
    n, nblk = x.shape[0], x.shape[1] // LANES
    xs = jnp.concatenate([x[:, i * LANES:(i + 1) * LANES] for i in range(nblk)], axis=0)
    if split:
        s = _mm(jnp.concatenate(_hi_lo(xs), axis=1), ones2)
    else:
        s = _mm(xs.astype(BF16), ones2[0:LANES])
    return jnp.concatenate([s[i * n:(i + 1) * n] for i in range(nblk)], axis=1)


def _front_kernel(u_ref, up_ref, mu_ref, *refs, steps_per_seq, groups):
    wr_ref, wk_ref, wv_ref, wg_ref = (refs[i * groups:(i + 1) * groups] for i in range(4))
    (w1_ref, w2_ref, w0_ref, a1_ref, a2_ref, a0_ref, kk_ref, ka_ref, rk_ref,
     rw_o, kkw_o, kt_o, bt_o, kend_o, bend_o, vb_o, bon_o, gate_o, dec_o) = refs[4 * groups:]
    tm = u_ref.shape[0]
    L = CHUNK
    nch = tm // L
    u = u_ref[...]
    delta = _shifted(u, up_ref, pl.program_id(0) % steps_per_seq == 0) - u
    x_r, x_w, x_k, x_v, x_a, x_g = ((u + delta * mu_ref[i:i + 1, :]).astype(BF16)
                                    for i in range(6))
    zw = jnp.tanh(_mm(x_w, w1_ref[...])).astype(BF16)
    za = _mm(x_a, a1_ref[...]).astype(BF16)
    ones2 = _pair_ones()
    ri = lax.broadcasted_iota(jnp.int32, (L, L), 0)
    ci = lax.broadcasted_iota(jnp.int32, (L, L), 1)
    tril = jnp.where(ri >= ci, 1.0, 0.0).astype(BF16)
    tril2 = jnp.concatenate([tril, tril], axis=1)

    def project(g, out):
        cols = slice(g * FRONT_COLS, (g + 1) * FRONT_COLS)
        out.append(_mm(x_r, wr_ref[g][...]))
        yield
        out.append(_mm(x_k, wk_ref[g][...]))
        yield
        out.append(_mm(x_v, wv_ref[g][...]))
        out.append(_mm(zw, w2_ref[:, cols]))
        out.append(_mm(za, a2_ref[:, cols]))
        yield
        gate_o[:, cols] = _silu(_mm(x_g, wg_ref[g][...]))
        yield

    def per_token(g, proj):
        cols = slice(g * FRONT_COLS, (g + 1) * FRONT_COLS)
        r_all, kraw_all, v_all, zw_all, za_all = proj
        for n in range(nch):
            rows = slice(n * L, (n + 1) * L)
            r, kraw, v = r_all[rows], kraw_all[rows], v_all[rows]
            lw = -DECAY_SCALE * _sigmoid(w0_ref[:, cols] + zw_all[rows])
            a = _sigmoid(a0_ref[:, cols] + za_all[rows])
            kkp = kraw * kk_ref[:, cols]
            kk = kkp * lax.rsqrt(jnp.maximum(_headsum(kkp * kkp, ones2, split=False), 1e-24))
            k = kraw * (1.0 + (a - 1.0) * ka_ref[:, cols])
            b = kk * a
            bon_o[rows, cols] = _headsum(r * k * rk_ref[:, cols], ones2, split=False) * v
            cum = _mm(tril2, jnp.concatenate(_hi_lo(lw), axis=0))
            dec = jnp.exp(cum[L - 1:L, :])
            dec_o[n, :, cols] = dec
            w_out = jnp.exp(-cum)
            w_end = w_out * dec
            rw_o[rows, cols] = (r * jnp.exp(cum)).astype(BF16)
            kkw_o[rows, cols] = (kk * jnp.exp(cum - lw)).astype(BF16)
            kt_o[rows, cols] = (k * w_out).astype(BF16)
            bt_o[rows, cols] = (b * w_out).astype(BF16)
            kend_o[rows, cols] = (k * w_end).astype(BF16)
            bend_o[rows, cols] = (b * w_end).astype(BF16)
            vb_o[rows, cols] = v.astype(BF16)
            yield

    proj = []
    for _ in project(0, proj):
        pass
    for g in range(groups):
        nxt = []
        ahead = project(g + 1, nxt) if g + 1 < groups else iter(())
        for _ in itertools.zip_longest(ahead, per_token(g, proj)):
            pass
        proj = nxt


def _front(u, mu, w_r, w_k, w_v, w_g, w1, w2, w0, a1, a2, a0, k_k, k_a, r_k, seq, tm):
    t, d = u.shape
    n = w_r.shape[1]
    groups = n // FRONT_COLS
    rb = tm // SUBLANES
    row = lambda i: (i, 0)

    def const(shape):
        return pl.BlockSpec(shape, lambda i: (0,) * len(shape), pipeline_mode=pl.Buffered(1))

    def colgroup(g):
        return pl.BlockSpec((d, FRONT_COLS), lambda i: (0, g), pipeline_mode=pl.Buffered(1))

    tok = pl.BlockSpec((tm, n), row)
    tok16 = jax.ShapeDtypeStruct((t, n), BF16)
    tok32 = jax.ShapeDtypeStruct((t, n), F32)
    wspecs = [colgroup(g) for g in range(groups)]
    return pl.pallas_call(
        functools.partial(_front_kernel, steps_per_seq=seq // tm, groups=groups),
        grid=(t // tm,),
        in_specs=[pl.BlockSpec((tm, d), row),
                  pl.BlockSpec((SUBLANES, d), lambda i: (jnp.maximum(i * rb - 1, 0), 0)),
                  const(mu.shape)] + wspecs * 4
        + [const(w1.shape), const(w2.shape), const((1, n)),
           const(a1.shape), const(a2.shape), const((1, n)),
           const((1, n)), const((1, n)), const((1, n))],
        out_specs=[tok] * 9 + [pl.BlockSpec((tm // CHUNK, 1, n), lambda i: (i, 0, 0))],
        out_shape=[tok16] * 7 + [tok32] * 2 + [jax.ShapeDtypeStruct((t // CHUNK, 1, n), F32)],
        compiler_params=_cparams("parallel"),
        name="l1_front",
    )(u, u, mu, *([w_r] * groups + [w_k] * groups + [w_v] * groups + [w_g] * groups),
      w1, w2, w0, a1, a2, a0, k_k, k_a, r_k)


def _rwkv_kernel(rw_ref, kkw_ref, kt_ref, bt_ref, kend_ref, bend_ref, vb_ref, bon_ref, g_ref,
                 dec_ref, lg_ref, lb_ref, o_ref, st, rt_s, y0_s, q_s, n_s, *, pairs):
    tc = rw_ref.shape[0]
    L = CHUNK
    hd = RW_HEAD_DIM
    nchunks = tc // L

    @pl.when(pl.program_id(2) == 0)
    def _():
        st[...] = jnp.zeros_like(st)

    row = lax.broadcasted_iota(jnp.int32, (L, LANES), 0)
    lane = lax.broadcasted_iota(jnp.int32, (L, LANES), 1)
    lane_h = lane & (hd - 1)
    strict = lane_h < row
    incl = lane_h <= row
    eye = jnp.where(lane_h == row, 1.0, 0.0).astype(F32)
    head_a = lane < hd
    row2 = lax.broadcasted_iota(jnp.int32, (2 * L, LANES), 0)
    lane2 = lax.broadcasted_iota(jnp.int32, (2 * L, LANES), 1)
    same_head = (row2 < hd) == (lane2 < hd)
    ones2 = _pair_ones()
    pcols = [slice(p * LANES, (p + 1) * LANES) for p in range(pairs)]

    def split(x):
        x = x.astype(F32)
        return jnp.concatenate([jnp.where(head_a, x, 0.0), jnp.where(head_a, 0.0, x)],
                               axis=0).astype(BF16)

    def build(g):
        jobs = [(c, p, slice(c * L, (c + 1) * L), pcols[p])
                for c in range(g * BUILD_CHUNKS, (g + 1) * BUILD_CHUNKS) for p in range(pairs)]
        J = range(len(jobs))
        ld = lambda ref: [ref[rows, cols] for (_, _, rows, cols) in jobs]
        rw, kkw, vb, bend = ld(rw_ref), ld(kkw_ref), ld(vb_ref), ld(bend_ref)
        zkb = [jnp.concatenate([split(tk), split(tb)], axis=0)
               for tk, tb in zip(ld(kt_ref), ld(bt_ref))]
        lhs = [jnp.concatenate([kkw[j], rw[j]], axis=0) for j in J]
        p_kb = [_mm_nt(lhs[j], zkb[j]) for j in J]
        p_k = [t[:, 0:LANES] for t in p_kb]
        p_b = [t[:, LANES:] for t in p_kb]
        yield
        amat = [jnp.where(strict, p_b[j][0:L], 0.0) for j in J]
        x = [eye - amat[j] for j in J]
        m = [_mm(amat[j].astype(BF16), split(amat[j])) for j in J]
        yield
        for _ in range(L.bit_length() - 3):
            xm = [_mm(jnp.concatenate([x[j], m[j]], axis=0).astype(BF16), split(m[j])) for j in J]
            x = [x[j] + xm[j][0:L] for j in J]
            m = [xm[j][L:2 * L] for j in J]
            yield
        x = [(x[j] + _mm(x[j].astype(BF16), split(m[j]))).astype(BF16) for j in J]
        yield
        a_kr = [jnp.concatenate([jnp.where(strict, p_k[j][0:L], 0.0),
                                 jnp.where(incl, p_k[j][L:2 * L], 0.0)], axis=0).astype(BF16)
                for j in J]
        akv = [_mm(a_kr[j], split(vb[j])) for j in J]
        yield
        ku = [_mm(x[j], jnp.concatenate([split(kkw[j]), split(akv[j][0:L])], axis=1)) for j in J]
        yield
        a_rb = [jnp.where(incl, p_b[j][L:2 * L], 0.0).astype(BF16) for j in J]
        ru = [_mm(a_rb[j], jnp.concatenate([split(ku[j][:, 0:LANES]), split(ku[j][:, LANES:])],
                                           axis=1)) for j in J]
        for j, (_, _, rows, cols) in enumerate(jobs):
            rt_s[rows, cols] = (rw[j].astype(F32) - ru[j][:, 0:LANES]).astype(BF16)
            y0_s[rows, cols] = akv[j][L:2 * L] - ru[j][:, LANES:]
        yield
        vu_t = [jnp.concatenate([vb[j].astype(F32), -ku[j][:, LANES:]], axis=0).T.astype(BF16)
                for j in J]
        kt_t = [ku[j][:, 0:LANES].T.astype(BF16) for j in J]
        kb_end = [jnp.concatenate([t, bend[j]], axis=0) for j, t in enumerate(ld(kend_ref))]
        nmat = [_mm(vu_t[j], kb_end[j]) for j in J]
        yield
        qmat = [_mm(kt_t[j], bend[j]) for j in J]
        for j, (c, p, _, _) in enumerate(jobs):
            n_s[c, p] = jnp.where(same_head, nmat[j], 0.0)
            q_s[c, p] = jnp.where(same_head, qmat[j], 0.0).astype(BF16)
        yield

    def scan_step(c):
        rows = slice(c * L, (c + 1) * L)
        dec = dec_ref[c]
        for p in range(pairs):
            s = st[p]
            sb = s.astype(BF16)
            y0_s[rows, pcols[p]] = y0_s[rows, pcols[p]] + _mm_nt(rt_s[rows, pcols[p]], sb)
            st[p] = s * dec[0:1, pcols[p]] + n_s[c, p] - _mm(sb, q_s[c, p])

    def finish(rows):
        y = y0_s[rows, :]
        yc = y - _headsum(y, ones2, split=False) * (1.0 / hd)
        var = _headsum(yc * yc, ones2, split=False) * (1.0 / hd)
        yn = yc * lax.rsqrt(var + GN_EPS) * lg_ref[...] + lb_ref[...]
        o_ref[rows, :] = ((yn + bon_ref[rows, :]) * g_ref[rows, :]).astype(o_ref.dtype)

    groups = nchunks // BUILD_CHUNKS
    half = BUILD_CHUNKS * L // 2
    for blk in range(groups + 2):
        scans, fins = [], []
        if 1 <= blk <= groups:
            scans = [functools.partial(scan_step, c)
                     for c in range((blk - 1) * BUILD_CHUNKS, blk * BUILD_CHUNKS)]
        if blk >= 2:
            r0 = (blk - 2) * BUILD_CHUNKS * L
            fins = [functools.partial(finish, slice(r0 + i * half, r0 + (i + 1) * half))
                    for i in range(2)]
        side = [t for pair in itertools.zip_longest(scans, fins) for t in pair if t is not None]
        if blk < groups:
            stride = max(1, BUILD_STAGES // (len(side) + 1))
            n_stages = 0
            for i, _ in enumerate(build(blk)):
                n_stages += 1
                if side and i % stride == stride - 1:
                    side.pop(0)()
            assert n_stages == BUILD_STAGES
        for thunk in side:
            thunk()


def _rwkv(rw, kkw, kt, bt, kend, bend, vb, bon, gate, dec, lnx_g, lnx_b, bsz, seq, tc, pairs):
    t, width = rw.shape
    wb = pairs * LANES
    nt = seq // tc
    nchunks = tc // CHUNK
    tok = pl.BlockSpec((tc, wb), lambda b, p, i: (b * nt + i, p))
    par = pl.BlockSpec((1, wb), lambda b, p, i: (0, p))
    return pl.pallas_call(
        functools.partial(_rwkv_kernel, pairs=pairs),
        grid=(bsz, width // wb, nt),
        in_specs=[tok] * 9 + [pl.BlockSpec((nchunks, 1, wb), lambda b, p, i: (b * nt + i, 0, p)),
                              par, par],
        out_specs=tok,
        out_shape=jax.ShapeDtypeStruct((t, width), BF16),
        scratch_shapes=[pltpu.VMEM((pairs, LANES, LANES), F32),
                        pltpu.VMEM((tc, wb), BF16),
                        pltpu.VMEM((tc, wb), F32),
                        pltpu.VMEM((nchunks, pairs, LANES, LANES), BF16),
                        pltpu.VMEM((nchunks, pairs, LANES, LANES), F32)],
        compiler_params=_cparams("parallel", "parallel", "arbitrary"),
        name="l1_rwkv7",
    )(rw, kkw, kt, bt, kend, bend, vb, bon, gate, dec, lnx_g, lnx_b)


def _outproj1_kernel(y_ref, w_ref, h_ref, g_ref, o_ref):
    h = h_ref[...] + _mm(y_ref[...], w_ref[...])
    o_ref[...] = _rms(h, g_ref[...])


def _outproj1(y, w, h, g, tm):
    t, d = h.shape
    row = lambda i: (i, 0)
    full = lambda i: (0, 0)
    return pl.pallas_call(
        _outproj1_kernel,
        grid=(t // tm,),
        in_specs=[pl.BlockSpec((tm, y.shape[1]), row), pl.BlockSpec(w.shape, full),
                  pl.BlockSpec((tm, d), row), pl.BlockSpec((1, d), full)],
        out_specs=pl.BlockSpec((tm, d), row),
        out_shape=jax.ShapeDtypeStruct((t, d), F32),
        compiler_params=_cparams("parallel"),
        name="l1_outproj_final",
    )(y, w, h, g)


def _tile(n, pref):
    t = min(n, pref)
    assert n % t == 0, (n, pref)
    return t


def _layer0(x2, bsz, seq, norm_g, w_in, conv_w, conv_b, w_a, b_a, w_x, b_x, lam, lb_logits,
            hg_g, w_out, next_norm_g):
    t, d = x2.shape
    width = conv_w.shape[1]
    row = lambda p: p.reshape(1, -1)
    ya, zb = _inproj_rglru(x2, row(norm_g), w_in.astype(BF16), conv_w, row(conv_b),
                           w_a.astype(BF16), row(b_a), w_x.astype(BF16), row(b_x), row(lam),
                           bsz, seq, _tile(seq, 512))
    yb = _hgrn2(zb, lb_logits, row(hg_g), bsz, seq, _tile(seq, 1024), 0, hps=2)
    w_out = w_out.astype(BF16)
    return _outproj0(ya, yb, w_out[:width], w_out[width:], x2, row(next_norm_g), _tile(t, 1024))


def _layer1(h, u, bsz, seq, mu, w_r, w_k, w_v, w_g, w0, w1, w2, a0, a1, a2, k_k, k_a, r_k,
            lnx_g, lnx_b, w_o, final_g):
    t, d = h.shape
    row = lambda p: p.reshape(1, -1)
    bf = lambda w: w.astype(BF16)
    ops = _front(u, mu, bf(w_r), bf(w_k), bf(w_v), bf(w_g), bf(w1), bf(w2), row(w0), bf(a1),
                 bf(a2), row(a0), row(k_k), row(k_a), row(r_k), seq, _tile(seq, 256))
    y = _rwkv(*ops, row(lnx_g), row(lnx_b), bsz, seq, _tile(seq, 1024), pairs=4)
    return _outproj1(y, bf(w_o), h, row(final_g), _tile(t, 1024))


def kernel(x, ab_norm_g, ab_w_in, rg_conv_w, rg_conv_b, rg_w_a, rg_b_a, rg_w_x, rg_b_x, rg_lambda, hg_lb_logits, hg_norm_g, ab_w_out, c_norm_g, c_mu, c_w_r, c_w_k, c_w_v, c_w_g, c_w0, c_w1, c_w2, c_a0, c_a1, c_a2, c_k_k, c_k_a, c_r_k, c_lnx_g, c_lnx_b, c_w_o, final_g):
    bsz, seq, d = x.shape
    assert ab_norm_g.shape[0] == 1 and c_norm_g.shape[0] == 1, "two-layer trunk only"
    x2 = x.reshape(bsz * seq, d)
    h, u = _layer0(x2, bsz, seq, ab_norm_g[0], ab_w_in[0], rg_conv_w[0], rg_conv_b[0], rg_w_a[0],
                   rg_b_a[0], rg_w_x[0], rg_b_x[0], rg_lambda[0], hg_lb_logits, hg_norm_g[0],
                   ab_w_out[0], c_norm_g[0])
    out = _layer1(h, u, bsz, seq, c_mu[0], c_w_r[0], c_w_k[0], c_w_v[0], c_w_g[0], c_w0[0],
                  c_w1[0], c_w2[0], c_a0[0], c_a1[0], c_a2[0], c_k_k[0], c_k_a[0],
                  c_r_k[0].reshape(-1), c_lnx_g[0], c_lnx_b[0], c_w_o[0], final_g)
    return out.reshape(bsz, seq, d)
```

```python
import functools
import itertools

import jax
import jax.numpy as jnp
from jax import lax
from jax.experimental import pallas as pl
from jax.experimental.pallas import tpu as pltpu

F32 = jnp.float32
BF16 = jnp.bfloat16

RMS_EPS = 1e-6
GN_EPS = 64e-5
DECAY_SCALE = 0.6065306597126334
RG_C = 8.0
RG_BLOCKS = 8
RG_CONV = 4
RG_ROWS = 32
HG_HEAD_DIM = 128
RW_HEAD_DIM = 64
CHUNK = 64
BUILD_CHUNKS = 4
BUILD_STAGES = 12
FRONT_COLS = 256
LANES = 128
SUBLANES = 8
VMEM_LIMIT = 56 * 1024 * 1024


def _cparams(*sem):
    return pltpu.CompilerParams(dimension_semantics=sem, vmem_limit_bytes=VMEM_LIMIT)


def _mm(a, b):
    return jnp.dot(a, b, preferred_element_type=F32)


def _mm_nt(a, b):
    return lax.dot_general(a, b, (((1,), (1,)), ((), ())), preferred_element_type=F32)


def _hi_lo(x):
    hi = x.astype(BF16)
    return hi, (x - hi.astype(F32)).astype(BF16)


def _rms(x, g):
    return x * lax.rsqrt(jnp.mean(x * x, axis=-1, keepdims=True) + RMS_EPS) * g


def _sigmoid(x):
    return 1.0 / (1.0 + jnp.exp(-x))


def _silu(x):
    return x * _sigmoid(x)


def _softplus(x):
    return jnp.maximum(x, 0.0) + jnp.log(1.0 + jnp.exp(-jnp.abs(x)))


def _inproj_rglru_kernel(x_ref, g_ref, *refs, nslab):
    w_refs = refs[:nslab]
    (cw_ref, cb_ref, wa_ref, ba_ref, wx_ref, bx_ref, lam_ref,
     ya_ref, zb_ref, xbuf, hcar) = refs[nslab:]
    ts, width = ya_ref.shape
    bd = width // RG_BLOCKS

    @pl.when(pl.program_id(1) == 0)
    def _():
        xbuf[0:SUBLANES, :] = jnp.zeros((SUBLANES, width), F32)
        hcar[...] = jnp.zeros_like(hcar)

    xn = _rms(x_ref[...], g_ref[...]).astype(BF16)

    def project_rest():
        piece = width // 4
        for k in range(2, nslab):
            for q in range(4):
                c0 = (k - 2) * width + q * piece
                zb_ref[:, c0:c0 + piece] = _mm(xn, w_refs[k][:, q * piece:(q + 1) * piece])
                yield

    def rglru():
        xa = _mm(xn, w_refs[0][...])
        ga = _mm(xn, w_refs[1][...])
        xbuf[SUBLANES:SUBLANES + ts, :] = xa
        yield
        sp = _softplus(-lam_ref[...])
        h = hcar[0:1, :]
        ngrp = RG_ROWS // SUBLANES
        for c in range(ts // RG_ROWS):
            r0 = c * RG_ROWS
            rows = slice(r0, r0 + RG_ROWS)
            xc = cb_ref[...] + cw_ref[RG_CONV - 1:RG_CONV, :] * xa[rows]
            for j in range(1, RG_CONV):
                xc = xc + (cw_ref[RG_CONV - 1 - j:RG_CONV - j, :]
                           * xbuf[pl.ds(SUBLANES - j + r0, RG_ROWS), :])
            xcb = xc.astype(BF16)
            gr, gi = [], []
            for n in range(RG_BLOCKS):
                blk = xcb[:, n * bd:(n + 1) * bd]
                gr.append(_mm(blk, wa_ref[n]))
                gi.append(_mm(blk, wx_ref[n]))
            gate_r = _sigmoid(jnp.concatenate(gr, axis=-1) + ba_ref[...])
            gate_i = _sigmoid(jnp.concatenate(gi, axis=-1) + bx_ref[...])
            log_a = (-RG_C) * gate_r * sp
            a = jnp.exp(log_a)
            u = jnp.sqrt(-jnp.tanh(log_a) * (1.0 + a * a)) * (gate_i * xc)
            a = a.reshape(ngrp, SUBLANES, width)
            u = u.reshape(ngrp, SUBLANES, width)
            row = lax.broadcasted_iota(jnp.int32, a.shape, 1)
            d = 1
            while d < SUBLANES:
                keep = row >= d
                u = jnp.where(keep, a * pltpu.roll(u, d, axis=1) + u, u)
                a = jnp.where(keep, a * pltpu.roll(a, d, axis=1), a)
                d *= 2
            hs = []
            for j in range(ngrp):
                hb = a[j] * h + u[j]
                hs.append(hb)
                h = hb[SUBLANES - 1:SUBLANES, :]
            ya_ref[rows, :] = (jnp.concatenate(hs, axis=0) * _silu(ga[rows])).astype(ya_ref.dtype)
            yield
        xbuf[0:SUBLANES, :] = xa[ts - SUBLANES:ts, :]
        hcar[0:1, :] = h

    head, rest = rglru(), project_rest()
    next(head)
    nsub = ts // RG_ROWS
    npiece = 4 * (nslab - 2)
    for c in range(nsub):
        for _ in range((c + 1) * npiece // nsub - c * npiece // nsub):
            next(rest)
        next(head)
    for _ in itertools.chain(head, rest):
        pass


def _inproj_rglru(x, g, w_in, conv_w, conv_b, w_a, b_a, w_x, b_x, lam, bsz, seq, ts):
    t, d = x.shape
    width = conv_w.shape[1]
    nslab = w_in.shape[1] // width
    nt = seq // ts
    row = lambda b, i: (b * nt + i, 0)

    def const(shape):
        return pl.BlockSpec(shape, lambda b, i: (0,) * len(shape), pipeline_mode=pl.Buffered(1))

    def slab(k):
        return pl.BlockSpec((d, width), lambda b, i: (0, k), pipeline_mode=pl.Buffered(1))

    return pl.pallas_call(
        functools.partial(_inproj_rglru_kernel, nslab=nslab),
        grid=(bsz, nt),
        in_specs=[pl.BlockSpec((ts, d), row), const((1, d))] + [slab(k) for k in range(nslab)]
        + [const(conv_w.shape), const((1, width)), const(w_a.shape), const((1, width)),
           const(w_x.shape), const((1, width)), const((1, width))],
        out_specs=[pl.BlockSpec((ts, width), row), pl.BlockSpec((ts, (nslab - 2) * width), row)],
        out_shape=[jax.ShapeDtypeStruct((t, width), BF16),
                   jax.ShapeDtypeStruct((t, (nslab - 2) * width), F32)],
        scratch_shapes=[pltpu.VMEM((ts + SUBLANES, width), F32),
                        pltpu.VMEM((SUBLANES, width), F32)],
        compiler_params=_cparams("parallel", "arbitrary"),
        name="l0_inproj_rglru",
    )(x, g, *([w_in] * nslab), conv_w, conv_b, w_a, b_a, w_x, b_x, lam)


def _hgrn2_kernel(q_ref, f_ref, v_ref, g_ref, lbl_ref, gn_ref, o_ref, st):
    tc = q_ref.shape[0]

    @pl.when(pl.program_id(2) == 0)
    def _():
        st[...] = jnp.zeros_like(st)

    ri = lax.broadcasted_iota(jnp.int32, (CHUNK, CHUNK), 0)
    ci = lax.broadcasted_iota(jnp.int32, (CHUNK, CHUNK), 1)
    causal = ri >= ci
    tril = jnp.where(causal, 1.0, 0.0).astype(BF16)
    tril2 = jnp.concatenate([tril, tril], axis=1)
    C = range(tc // CHUNK)
    sl = [slice(n * CHUNK, (n + 1) * CHUNK) for n in C]

    for h in range(st.shape[0]):
        hc = slice(h * HG_HEAD_DIM, (h + 1) * HG_HEAD_DIM)
        lbl = lbl_ref[:, hc]
        e = jnp.exp(lbl - jnp.max(lbl, axis=0, keepdims=True))
        lb = e[0:1, :] / jnp.sum(e, axis=0, keepdims=True)

        sg = _sigmoid(f_ref[:, hc])
        log_f = jnp.log(lb + (1.0 - lb) * sg)
        k_all = (1.0 - lb) * (1.0 - sg)
        lf_hi, lf_lo = _hi_lo(jnp.concatenate([log_f[s] for s in sl], axis=1))
        cum_w = _mm(tril2, jnp.concatenate([lf_hi, lf_lo], axis=0))
        cum = [cum_w[:, n * HG_HEAD_DIM:(n + 1) * HG_HEAD_DIM] for n in C]
        dec = [jnp.exp(cum[n][CHUNK - 1:CHUNK, :]) for n in C]
        k = [k_all[s] for s in sl]
        v = [v_ref[s, hc] for s in sl]
        q_dec = [(q_ref[sl[n], hc] * jnp.exp(cum[n])).astype(BF16) for n in C]
        k_inv = [k[n] * jnp.exp(-cum[n]) for n in C]
        k_end = [(k_inv[n] * dec[n]).astype(BF16) for n in C]
        k_inv = [t.astype(BF16) for t in k_inv]
        scores = [jnp.where(causal, _mm_nt(q_dec[n], k_inv[n]), 0.0).astype(BF16) for n in C]
        upd = [_mm(v[n].T.astype(BF16), k_end[n]) for n in C]
        o = [_mm(scores[n], v[n].astype(BF16)) for n in C]
        state = st[h]
        states = []
        for n in C:
            states.append(state.astype(BF16))
            state = state * dec[n] + upd[n]
        st[h] = state
        o = jnp.concatenate([o[n] + _mm_nt(q_dec[n], states[n]) for n in C], axis=0)
        o = o * lax.rsqrt(jnp.mean(o * o, axis=-1, keepdims=True) + RMS_EPS) * gn_ref[...]
        o_ref[:, hc] = (o * _silu(g_ref[:, hc])).astype(o_ref.dtype)


def _hgrn2(z, lb_logits, hg_g, bsz, seq, tc, col0, hps):
    width = lb_logits.shape[1]
    wb = hps * HG_HEAD_DIM
    nt = seq // tc

    def col(k):
        return lambda b, h, i: (b * nt + i, (col0 + k * width) // wb + h)

    return pl.pallas_call(
        _hgrn2_kernel,
        grid=(bsz, width // wb, nt),
        in_specs=[pl.BlockSpec((tc, wb), col(0)),
                  pl.BlockSpec((tc, wb), col(1)),
                  pl.BlockSpec((tc, wb), col(2)),
                  pl.BlockSpec((tc, wb), col(3)),
                  pl.BlockSpec((lb_logits.shape[0], wb), lambda b, h, i: (0, h)),
                  pl.BlockSpec((1, HG_HEAD_DIM), lambda b, h, i: (0, 0))],
        out_specs=pl.BlockSpec((tc, wb), lambda b, h, i: (b * nt + i, h)),
        out_shape=jax.ShapeDtypeStruct((bsz * seq, width), BF16),
        scratch_shapes=[pltpu.VMEM((hps, HG_HEAD_DIM, HG_HEAD_DIM), F32)],
        compiler_params=_cparams("parallel", "parallel", "arbitrary"),
        name="l0_hgrn2",
    )(z, z, z, z, lb_logits, hg_g)


def _outproj0_kernel(ya_ref, yb_ref, wa_ref, wb_ref, x_ref, g_ref, h_ref, u_ref):
    h = x_ref[...] + _mm(ya_ref[...], wa_ref[...]) + _mm(yb_ref[...], wb_ref[...])
    h_ref[...] = h
    u_ref[...] = _rms(h, g_ref[...])


def _outproj0(ya, yb, w_a, w_b, x, g, tm):
    t, d = x.shape
    row = lambda i: (i, 0)
    full = lambda i: (0, 0)
    return pl.pallas_call(
        _outproj0_kernel,
        grid=(t // tm,),
        in_specs=[pl.BlockSpec((tm, ya.shape[1]), row),
                  pl.BlockSpec((tm, yb.shape[1]), row),
                  pl.BlockSpec(w_a.shape, full),
                  pl.BlockSpec(w_b.shape, full),
                  pl.BlockSpec((tm, d), row),
                  pl.BlockSpec((1, d), full)],
        out_specs=[pl.BlockSpec((tm, d), row), pl.BlockSpec((tm, d), row)],
        out_shape=[jax.ShapeDtypeStruct((t, d), F32), jax.ShapeDtypeStruct((t, d), F32)],
        compiler_params=_cparams("parallel"),
        name="l0_outproj",
    )(ya, yb, w_a, w_b, x, g)


def _shifted(u, up_ref, first_of_seq):
    prev_last = jnp.where(first_of_seq, 0.0, up_ref[SUBLANES - 1:SUBLANES, :])
    row = lax.broadcasted_iota(jnp.int32, u.shape, 0)
    return jnp.where(row == 0, prev_last, pltpu.roll(u, 1, axis=0))


def _pair_ones():
    r = lax.broadcasted_iota(jnp.int32, (2 * LANES, LANES), 0) & (LANES - 1)
    c = lax.broadcasted_iota(jnp.int32, (2 * LANES, LANES), 1)
    return jnp.where((r < RW_HEAD_DIM) == (c < RW_HEAD_DIM), 1.0, 0.0).astype(BF16)


def _headsum(x, ones2, split=True):
    n, nblk = x.shape[0], x.shape[1] // LANES
    xs = jnp.concatenate([x[:, i * LANES:(i + 1) * LANES] for i in range(nblk)], axis=0)
    if split:
        s = _mm(jnp.concatenate(_hi_lo(xs), axis=1), ones2)
    else:
        s = _mm(xs.astype(BF16), ones2[0:LANES])
    return jnp.concatenate([s[i * n:(i + 1) * n] for i in range(nblk)], axis=1)


def _front_kernel(u_ref, up_ref, mu_ref, *refs, steps_per_seq, groups):
    wr_ref, wk_ref, wv_ref, wg_ref = (refs[i * groups:(i + 1) * groups] for i in range(4))
    (w1_ref, w2_ref, w0_ref, a1_ref, a2_ref, a0_ref, kk_ref, ka_ref, rk_ref,
     rw_o, kkw_o, kt_o, bt_o, kend_o, bend_o, vb_o, bon_o, gate_o, dec_o) = refs[4 * groups:]
    tm = u_ref.shape[0]
    L = CHUNK
    nch = tm // L
    u = u_ref[...]
    delta = _shifted(u, up_ref, pl.program_id(0) % steps_per_seq == 0) - u
    x_r, x_w, x_k, x_v, x_a, x_g = ((u + delta * mu_ref[i:i + 1, :]).astype(BF16)
                                    for i in range(6))
    zw = jnp.tanh(_mm(x_w, w1_ref[...])).astype(BF16)
    za = _mm(x_a, a1_ref[...]).astype(BF16)
    ones2 = _pair_ones()
    ri = lax.broadcasted_iota(jnp.int32, (L, L), 0)
    ci = lax.broadcasted_iota(jnp.int32, (L, L), 1)
    tril = jnp.where(ri >= ci, 1.0, 0.0).astype(BF16)
    tril2 = jnp.concatenate([tril, tril], axis=1)

    def project(g, out):
        cols = slice(g * FRONT_COLS, (g + 1) * FRONT_COLS)
        out.append(_mm(x_r, wr_ref[g][...]))
        yield
        out.append(_mm(x_k, wk_ref[g][...]))
        yield
        out.append(_mm(x_v, wv_ref[g][...]))
        out.append(_mm(zw, w2_ref[:, cols]))
        out.append(_mm(za, a2_ref[:, cols]))
        yield
        gate_o[:, cols] = _silu(_mm(x_g, wg_ref[g][...]))
        yield

    def per_token(g, proj):
        cols = slice(g * FRONT_COLS, (g + 1) * FRONT_COLS)
        r_all, kraw_all, v_all, zw_all, za_all = proj
        for n in range(nch):
            rows = slice(n * L, (n + 1) * L)
            r, kraw, v = r_all[rows], kraw_all[rows], v_all[rows]
            lw = -DECAY_SCALE * _sigmoid(w0_ref[:, cols] + zw_all[rows])
            a = _sigmoid(a0_ref[:, cols] + za_all[rows])
            kkp = kraw * kk_ref[:, cols]
            kk = kkp * lax.rsqrt(jnp.maximum(_headsum(kkp * kkp, ones2, split=False), 1e-24))
            k = kraw * (1.0 + (a - 1.0) * ka_ref[:, cols])
            b = kk * a
            bon_o[rows, cols] = _headsum(r * k * rk_ref[:, cols], ones2, split=False) * v
            cum = _mm(tril2, jnp.concatenate(_hi_lo(lw), axis=0))
            dec = jnp.exp(cum[L - 1:L, :])
            dec_o[n, :, cols] = dec
            w_out = jnp.exp(-cum)
            w_end = w_out * dec
            rw_o[rows, cols] = (r * jnp.exp(cum)).astype(BF16)
            kkw_o[rows, cols] = (kk * jnp.exp(cum - lw)).astype(BF16)
            kt_o[rows, cols] = (k * w_out).astype(BF16)
            bt_o[rows, cols] = (b * w_out).astype(BF16)
            kend_o[rows, cols] = (k * w_end).astype(BF16)
            bend_o[rows, cols] = (b * w_end).astype(BF16)
            vb_o[rows, cols] = v.astype(BF16)
            yield

    proj = []
    for _ in project(0, proj):
        pass
    for g in range(groups):
        nxt = []
        ahead = project(g + 1, nxt) if g + 1 < groups else iter(())
        for _ in itertools.zip_longest(ahead, per_token(g, proj)):
            pass
        proj = nxt


def _front(u, mu, w_r, w_k, w_v, w_g, w1, w2, w0, a1, a2, a0, k_k, k_a, r_k, seq, tm):
    t, d = u.shape
    n = w_r.shape[1]
    groups = n // FRONT_COLS
    rb = tm // SUBLANES
    row = lambda i: (i, 0)

    def const(shape):
        return pl.BlockSpec(shape, lambda i: (0,) * len(shape), pipeline_mode=pl.Buffered(1))

    def colgroup(g):
        return pl.BlockSpec((d, FRONT_COLS), lambda i: (0, g), pipeline_mode=pl.Buffered(1))

    tok = pl.BlockSpec((tm, n), row)
    tok16 = jax.ShapeDtypeStruct((t, n), BF16)
    tok32 = jax.ShapeDtypeStruct((t, n), F32)
    wspecs = [colgroup(g) for g in range(groups)]
    return pl.pallas_call(
        functools.partial(_front_kernel, steps_per_seq=seq // tm, groups=groups),
        grid=(t // tm,),
        in_specs=[pl.BlockSpec((tm, d), row),
                  pl.BlockSpec((SUBLANES, d), lambda i: (jnp.maximum(i * rb - 1, 0), 0)),
                  const(mu.shape)] + wspecs * 4
        + [const(w1.shape), const(w2.shape), const((1, n)),
           const(a1.shape), const(a2.shape), const((1, n)),
           const((1, n)), const((1, n)), const((1, n))],
        out_specs=[tok] * 9 + [pl.BlockSpec((tm // CHUNK, 1, n), lambda i: (i, 0, 0))],
        out_shape=[tok16] * 7 + [tok32] * 2 + [jax.ShapeDtypeStruct((t // CHUNK, 1, n), F32)],
        compiler_params=_cparams("parallel"),
        name="l1_front",
    )(u, u, mu, *([w_r] * groups + [w_k] * groups + [w_v] * groups + [w_g] * groups),
      w1, w2, w0, a1, a2, a0, k_k, k_a, r_k)


def _rwkv_kernel(rw_ref, kkw_ref, kt_ref, bt_ref, kend_ref, bend_ref, vb_ref, bon_ref, g_ref,
                 dec_ref, lg_ref, lb_ref, o_ref, st, rt_s, y0_s, q_s, n_s, *, pairs):
    tc = rw_ref.shape[0]
    L = CHUNK
    hd = RW_HEAD_DIM
    nchunks = tc // L

    @pl.when(pl.program_id(2) == 0)
    def _():
        st[...] = jnp.zeros_like(st)

    row = lax.broadcasted_iota(jnp.int32, (L, LANES), 0)
    lane = lax.broadcasted_iota(jnp.int32, (L, LANES), 1)
    lane_h = lane & (hd - 1)
    strict = lane_h < row
    incl = lane_h <= row
    eye = jnp.where(lane_h == row, 1.0, 0.0).astype(F32)
    head_a = lane < hd
    row2 = lax.broadcasted_iota(jnp.int32, (2 * L, LANES), 0)
    lane2 = lax.broadcasted_iota(jnp.int32, (2 * L, LANES), 1)
    same_head = (row2 < hd) == (lane2 < hd)
    ones2 = _pair_ones()
    pcols = [slice(p * LANES, (p + 1) * LANES) for p in range(pairs)]

    def split(x):
        x = x.astype(F32)
        return jnp.concatenate([jnp.where(head_a, x, 0.0), jnp.where(head_a, 0.0, x)],
                               axis=0).astype(BF16)

    def build(g):
        jobs = [(c, p, slice(c * L, (c + 1) * L), pcols[p])
                for c in range(g * BUILD_CHUNKS, (g + 1) * BUILD_CHUNKS) for p in range(pairs)]
        J = range(len(jobs))
        ld = lambda ref: [ref[rows, cols] for (_, _, rows, cols) in jobs]
        rw, kkw, vb, bend = ld(rw_ref), ld(kkw_ref), ld(vb_ref), ld(bend_ref)
        zkb = [jnp.concatenate([split(tk), split(tb)], axis=0)
               for tk, tb in zip(ld(kt_ref), ld(bt_ref))]
        lhs = [jnp.concatenate([kkw[j], rw[j]], axis=0) for j in J]
        p_kb = [_mm_nt(lhs[j], zkb[j]) for j in J]
        p_k = [t[:, 0:LANES] for t in p_kb]
        p_b = [t[:, LANES:] for t in p_kb]
        yield
        amat = [jnp.where(strict, p_b[j][0:L], 0.0) for j in J]
        x = [eye - amat[j] for j in J]
        m = [_mm(amat[j].astype(BF16), split(amat[j])) for j in J]
        yield
        for _ in range(L.bit_length() - 3):
            xm = [_mm(jnp.concatenate([x[j], m[j]], axis=0).astype(BF16), split(m[j])) for j in J]
            x = [x[j] + xm[j][0:L] for j in J]
            m = [xm[j][L:2 * L] for j in J]
            yield
        x = [(x[j] + _mm(x[j].astype(BF16), split(m[j]))).astype(BF16) for j in J]
        yield
        a_kr = [jnp.concatenate([jnp.where(strict, p_k[j][0:L], 0.0),
                                 jnp.where(incl, p_k[j][L:2 * L], 0.0)], axis=0).astype(BF16)
                for j in J]
        akv = [_mm(a_kr[j], split(vb[j])) for j in J]
        yield
        ku = [_mm(x[j], jnp.concatenate([split(kkw[j]), split(akv[j][0:L])], axis=1)) for j in J]
        yield
        a_rb = [jnp.where(incl, p_b[j][L:2 * L], 0.0).astype(BF16) for j in J]
        ru = [_mm(a_rb[j], jnp.concatenate([split(ku[j][:, 0:LANES]), split(ku[j][:, LANES:])],
                                           axis=1)) for j in J]
        for j, (_, _, rows, cols) in enumerate(jobs):
            rt_s[rows, cols] = (rw[j].astype(F32) - ru[j][:, 0:LANES]).astype(BF16)
            y0_s[rows, cols] = akv[j][L:2 * L] - ru[j][:, LANES:]
        yield
        vu_t = [jnp.concatenate([vb[j].astype(F32), -ku[j][:, LANES:]], axis=0).T.astype(BF16)
                for j in J]
        kt_t = [ku[j][:, 0:LANES].T.astype(BF16) for j in J]
        kb_end = [jnp.concatenate([t, bend[j]], axis=0) for j, t in enumerate(ld(kend_ref))]
        nmat = [_mm(vu_t[j], kb_end[j]) for j in J]
        yield
        qmat = [_mm(kt_t[j], bend[j]) for j in J]
        for j, (c, p, _, _) in enumerate(jobs):
            n_s[c, p] = jnp.where(same_head, nmat[j], 0.0)
            q_s[c, p] = jnp.where(same_head, qmat[j], 0.0).astype(BF16)
        yield

    def scan_step(c):
        rows = slice(c * L, (c + 1) * L)
        dec = dec_ref[c]
        for p in range(pairs):
            s = st[p]
            sb = s.astype(BF16)
            y0_s[rows, pcols[p]] = y0_s[rows, pcols[p]] + _mm_nt(rt_s[rows, pcols[p]], sb)
            st[p] = s * dec[0:1, pcols[p]] + n_s[c, p] - _mm(sb, q_s[c, p])

    def finish(rows):
        y = y0_s[rows, :]
        yc = y - _headsum(y, ones2, split=False) * (1.0 / hd)
        var = _headsum(yc * yc, ones2, split=False) * (1.0 / hd)
        yn = yc * lax.rsqrt(var + GN_EPS) * lg_ref[...] + lb_ref[...]
        o_ref[rows, :] = ((yn + bon_ref[rows, :]) * g_ref[rows, :]).astype(o_ref.dtype)

    groups = nchunks // BUILD_CHUNKS
    half = BUILD_CHUNKS * L // 2
    for blk in range(groups + 2):
        scans, fins = [], []
        if 1 <= blk <= groups:
            scans = [functools.partial(scan_step, c)
                     for c in range((blk - 1) * BUILD_CHUNKS, blk * BUILD_CHUNKS)]
        if blk >= 2:
            r0 = (blk - 2) * BUILD_CHUNKS * L
            fins = [functools.partial(finish, slice(r0 + i * half, r0 + (i + 1) * half))
                    for i in range(2)]
        side = [t for pair in itertools.zip_longest(scans, fins) for t in pair if t is not None]
        if blk < groups:
            stride = max(1, BUILD_STAGES // (len(side) + 1))
            n_stages = 0
            for i, _ in enumerate(build(blk)):
                n_stages += 1
                if side and i % stride == stride - 1:
                    side.pop(0)()
            assert n_stages == BUILD_STAGES
        for thunk in side:
            thunk()


def _rwkv(rw, kkw, kt, bt, kend, bend, vb, bon, gate, dec, lnx_g, lnx_b, bsz, seq, tc, pairs):
    t, width = rw.shape
    wb = pairs * LANES
    nt = seq // tc
    nchunks = tc // CHUNK
    tok = pl.BlockSpec((tc, wb), lambda b, p, i: (b * nt + i, p))
    par = pl.BlockSpec((1, wb), lambda b, p, i: (0, p))
    return pl.pallas_call(
        functools.partial(_rwkv_kernel, pairs=pairs),
        grid=(bsz, width // wb, nt),
        in_specs=[tok] * 9 + [pl.BlockSpec((nchunks, 1, wb), lambda b, p, i: (b * nt + i, 0, p)),
                              par, par],
        out_specs=tok,
        out_shape=jax.ShapeDtypeStruct((t, width), BF16),
        scratch_shapes=[pltpu.VMEM((pairs, LANES, LANES), F32),
                        pltpu.VMEM((tc, wb), BF16),
                        pltpu.VMEM((tc, wb), F32),
                        pltpu.VMEM((nchunks, pairs, LANES, LANES), BF16),
                        pltpu.VMEM((nchunks, pairs, LANES, LANES), F32)],
        compiler_params=_cparams("parallel", "parallel", "arbitrary"),
        name="l1_rwkv7",
    )(rw, kkw, kt, bt, kend, bend, vb, bon, gate, dec, lnx_g, lnx_b)


def _outproj1_kernel(y_ref, w_ref, h_ref, g_ref, o_ref):
    h = h_ref[...] + _mm(y_ref[...], w_ref[...])
    o_ref[...] = _rms(h, g_ref[...])


def _outproj1(y, w, h, g, tm):
    t, d = h.shape
    row = lambda i: (i, 0)
    full = lambda i: (0, 0)
    return pl.pallas_call(
        _outproj1_kernel,
        grid=(t // tm,),
        in_specs=[pl.BlockSpec((tm, y.shape[1]), row), pl.BlockSpec(w.shape, full),
                  pl.BlockSpec((tm, d), row), pl.BlockSpec((1, d), full)],
        out_specs=pl.BlockSpec((tm, d), row),
        out_shape=jax.ShapeDtypeStruct((t, d), F32),
        compiler_params=_cparams("parallel"),
        name="l1_outproj_final",
    )(y, w, h, g)


def _tile(n, pref):
    t = min(n, pref)
    assert n % t == 0, (n, pref)
    return t


def _layer0(x2, bsz, seq, norm_g, w_in, conv_w, conv_b, w_a, b_a, w_x, b_x, lam, lb_logits,
            hg_g, w_out, next_norm_g):
    t, d = x2.shape
    width = conv_w.shape[1]
    row = lambda p: p.reshape(1, -1)
    ya, zb = _inproj_rglru(x2, row(norm_g), w_in.astype(BF16), conv_w, row(conv_b),
                           w_a.astype(BF16), row(b_a), w_x.astype(BF16), row(b_x), row(lam),
                           bsz, seq, _tile(seq, 512))
    yb = _hgrn2(zb, lb_logits, row(hg_g), bsz, seq, _tile(seq, 1024), 0, hps=2)
    w_out = w_out.astype(BF16)
    return _outproj0(ya, yb, w_out[:width], w_out[width:], x2, row(next_norm_g), _tile(t, 1024))


def _layer1(h, u, bsz, seq, mu, w_r, w_k, w_v, w_g, w0, w1, w2, a0, a1, a2, k_k, k_a, r_k,
            lnx_g, lnx_b, w_o, final_g):
    t, d = h.shape
    row = lambda p: p.reshape(1, -1)
    bf = lambda w: w.astype(BF16)
    ops = _front(u, mu, bf(w_r), bf(w_k), bf(w_v), bf(w_g), bf(w1), bf(w2), row(w0), bf(a1),
                 bf(a2), row(a0), row(k_k), row(k_a), row(r_k), seq, _tile(seq, 256))
    y = _rwkv(*ops, row(lnx_g), row(lnx_b), bsz, seq, _tile(seq, 1024), pairs=4)
    return _outproj1(y, bf(w_o), h, row(final_g), _tile(t, 1024))


def kernel(x, ab_norm_g, ab_w_in, rg_conv_w, rg_conv_b, rg_w_a, rg_b_a, rg_w_x, rg_b_x, rg_lambda, hg_lb_logits, hg_norm_g, ab_w_out, c_norm_g, c_mu, c_w_r, c_w_k, c_w_v, c_w_g, c_w0, c_w1, c_w2, c_a0, c_a1, c_a2, c_k_k, c_k_a, c_r_k, c_lnx_g, c_lnx_b, c_w_o, final_g):
    bsz, seq, d = x.shape
    assert ab_norm_g.shape[0] == 1 and c_norm_g.shape[0] == 1, "two-layer trunk only"
    x2 = x.reshape(bsz * seq, d)
    h, u = _layer0(x2, bsz, seq, ab_norm_g[0], ab_w_in[0], rg_conv_w[0], rg_conv_b[0], rg_w_a[0],
                   rg_b_a[0], rg_w_x[0], rg_b_x[0], rg_lambda[0], hg_lb_logits, hg_norm_g[0],
                   ab_w_out[0], c_norm_g[0])
    out = _layer1(h, u, bsz, seq, c_mu[0], c_w_r[0], c_w_k[0], c_w_v[0], c_w_g[0], c_w0[0],
                  c_w1[0], c_w2[0], c_a0[0], c_a1[0], c_a2[0], c_k_k[0], c_k_a[0],
                  c_r_k[0].reshape(-1), c_lnx_g[0], c_lnx_b[0], c_w_o[0], final_g)
    return out.reshape(bsz, seq, d)
```

```python
import functools
import itertools
from typing import NamedTuple

import jax
import jax.numpy as jnp
from jax import lax
from jax.experimental import pallas as pl
from jax.experimental.pallas import tpu as pltpu

F32 = jnp.float32
BF16 = jnp.bfloat16

RMS_EPS = 1e-6
GN_EPS = 64e-5
DECAY_SCALE = 0.6065306597126334
RG_C = 8.0
RG_BLOCKS = 8
RG_CONV = 4
RG_ROWS = 32
HG_HEAD_DIM = 128
HG_HEADS_PER_STEP = 2
RW_HEAD_DIM = 64
RW_PAIRS_PER_STEP = 4
CHUNK = 64
BUILD_CHUNKS = 4
BUILD_STAGES = 12
FRONT_COLS = 256
LANES = 128
SUBLANES = 8
VMEM_LIMIT = 56 * 1024 * 1024


def _cparams(*sem):
    return pltpu.CompilerParams(dimension_semantics=sem, vmem_limit_bytes=VMEM_LIMIT)


def _mm(a, b):
    return jnp.dot(a, b, preferred_element_type=F32)


def _mm_nt(a, b):
    return lax.dot_general(a, b, (((1,), (1,)), ((), ())), preferred_element_type=F32)


def _hi_lo(x):
    hi = x.astype(BF16)
    return hi, (x - hi.astype(F32)).astype(BF16)


def _rms(x, g):
    return x * lax.rsqrt(jnp.mean(x * x, axis=-1, keepdims=True) + RMS_EPS) * g


def _sigmoid(x):
    return 1.0 / (1.0 + jnp.exp(-x))


def _silu(x):
    return x * _sigmoid(x)


def _softplus(x):
    return jnp.maximum(x, 0.0) + jnp.log(1.0 + jnp.exp(-jnp.abs(x)))


def _inproj_rglru_kernel(x_ref, g_ref, *refs, nslab):
    w_refs = refs[:nslab]
    (cw_ref, cb_ref, wa_ref, ba_ref, wx_ref, bx_ref, lam_ref,
     ya_ref, zb_ref, xbuf, hcar) = refs[nslab:]
    ts, width = ya_ref.shape
    bd = width // RG_BLOCKS

    @pl.when(pl.program_id(1) == 0)
    def _():
        xbuf[0:SUBLANES, :] = jnp.zeros((SUBLANES, width), F32)
        hcar[...] = jnp.zeros_like(hcar)

    xn = _rms(x_ref[...], g_ref[...]).astype(BF16)

    def project_rest():
        piece = width // 4
        for k in range(2, nslab):
            for q in range(4):
                c0 = (k - 2) * width + q * piece
                zb_ref[:, c0:c0 + piece] = _mm(xn, w_refs[k][:, q * piece:(q + 1) * piece])
                yield

    def rglru():
        xa = _mm(xn, w_refs[0][...])
        ga = _mm(xn, w_refs[1][...])
        xbuf[SUBLANES:SUBLANES + ts, :] = xa
        yield
        sp = _softplus(-lam_ref[...])
        h = hcar[0:1, :]
        ngrp = RG_ROWS // SUBLANES
        for c in range(ts // RG_ROWS):
            r0 = c * RG_ROWS
            rows = slice(r0, r0 + RG_ROWS)
            xc = cb_ref[...] + cw_ref[RG_CONV - 1:RG_CONV, :] * xa[rows]
            for j in range(1, RG_CONV):
                xc = xc + (cw_ref[RG_CONV - 1 - j:RG_CONV - j, :]
                           * xbuf[pl.ds(SUBLANES - j + r0, RG_ROWS), :])
            xcb = xc.astype(BF16)
            gr, gi = [], []
            for n in range(RG_BLOCKS):
                blk = xcb[:, n * bd:(n + 1) * bd]
                gr.append(_mm(blk, wa_ref[n]))
                gi.append(_mm(blk, wx_ref[n]))
            gate_r = _sigmoid(jnp.concatenate(gr, axis=-1) + ba_ref[...])
            gate_i = _sigmoid(jnp.concatenate(gi, axis=-1) + bx_ref[...])
            log_a = (-RG_C) * gate_r * sp
            a = jnp.exp(log_a)
            u = jnp.sqrt(-jnp.tanh(log_a) * (1.0 + a * a)) * (gate_i * xc)
            a = a.reshape(ngrp, SUBLANES, width)
            u = u.reshape(ngrp, SUBLANES, width)
            row = lax.broadcasted_iota(jnp.int32, a.shape, 1)
            d = 1
            while d < SUBLANES:
                keep = row >= d
                u = jnp.where(keep, a * pltpu.roll(u, d, axis=1) + u, u)
                a = jnp.where(keep, a * pltpu.roll(a, d, axis=1), a)
                d *= 2
            hs = []
            for j in range(ngrp):
                hb = a[j] * h + u[j]
                hs.append(hb)
                h = hb[SUBLANES - 1:SUBLANES, :]
            ya_ref[rows, :] = (jnp.concatenate(hs, axis=0) * _silu(ga[rows])).astype(ya_ref.dtype)
            yield
        xbuf[0:SUBLANES, :] = xa[ts - SUBLANES:ts, :]
        hcar[0:1, :] = h

    head, rest = rglru(), project_rest()
    next(head)
    nsub = ts // RG_ROWS
    npiece = 4 * (nslab - 2)
    for c in range(nsub):
        for _ in range((c + 1) * npiece // nsub - c * npiece // nsub):
            next(rest)
        next(head)
    for _ in itertools.chain(head, rest):
        pass


def _inproj_rglru(x, g, w_in, conv_w, conv_b, w_a, b_a, w_x, b_x, lam, bsz, seq, ts):
    t, d = x.shape
    width = conv_w.shape[1]
    nslab = w_in.shape[1] // width
    nt = seq // ts
    row = lambda b, i: (b * nt + i, 0)

    def const(shape):
        return pl.BlockSpec(shape, lambda b, i: (0,) * len(shape), pipeline_mode=pl.Buffered(1))

    def slab(k):
        return pl.BlockSpec((d, width), lambda b, i: (0, k), pipeline_mode=pl.Buffered(1))

    return pl.pallas_call(
        functools.partial(_inproj_rglru_kernel, nslab=nslab),
        grid=(bsz, nt),
        in_specs=[pl.BlockSpec((ts, d), row), const((1, d))] + [slab(k) for k in range(nslab)]
        + [const(conv_w.shape), const((1, width)), const(w_a.shape), const((1, width)),
           const(w_x.shape), const((1, width)), const((1, width))],
        out_specs=[pl.BlockSpec((ts, width), row), pl.BlockSpec((ts, (nslab - 2) * width), row)],
        out_shape=[jax.ShapeDtypeStruct((t, width), BF16),
                   jax.ShapeDtypeStruct((t, (nslab - 2) * width), F32)],
        scratch_shapes=[pltpu.VMEM((ts + SUBLANES, width), F32),
                        pltpu.VMEM((SUBLANES, width), F32)],
        compiler_params=_cparams("parallel", "arbitrary"),
        name="l0_inproj_rglru",
    )(x, g, *([w_in] * nslab), conv_w, conv_b, w_a, b_a, w_x, b_x, lam)


def _hgrn2_kernel(q_ref, f_ref, v_ref, g_ref, lbl_ref, gn_ref, o_ref, st):
    tc = q_ref.shape[0]

    @pl.when(pl.program_id(2) == 0)
    def _():
        st[...] = jnp.zeros_like(st)

    ri = lax.broadcasted_iota(jnp.int32, (CHUNK, CHUNK), 0)
    ci = lax.broadcasted_iota(jnp.int32, (CHUNK, CHUNK), 1)
    causal = ri >= ci
    tril = jnp.where(causal, 1.0, 0.0).astype(BF16)
    tril2 = jnp.concatenate([tril, tril], axis=1)
    C = range(tc // CHUNK)
    sl = [slice(n * CHUNK, (n + 1) * CHUNK) for n in C]

    for h in range(st.shape[0]):
        hc = slice(h * HG_HEAD_DIM, (h + 1) * HG_HEAD_DIM)
        lbl = lbl_ref[:, hc]
        e = jnp.exp(lbl - jnp.max(lbl, axis=0, keepdims=True))
        lb = e[0:1, :] / jnp.sum(e, axis=0, keepdims=True)

        sg = _sigmoid(f_ref[:, hc])
        log_f = jnp.log(lb + (1.0 - lb) * sg)
        k_all = (1.0 - lb) * (1.0 - sg)
        lf_hi, lf_lo = _hi_lo(jnp.concatenate([log_f[s] for s in sl], axis=1))
        cum_w = _mm(tril2, jnp.concatenate([lf_hi, lf_lo], axis=0))
        cum = [cum_w[:, n * HG_HEAD_DIM:(n + 1) * HG_HEAD_DIM] for n in C]
        dec = [jnp.exp(cum[n][CHUNK - 1:CHUNK, :]) for n in C]
        k = [k_all[s] for s in sl]
        v = [v_ref[s, hc] for s in sl]
        q_dec = [(q_ref[sl[n], hc] * jnp.exp(cum[n])).astype(BF16) for n in C]
        k_inv = [k[n] * jnp.exp(-cum[n]) for n in C]
        k_end = [(k_inv[n] * dec[n]).astype(BF16) for n in C]
        k_inv = [t.astype(BF16) for t in k_inv]
        scores = [jnp.where(causal, _mm_nt(q_dec[n], k_inv[n]), 0.0).astype(BF16) for n in C]
        upd = [_mm(v[n].T.astype(BF16), k_end[n]) for n in C]
        o = [_mm(scores[n], v[n].astype(BF16)) for n in C]
        state = st[h]
        states = []
        for n in C:
            states.append(state.astype(BF16))
            state = state * dec[n] + upd[n]
        st[h] = state
        o = jnp.concatenate([o[n] + _mm_nt(q_dec[n], states[n]) for n in C], axis=0)
        o = o * lax.rsqrt(jnp.mean(o * o, axis=-1, keepdims=True) + RMS_EPS) * gn_ref[...]
        o_ref[:, hc] = (o * _silu(g_ref[:, hc])).astype(o_ref.dtype)


def _hgrn2(z, lb_logits, hg_g, bsz, seq, tc, col0, hps):
    width = lb_logits.shape[1]
    wb = hps * HG_HEAD_DIM
    nt = seq // tc

    def col(k):
        return lambda b, h, i: (b * nt + i, (col0 + k * width) // wb + h)

    return pl.pallas_call(
        _hgrn2_kernel,
        grid=(bsz, width // wb, nt),
        in_specs=[pl.BlockSpec((tc, wb), col(0)),
                  pl.BlockSpec((tc, wb), col(1)),
                  pl.BlockSpec((tc, wb), col(2)),
                  pl.BlockSpec((tc, wb), col(3)),
                  pl.BlockSpec((lb_logits.shape[0], wb), lambda b, h, i: (0, h)),
                  pl.BlockSpec((1, HG_HEAD_DIM), lambda b, h, i: (0, 0))],
        out_specs=pl.BlockSpec((tc, wb), lambda b, h, i: (b * nt + i, h)),
        out_shape=jax.ShapeDtypeStruct((bsz * seq, width), BF16),
        scratch_shapes=[pltpu.VMEM((hps, HG_HEAD_DIM, HG_HEAD_DIM), F32)],
        compiler_params=_cparams("parallel", "parallel", "arbitrary"),
        name="l0_hgrn2",
    )(z, z, z, z, lb_logits, hg_g)


def _outproj0_kernel(ya_ref, yb_ref, wa_ref, wb_ref, x_ref, g_ref, h_ref, u_ref):
    h = x_ref[...] + _mm(ya_ref[...], wa_ref[...]) + _mm(yb_ref[...], wb_ref[...])
    h_ref[...] = h
    u_ref[...] = _rms(h, g_ref[...])


def _outproj0(ya, yb, w_a, w_b, x, g, tm):
    t, d = x.shape
    row = lambda i: (i, 0)
    full = lambda i: (0, 0)
    return pl.pallas_call(
        _outproj0_kernel,
        grid=(t // tm,),
        in_specs=[pl.BlockSpec((tm, ya.shape[1]), row),
                  pl.BlockSpec((tm, yb.shape[1]), row),
                  pl.BlockSpec(w_a.shape, full),
                  pl.BlockSpec(w_b.shape, full),
                  pl.BlockSpec((tm, d), row),
                  pl.BlockSpec((1, d), full)],
        out_specs=[pl.BlockSpec((tm, d), row), pl.BlockSpec((tm, d), row)],
        out_shape=[jax.ShapeDtypeStruct((t, d), F32), jax.ShapeDtypeStruct((t, d), F32)],
        compiler_params=_cparams("parallel"),
        name="l0_outproj",
    )(ya, yb, w_a, w_b, x, g)


def _shifted(u, up_ref, first_of_seq):
    prev_last = jnp.where(first_of_seq, 0.0, up_ref[SUBLANES - 1:SUBLANES, :])
    row = lax.broadcasted_iota(jnp.int32, u.shape, 0)
    return jnp.where(row == 0, prev_last, pltpu.roll(u, 1, axis=0))


def _pair_ones():
    r = lax.broadcasted_iota(jnp.int32, (2 * LANES, LANES), 0) & (LANES - 1)
    c = lax.broadcasted_iota(jnp.int32, (2 * LANES, LANES), 1)
    return jnp.where((r < RW_HEAD_DIM) == (c < RW_HEAD_DIM), 1.0, 0.0).astype(BF16)


def _headsum(x, ones2, split=True):
    n, nblk = x.shape[0], x.shape[1] // LANES
    xs = jnp.concatenate([x[:, i * LANES:(i + 1) * LANES] for i in range(nblk)], axis=0)
    if split:
        s = _mm(jnp.concatenate(_hi_lo(xs), axis=1), ones2)
    else:
        s = _mm(xs.astype(BF16), ones2[0:LANES])
    return jnp.concatenate([s[i * n:(i + 1) * n] for i in range(nblk)], axis=1)


def _front_kernel(u_ref, up_ref, mu_ref, *refs, steps_per_seq, groups):
    wr_ref, wk_ref, wv_ref, wg_ref = (refs[i * groups:(i + 1) * groups] for i in range(4))
    (w1_ref, w2_ref, w0_ref, a1_ref, a2_ref, a0_ref, kk_ref, ka_ref, rk_ref,
     rw_o, kkw_o, kt_o, bt_o, kend_o, bend_o, vb_o, bon_o, gate_o, dec_o) = refs[4 * groups:]
    tm = u_ref.shape[0]
    L = CHUNK
    nch = tm // L
    u = u_ref[...]
    delta = _shifted(u, up_ref, pl.program_id(0) % steps_per_seq == 0) - u
    x_r, x_w, x_k, x_v, x_a, x_g = ((u + delta * mu_ref[i:i + 1, :]).astype(BF16)
                                    for i in range(6))
    zw = jnp.tanh(_mm(x_w, w1_ref[...])).astype(BF16)
    za = _mm(x_a, a1_ref[...]).astype(BF16)
    ones2 = _pair_ones()
    ri = lax.broadcasted_iota(jnp.int32, (L, L), 0)
    ci = lax.broadcasted_iota(jnp.int32, (L, L), 1)
    tril = jnp.where(ri >= ci, 1.0, 0.0).astype(BF16)
    tril2 = jnp.concatenate([tril, tril], axis=1)

    def project(g, out):
        cols = slice(g * FRONT_COLS, (g + 1) * FRONT_COLS)
        out.append(_mm(x_r, wr_ref[g][...]))
        yield
        out.append(_mm(x_k, wk_ref[g][...]))
        yield
        out.append(_mm(x_v, wv_ref[g][...]))
        out.append(_mm(zw, w2_ref[:, cols]))
        out.append(_mm(za, a2_ref[:, cols]))
        yield
        gate_o[:, cols] = _silu(_mm(x_g, wg_ref[g][...]))
        yield

    def per_token(g, proj):
        cols = slice(g * FRONT_COLS, (g + 1) * FRONT_COLS)
        r_all, kraw_all, v_all, zw_all, za_all = proj
        for n in range(nch):
            rows = slice(n * L, (n + 1) * L)
            r, kraw, v = r_all[rows], kraw_all[rows], v_all[rows]
            lw = -DECAY_SCALE * _sigmoid(w0_ref[:, cols] + zw_all[rows])
            a = _sigmoid(a0_ref[:, cols] + za_all[rows])
            kkp = kraw * kk_ref[:, cols]
            kk = kkp * lax.rsqrt(jnp.maximum(_headsum(kkp * kkp, ones2, split=False), 1e-24))
            k = kraw * (1.0 + (a - 1.0) * ka_ref[:, cols])
            b = kk * a
            bon_o[rows, cols] = _headsum(r * k * rk_ref[:, cols], ones2, split=False) * v
            cum = _mm(tril2, jnp.concatenate(_hi_lo(lw), axis=0))
            dec = jnp.exp(cum[L - 1:L, :])
            dec_o[n, :, cols] = dec
            w_out = jnp.exp(-cum)
            w_end = w_out * dec
            rw_o[rows, cols] = (r * jnp.exp(cum)).astype(BF16)
            kkw_o[rows, cols] = (kk * jnp.exp(cum - lw)).astype(BF16)
            kt_o[rows, cols] = (k * w_out).astype(BF16)
            bt_o[rows, cols] = (b * w_out).astype(BF16)
            kend_o[rows, cols] = (k * w_end).astype(BF16)
            bend_o[rows, cols] = (b * w_end).astype(BF16)
            vb_o[rows, cols] = v.astype(BF16)
            yield

    proj = []
    for _ in project(0, proj):
        pass
    for g in range(groups):
        nxt = []
        ahead = project(g + 1, nxt) if g + 1 < groups else iter(())
        for _ in itertools.zip_longest(ahead, per_token(g, proj)):
            pass
        proj = nxt


def _front(u, mu, w_r, w_k, w_v, w_g, w1, w2, w0, a1, a2, a0, k_k, k_a, r_k, seq, tm):
    t, d = u.shape
    n = w_r.shape[1]
    groups = n // FRONT_COLS
    rb = tm // SUBLANES
    row = lambda i: (i, 0)

    def const(shape):
        return pl.BlockSpec(shape, lambda i: (0,) * len(shape), pipeline_mode=pl.Buffered(1))

    def colgroup(g):
        return pl.BlockSpec((d, FRONT_COLS), lambda i: (0, g), pipeline_mode=pl.Buffered(1))

    tok = pl.BlockSpec((tm, n), row)
    tok16 = jax.ShapeDtypeStruct((t, n), BF16)
    tok32 = jax.ShapeDtypeStruct((t, n), F32)
    wspecs = [colgroup(g) for g in range(groups)]
    return pl.pallas_call(
        functools.partial(_front_kernel, steps_per_seq=seq // tm, groups=groups),
        grid=(t // tm,),
        in_specs=[pl.BlockSpec((tm, d), row),
                  pl.BlockSpec((SUBLANES, d), lambda i: (jnp.maximum(i * rb - 1, 0), 0)),
                  const(mu.shape)] + wspecs * 4
        + [const(w1.shape), const(w2.shape), const((1, n)),
           const(a1.shape), const(a2.shape), const((1, n)),
           const((1, n)), const((1, n)), const((1, n))],
        out_specs=[tok] * 9 + [pl.BlockSpec((tm // CHUNK, 1, n), lambda i: (i, 0, 0))],
        out_shape=[tok16] * 7 + [tok32] * 2 + [jax.ShapeDtypeStruct((t // CHUNK, 1, n), F32)],
        compiler_params=_cparams("parallel"),
        name="l1_front",
    )(u, u, mu, *([w_r] * groups + [w_k] * groups + [w_v] * groups + [w_g] * groups),
      w1, w2, w0, a1, a2, a0, k_k, k_a, r_k)


def _rwkv_kernel(rw_ref, kkw_ref, kt_ref, bt_ref, kend_ref, bend_ref, vb_ref, bon_ref, g_ref,
                 dec_ref, lg_ref, lb_ref, o_ref, st, rt_s, y0_s, q_s, n_s, *, pairs):
    tc = rw_ref.shape[0]
    L = CHUNK
    hd = RW_HEAD_DIM
    nchunks = tc // L

    @pl.when(pl.program_id(2) == 0)
    def _():
        st[...] = jnp.zeros_like(st)

    row = lax.broadcasted_iota(jnp.int32, (L, LANES), 0)
    lane = lax.broadcasted_iota(jnp.int32, (L, LANES), 1)
    lane_h = lane & (hd - 1)
    strict = lane_h < row
    incl = lane_h <= row
    eye = jnp.where(lane_h == row, 1.0, 0.0).astype(F32)
    head_a = lane < hd
    row2 = lax.broadcasted_iota(jnp.int32, (2 * L, LANES), 0)
    lane2 = lax.broadcasted_iota(jnp.int32, (2 * L, LANES), 1)
    same_head = (row2 < hd) == (lane2 < hd)
    ones2 = _pair_ones()
    pcols = [slice(p * LANES, (p + 1) * LANES) for p in range(pairs)]

    def split(x):
        x = x.astype(F32)
        return jnp.concatenate([jnp.where(head_a, x, 0.0), jnp.where(head_a, 0.0, x)],
                               axis=0).astype(BF16)

    def build(g):
        jobs = [(c, p, slice(c * L, (c + 1) * L), pcols[p])
                for c in range(g * BUILD_CHUNKS, (g + 1) * BUILD_CHUNKS) for p in range(pairs)]
        J = range(len(jobs))
        ld = lambda ref: [ref[rows, cols] for (_, _, rows, cols) in jobs]
        rw, kkw, vb, bend = ld(rw_ref), ld(kkw_ref), ld(vb_ref), ld(bend_ref)
        zkb = [jnp.concatenate([split(tk), split(tb)], axis=0)
               for tk, tb in zip(ld(kt_ref), ld(bt_ref))]
        lhs = [jnp.concatenate([kkw[j], rw[j]], axis=0) for j in J]
        p_kb = [_mm_nt(lhs[j], zkb[j]) for j in J]
        p_k = [t[:, 0:LANES] for t in p_kb]
        p_b = [t[:, LANES:] for t in p_kb]
        yield
        amat = [jnp.where(strict, p_b[j][0:L], 0.0) for j in J]
        x = [eye - amat[j] for j in J]
        m = [_mm(amat[j].astype(BF16), split(amat[j])) for j in J]
        yield
        for _ in range(L.bit_length() - 3):
            xm = [_mm(jnp.concatenate([x[j], m[j]], axis=0).astype(BF16), split(m[j])) for j in J]
            x = [x[j] + xm[j][0:L] for j in J]
            m = [xm[j][L:2 * L] for j in J]
            yield
        x = [(x[j] + _mm(x[j].astype(BF16), split(m[j]))).astype(BF16) for j in J]
        yield
        a_kr = [jnp.concatenate([jnp.where(strict, p_k[j][0:L], 0.0),
                                 jnp.where(incl, p_k[j][L:2 * L], 0.0)], axis=0).astype(BF16)
                for j in J]
        akv = [_mm(a_kr[j], split(vb[j])) for j in J]
        yield
        ku = [_mm(x[j], jnp.concatenate([split(kkw[j]), split(akv[j][0:L])], axis=1)) for j in J]
        yield
        a_rb = [jnp.where(incl, p_b[j][L:2 * L], 0.0).astype(BF16) for j in J]
        ru = [_mm(a_rb[j], jnp.concatenate([split(ku[j][:, 0:LANES]), split(ku[j][:, LANES:])],
                                           axis=1)) for j in J]
        for j, (_, _, rows, cols) in enumerate(jobs):
            rt_s[rows, cols] = (rw[j].astype(F32) - ru[j][:, 0:LANES]).astype(BF16)
            y0_s[rows, cols] = akv[j][L:2 * L] - ru[j][:, LANES:]
        yield
        vu_t = [jnp.concatenate([vb[j].astype(F32), -ku[j][:, LANES:]], axis=0).T.astype(BF16)
                for j in J]
        kt_t = [ku[j][:, 0:LANES].T.astype(BF16) for j in J]
        kb_end = [jnp.concatenate([t, bend[j]], axis=0) for j, t in enumerate(ld(kend_ref))]
        nmat = [_mm(vu_t[j], kb_end[j]) for j in J]
        yield
        qmat = [_mm(kt_t[j], bend[j]) for j in J]
        for j, (c, p, _, _) in enumerate(jobs):
            n_s[c, p] = jnp.where(same_head, nmat[j], 0.0)
            q_s[c, p] = jnp.where(same_head, qmat[j], 0.0).astype(BF16)
        yield

    def scan_step(c):
        rows = slice(c * L, (c + 1) * L)
        dec = dec_ref[c]
        for p in range(pairs):
            s = st[p]
            sb = s.astype(BF16)
            y0_s[rows, pcols[p]] = y0_s[rows, pcols[p]] + _mm_nt(rt_s[rows, pcols[p]], sb)
            st[p] = s * dec[0:1, pcols[p]] + n_s[c, p] - _mm(sb, q_s[c, p])

    def finish(rows):
        y = y0_s[rows, :]
        yc = y - _headsum(y, ones2, split=False) * (1.0 / hd)
        var = _headsum(yc * yc, ones2, split=False) * (1.0 / hd)
        yn = yc * lax.rsqrt(var + GN_EPS) * lg_ref[...] + lb_ref[...]
        o_ref[rows, :] = ((yn + bon_ref[rows, :]) * g_ref[rows, :]).astype(o_ref.dtype)

    groups = nchunks // BUILD_CHUNKS
    half = BUILD_CHUNKS * L // 2
    for blk in range(groups + 2):
        scans, fins = [], []
        if 1 <= blk <= groups:
            scans = [functools.partial(scan_step, c)
                     for c in range((blk - 1) * BUILD_CHUNKS, blk * BUILD_CHUNKS)]
        if blk >= 2:
            r0 = (blk - 2) * BUILD_CHUNKS * L
            fins = [functools.partial(finish, slice(r0 + i * half, r0 + (i + 1) * half))
                    for i in range(2)]
        side = [t for pair in itertools.zip_longest(scans, fins) for t in pair if t is not None]
        if blk < groups:
            stride = max(1, BUILD_STAGES // (len(side) + 1))
            n_stages = 0
            for i, _ in enumerate(build(blk)):
                n_stages += 1
                if side and i % stride == stride - 1:
                    side.pop(0)()
            assert n_stages == BUILD_STAGES
        for thunk in side:
            thunk()


def _rwkv(rw, kkw, kt, bt, kend, bend, vb, bon, gate, dec, lnx_g, lnx_b, bsz, seq, tc, pairs):
    t, width = rw.shape
    wb = pairs * LANES
    nt = seq // tc
    nchunks = tc // CHUNK
    tok = pl.BlockSpec((tc, wb), lambda b, p, i: (b * nt + i, p))
    par = pl.BlockSpec((1, wb), lambda b, p, i: (0, p))
    return pl.pallas_call(
        functools.partial(_rwkv_kernel, pairs=pairs),
        grid=(bsz, width // wb, nt),
        in_specs=[tok] * 9 + [pl.BlockSpec((nchunks, 1, wb), lambda b, p, i: (b * nt + i, 0, p)),
                              par, par],
        out_specs=tok,
        out_shape=jax.ShapeDtypeStruct((t, width), BF16),
        scratch_shapes=[pltpu.VMEM((pairs, LANES, LANES), F32),
                        pltpu.VMEM((tc, wb), BF16),
                        pltpu.VMEM((tc, wb), F32),
                        pltpu.VMEM((nchunks, pairs, LANES, LANES), BF16),
                        pltpu.VMEM((nchunks, pairs, LANES, LANES), F32)],
        compiler_params=_cparams("parallel", "parallel", "arbitrary"),
        name="l1_rwkv7",
    )(rw, kkw, kt, bt, kend, bend, vb, bon, gate, dec, lnx_g, lnx_b)


def _outproj1_kernel(y_ref, w_ref, h_ref, g_ref, o_ref):
    h = h_ref[...] + _mm(y_ref[...], w_ref[...])
    o_ref[...] = _rms(h, g_ref[...])


def _outproj1(y, w, h, g, tm):
    t, d = h.shape
    row = lambda i: (i, 0)
    full = lambda i: (0, 0)
    return pl.pallas_call(
        _outproj1_kernel,
        grid=(t // tm,),
        in_specs=[pl.BlockSpec((tm, y.shape[1]), row), pl.BlockSpec(w.shape, full),
                  pl.BlockSpec((tm, d), row), pl.BlockSpec((1, d), full)],
        out_specs=pl.BlockSpec((tm, d), row),
        out_shape=jax.ShapeDtypeStruct((t, d), F32),
        compiler_params=_cparams("parallel"),
        name="l1_outproj_final",
    )(y, w, h, g)


class _Tiles(NamedTuple):
    inproj_rglru: int
    hgrn2: int
    outproj: int
    front: int
    rwkv: int


def _tile(n, pref):
    t = min(n, pref)
    assert n % t == 0, (n, pref)
    return t


def _tiles(tokens, seq):
    return _Tiles(inproj_rglru=_tile(seq, 512), hgrn2=_tile(seq, 2048),
                  outproj=_tile(tokens, 1024), front=_tile(seq, 256), rwkv=_tile(seq, 1024))


def _layer0(x2, bsz, seq, tiles, norm_g, w_in, conv_w, conv_b, w_a, b_a, w_x, b_x, lam,
            lb_logits, hg_g, w_out, next_norm_g):
    width = conv_w.shape[1]
    row = lambda p: p.reshape(1, -1)
    ya, zb = _inproj_rglru(x2, row(norm_g), w_in.astype(BF16), conv_w, row(conv_b),
                           w_a.astype(BF16), row(b_a), w_x.astype(BF16), row(b_x), row(lam),
                           bsz, seq, tiles.inproj_rglru)
    yb = _hgrn2(zb, lb_logits, row(hg_g), bsz, seq, tiles.hgrn2, 0, HG_HEADS_PER_STEP)
    w_out = w_out.astype(BF16)
    return _outproj0(ya, yb, w_out[:width], w_out[width:], x2, row(next_norm_g), tiles.outproj)


def _layer1(h, u, bsz, seq, tiles, mu, w_r, w_k, w_v, w_g, w0, w1, w2, a0, a1, a2, k_k, k_a,
            r_k, lnx_g, lnx_b, w_o, final_g):
    row = lambda p: p.reshape(1, -1)
    bf = lambda w: w.astype(BF16)
    ops = _front(u, mu, bf(w_r), bf(w_k), bf(w_v), bf(w_g), bf(w1), bf(w2), row(w0), bf(a1),
                 bf(a2), row(a0), row(k_k), row(k_a), row(r_k), seq, tiles.front)
    y = _rwkv(*ops, row(lnx_g), row(lnx_b), bsz, seq, tiles.rwkv, RW_PAIRS_PER_STEP)
    return _outproj1(y, bf(w_o), h, row(final_g), tiles.outproj)


def kernel(x, ab_norm_g, ab_w_in, rg_conv_w, rg_conv_b, rg_w_a, rg_b_a, rg_w_x, rg_b_x, rg_lambda, hg_lb_logits, hg_norm_g, ab_w_out, c_norm_g, c_mu, c_w_r, c_w_k, c_w_v, c_w_g, c_w0, c_w1, c_w2, c_a0, c_a1, c_a2, c_k_k, c_k_a, c_r_k, c_lnx_g, c_lnx_b, c_w_o, final_g):
    bsz, seq, d = x.shape
    assert ab_norm_g.shape[0] == 1 and c_norm_g.shape[0] == 1, "two-layer trunk only"
    x2 = x.reshape(bsz * seq, d)
    tiles = _tiles(bsz * seq, seq)
    h, u = _layer0(x2, bsz, seq, tiles, ab_norm_g[0], ab_w_in[0], rg_conv_w[0], rg_conv_b[0],
                   rg_w_a[0], rg_b_a[0], rg_w_x[0], rg_b_x[0], rg_lambda[0], hg_lb_logits,
                   hg_norm_g[0], ab_w_out[0], c_norm_g[0])
    out = _layer1(h, u, bsz, seq, tiles, c_mu[0], c_w_r[0], c_w_k[0], c_w_v[0], c_w_g[0],
                  c_w0[0], c_w1[0], c_w2[0], c_a0[0], c_a1[0], c_a2[0], c_k_k[0], c_k_a[0],
                  c_r_k[0].reshape(-1), c_lnx_g[0], c_lnx_b[0], c_w_o[0], final_g)
    return out.reshape(bsz, seq, d)
```

```python
import functools
import itertools
from typing import NamedTuple

import jax
import jax.numpy as jnp
from jax import lax
from jax.experimental import pallas as pl
from jax.experimental.pallas import tpu as pltpu

F32 = jnp.float32
BF16 = jnp.bfloat16

RMS_EPS = 1e-6
GN_EPS = 64e-5
DECAY_SCALE = 0.6065306597126334
RG_C = 8.0
RG_BLOCKS = 8
RG_CONV = 4
RG_ROWS = 32
HG_HEAD_DIM = 128
HG_HEADS_PER_STEP = 4
RW_HEAD_DIM = 64
RW_PAIRS_PER_STEP = 4
CHUNK = 64
BUILD_CHUNKS = 4
BUILD_STAGES = 12
FRONT_COLS = 256
LANES = 128
SUBLANES = 8
VMEM_LIMIT = 56 * 1024 * 1024


def _cparams(*sem):
    return pltpu.CompilerParams(dimension_semantics=sem, vmem_limit_bytes=VMEM_LIMIT)


def _mm(a, b):
    return jnp.dot(a, b, preferred_element_type=F32)


def _mm_nt(a, b):
    return lax.dot_general(a, b, (((1,), (1,)), ((), ())), preferred_element_type=F32)


def _hi_lo(x):
    hi = x.astype(BF16)
    return hi, (x - hi.astype(F32)).astype(BF16)


def _rms(x, g):
    return x * lax.rsqrt(jnp.mean(x * x, axis=-1, keepdims=True) + RMS_EPS) * g


def _sigmoid(x):
    return 1.0 / (1.0 + jnp.exp(-x))


def _silu(x):
    return x * _sigmoid(x)


def _softplus(x):
    return jnp.maximum(x, 0.0) + jnp.log(1.0 + jnp.exp(-jnp.abs(x)))


def _inproj_rglru_kernel(x_ref, g_ref, *refs, nslab):
    w_refs = refs[:nslab]
    (cw_ref, cb_ref, wa_ref, ba_ref, wx_ref, bx_ref, lam_ref,
     ya_ref, zb_ref, xbuf, hcar) = refs[nslab:]
    ts, width = ya_ref.shape
    bd = width // RG_BLOCKS

    @pl.when(pl.program_id(1) == 0)
    def _():
        xbuf[0:SUBLANES, :] = jnp.zeros((SUBLANES, width), F32)
        hcar[...] = jnp.zeros_like(hcar)

    xn = _rms(x_ref[...], g_ref[...]).astype(BF16)

    def project_rest():
        piece = width // 4
        for k in range(2, nslab):
            for q in range(4):
                c0 = (k - 2) * width + q * piece
                zb_ref[:, c0:c0 + piece] = _mm(xn, w_refs[k][:, q * piece:(q + 1) * piece])
                yield

    def rglru():
        xa = _mm(xn, w_refs[0][...])
        ga = _mm(xn, w_refs[1][...])
        xbuf[SUBLANES:SUBLANES + ts, :] = xa
        yield
        sp = _softplus(-lam_ref[...])
        h = hcar[0:1, :]
        ngrp = RG_ROWS // SUBLANES
        for c in range(ts // RG_ROWS):
            r0 = c * RG_ROWS
            rows = slice(r0, r0 + RG_ROWS)
            xc = cb_ref[...] + cw_ref[RG_CONV - 1:RG_CONV, :] * xa[rows]
            for j in range(1, RG_CONV):
                xc = xc + (cw_ref[RG_CONV - 1 - j:RG_CONV - j, :]
                           * xbuf[pl.ds(SUBLANES - j + r0, RG_ROWS), :])
            xcb = xc.astype(BF16)
            gr, gi = [], []
            for n in range(RG_BLOCKS):
                blk = xcb[:, n * bd:(n + 1) * bd]
                gr.append(_mm(blk, wa_ref[n]))
                gi.append(_mm(blk, wx_ref[n]))
            gate_r = _sigmoid(jnp.concatenate(gr, axis=-1) + ba_ref[...])
            gate_i = _sigmoid(jnp.concatenate(gi, axis=-1) + bx_ref[...])
            log_a = (-RG_C) * gate_r * sp
            a = jnp.exp(log_a)
            u = jnp.sqrt(-jnp.tanh(log_a) * (1.0 + a * a)) * (gate_i * xc)
            a = a.reshape(ngrp, SUBLANES, width)
            u = u.reshape(ngrp, SUBLANES, width)
            row = lax.broadcasted_iota(jnp.int32, a.shape, 1)
            d = 1
            while d < SUBLANES:
                keep = row >= d
                u = jnp.where(keep, a * pltpu.roll(u, d, axis=1) + u, u)
                a = jnp.where(keep, a * pltpu.roll(a, d, axis=1), a)
                d *= 2
            hs = []
            for j in range(ngrp):
                hb = a[j] * h + u[j]
                hs.append(hb)
                h = hb[SUBLANES - 1:SUBLANES, :]
            ya_ref[rows, :] = (jnp.concatenate(hs, axis=0) * _silu(ga[rows])).astype(ya_ref.dtype)
            yield
        xbuf[0:SUBLANES, :] = xa[ts - SUBLANES:ts, :]
        hcar[0:1, :] = h

    head, rest = rglru(), project_rest()
    next(head)
    nsub = ts // RG_ROWS
    npiece = 4 * (nslab - 2)
    for c in range(nsub):
        for _ in range((c + 1) * npiece // nsub - c * npiece // nsub):
            next(rest)
        next(head)
    for _ in itertools.chain(head, rest):
        pass


def _inproj_rglru(x, g, w_in, conv_w, conv_b, w_a, b_a, w_x, b_x, lam, bsz, seq, ts):
    t, d = x.shape
    width = conv_w.shape[1]
    nslab = w_in.shape[1] // width
    nt = seq // ts
    row = lambda b, i: (b * nt + i, 0)

    def const(shape):
        return pl.BlockSpec(shape, lambda b, i: (0,) * len(shape), pipeline_mode=pl.Buffered(1))

    def slab(k):
        return pl.BlockSpec((d, width), lambda b, i: (0, k), pipeline_mode=pl.Buffered(1))

    return pl.pallas_call(
        functools.partial(_inproj_rglru_kernel, nslab=nslab),
        grid=(bsz, nt),
        in_specs=[pl.BlockSpec((ts, d), row), const((1, d))] + [slab(k) for k in range(nslab)]
        + [const(conv_w.shape), const((1, width)), const(w_a.shape), const((1, width)),
           const(w_x.shape), const((1, width)), const((1, width))],
        out_specs=[pl.BlockSpec((ts, width), row), pl.BlockSpec((ts, (nslab - 2) * width), row)],
        out_shape=[jax.ShapeDtypeStruct((t, width), BF16),
                   jax.ShapeDtypeStruct((t, (nslab - 2) * width), F32)],
        scratch_shapes=[pltpu.VMEM((ts + SUBLANES, width), F32),
                        pltpu.VMEM((SUBLANES, width), F32)],
        compiler_params=_cparams("parallel", "arbitrary"),
        name="l0_inproj_rglru",
    )(x, g, *([w_in] * nslab), conv_w, conv_b, w_a, b_a, w_x, b_x, lam)


def _hgrn2_kernel(q_ref, f_ref, v_ref, g_ref, lbl_ref, gn_ref, o_ref, st):
    tc = q_ref.shape[0]

    @pl.when(pl.program_id(2) == 0)
    def _():
        st[...] = jnp.zeros_like(st)

    ri = lax.broadcasted_iota(jnp.int32, (CHUNK, CHUNK), 0)
    ci = lax.broadcasted_iota(jnp.int32, (CHUNK, CHUNK), 1)
    causal = ri >= ci
    tril = jnp.where(causal, 1.0, 0.0).astype(BF16)
    tril2 = jnp.concatenate([tril, tril], axis=1)
    C = range(tc // CHUNK)
    sl = [slice(n * CHUNK, (n + 1) * CHUNK) for n in C]

    for h in range(st.shape[0]):
        hc = slice(h * HG_HEAD_DIM, (h + 1) * HG_HEAD_DIM)
        lbl = lbl_ref[:, hc]
        e = jnp.exp(lbl - jnp.max(lbl, axis=0, keepdims=True))
        lb = e[0:1, :] / jnp.sum(e, axis=0, keepdims=True)

        sg = _sigmoid(f_ref[:, hc])
        log_f = jnp.log(lb + (1.0 - lb) * sg)
        k_all = (1.0 - lb) * (1.0 - sg)
        lf_hi, lf_lo = _hi_lo(jnp.concatenate([log_f[s] for s in sl], axis=1))
        cum_w = _mm(tril2, jnp.concatenate([lf_hi, lf_lo], axis=0))
        cum = [cum_w[:, n * HG_HEAD_DIM:(n + 1) * HG_HEAD_DIM] for n in C]
        dec = [jnp.exp(cum[n][CHUNK - 1:CHUNK, :]) for n in C]
        k = [k_all[s] for s in sl]
        v = [v_ref[s, hc] for s in sl]
        q_dec = [(q_ref[sl[n], hc] * jnp.exp(cum[n])).astype(BF16) for n in C]
        k_inv = [k[n] * jnp.exp(-cum[n]) for n in C]
        k_end = [(k_inv[n] * dec[n]).astype(BF16) for n in C]
        k_inv = [t.astype(BF16) for t in k_inv]
        scores = [jnp.where(causal, _mm_nt(q_dec[n], k_inv[n]), 0.0).astype(BF16) for n in C]
        upd = [_mm(v[n].T.astype(BF16), k_end[n]) for n in C]
        o = [_mm(scores[n], v[n].astype(BF16)) for n in C]
        state = st[h]
        states = []
        for n in C:
            states.append(state.astype(BF16))
            state = state * dec[n] + upd[n]
        st[h] = state
        o = jnp.concatenate([o[n] + _mm_nt(q_dec[n], states[n]) for n in C], axis=0)
        o = o * lax.rsqrt(jnp.mean(o * o, axis=-1, keepdims=True) + RMS_EPS) * gn_ref[...]
        o_ref[:, hc] = (o * _silu(g_ref[:, hc])).astype(o_ref.dtype)


def _hgrn2(z, lb_logits, hg_g, bsz, seq, tc, col0, hps):
    width = lb_logits.shape[1]
    wb = hps * HG_HEAD_DIM
    nt = seq // tc

    def col(k):
        return lambda b, h, i: (b * nt + i, (col0 + k * width) // wb + h)

    return pl.pallas_call(
        _hgrn2_kernel,
        grid=(bsz, width // wb, nt),
        in_specs=[pl.BlockSpec((tc, wb), col(0)),
                  pl.BlockSpec((tc, wb), col(1)),
                  pl.BlockSpec((tc, wb), col(2)),
                  pl.BlockSpec((tc, wb), col(3)),
                  pl.BlockSpec((lb_logits.shape[0], wb), lambda b, h, i: (0, h)),
                  pl.BlockSpec((1, HG_HEAD_DIM), lambda b, h, i: (0, 0))],
        out_specs=pl.BlockSpec((tc, wb), lambda b, h, i: (b * nt + i, h)),
        out_shape=jax.ShapeDtypeStruct((bsz * seq, width), BF16),
        scratch_shapes=[pltpu.VMEM((hps, HG_HEAD_DIM, HG_HEAD_DIM), F32)],
        compiler_params=_cparams("parallel", "parallel", "arbitrary"),
        name="l0_hgrn2",
    )(z, z, z, z, lb_logits, hg_g)


def _outproj0_kernel(ya_ref, yb_ref, wa_ref, wb_ref, x_ref, g_ref, h_ref, u_ref):
    h = x_ref[...] + _mm(ya_ref[...], wa_ref[...]) + _mm(yb_ref[...], wb_ref[...])
    h_ref[...] = h
    u_ref[...] = _rms(h, g_ref[...])


def _outproj0(ya, yb, w_out, x, g, tm):
    t, d = x.shape
    wa, wb = ya.shape[1], yb.shape[1]
    assert wa == wb and w_out.shape[0] == wa + wb
    row = lambda i: (i, 0)
    full = lambda i: (0, 0)
    return pl.pallas_call(
        _outproj0_kernel,
        grid=(t // tm,),
        in_specs=[pl.BlockSpec((tm, wa), row),
                  pl.BlockSpec((tm, wb), row),
                  pl.BlockSpec((wa, d), lambda i: (0, 0)),
                  pl.BlockSpec((wb, d), lambda i: (1, 0)),
                  pl.BlockSpec((tm, d), row),
                  pl.BlockSpec((1, d), full)],
        out_specs=[pl.BlockSpec((tm, d), row), pl.BlockSpec((tm, d), row)],
        out_shape=[jax.ShapeDtypeStruct((t, d), F32), jax.ShapeDtypeStruct((t, d), F32)],
        compiler_params=_cparams("parallel"),
        name="l0_outproj",
    )(ya, yb, w_out, w_out, x, g)


def _shifted(u, up_ref, first_of_seq):
    prev_last = jnp.where(first_of_seq, 0.0, up_ref[SUBLANES - 1:SUBLANES, :])
    row = lax.broadcasted_iota(jnp.int32, u.shape, 0)
    return jnp.where(row == 0, prev_last, pltpu.roll(u, 1, axis=0))


def _pair_ones():
    r = lax.broadcasted_iota(jnp.int32, (2 * LANES, LANES), 0) & (LANES - 1)
    c = lax.broadcasted_iota(jnp.int32, (2 * LANES, LANES), 1)
    return jnp.where((r < RW_HEAD_DIM) == (c < RW_HEAD_DIM), 1.0, 0.0).astype(BF16)


def _headsum(x, ones2, split=True):
    n, nblk = x.shape[0], x.shape[1] // LANES
    xs = jnp.concatenate([x[:, i * LANES:(i + 1) * LANES] for i in range(nblk)], axis=0)
    if split:
        s = _mm(jnp.concatenate(_hi_lo(xs), axis=1), ones2)
    else:
        s = _mm(xs.astype(BF16), ones2[0:LANES])
    return jnp.concatenate([s[i * n:(i + 1) * n] for i in range(nblk)], axis=1)


def _front_kernel(u_ref, up_ref, mu_ref, *refs, steps_per_seq, groups):
    wr_ref, wk_ref, wv_ref, wg_ref = (refs[i * groups:(i + 1) * groups] for i in range(4))
    (w1_ref, w2_ref, w0_ref, a1_ref, a2_ref, a0_ref, kk_ref, ka_ref, rk_ref,
     rw_o, kkw_o, kt_o, bt_o, kend_o, bend_o, vb_o, bon_o, gate_o, dec_o) = refs[4 * groups:]
    tm = u_ref.shape[0]
    L = CHUNK
    nch = tm // L
    u = u_ref[...]
    delta = _shifted(u, up_ref, pl.program_id(0) % steps_per_seq == 0) - u
    x_r, x_w, x_k, x_v, x_a, x_g = ((u + delta * mu_ref[i:i + 1, :]).astype(BF16)
                                    for i in range(6))
    zw = jnp.tanh(_mm(x_w, w1_ref[...])).astype(BF16)
    za = _mm(x_a, a1_ref[...]).astype(BF16)
    ones2 = _pair_ones()
    ri = lax.broadcasted_iota(jnp.int32, (L, L), 0)
    ci = lax.broadcasted_iota(jnp.int32, (L, L), 1)
    tril = jnp.where(ri >= ci, 1.0, 0.0).astype(BF16)
    tril2 = jnp.concatenate([tril, tril], axis=1)

    def project(g, out):
        cols = slice(g * FRONT_COLS, (g + 1) * FRONT_COLS)
        out.append(_mm(x_r, wr_ref[g][...]))
        yield
        out.append(_mm(x_k, wk_ref[g][...]))
        yield
        out.append(_mm(x_v, wv_ref[g][...]))
        out.append(_mm(zw, w2_ref[:, cols]))
        out.append(_mm(za, a2_ref[:, cols]))
        yield
        gate_o[:, cols] = _silu(_mm(x_g, wg_ref[g][...]))
        yield

    def per_token(g, proj):
        cols = slice(g * FRONT_COLS, (g + 1) * FRONT_COLS)
        r_all, kraw_all, v_all, zw_all, za_all = proj
        for n in range(nch):
            rows = slice(n * L, (n + 1) * L)
            r, kraw, v = r_all[rows], kraw_all[rows], v_all[rows]
            lw = -DECAY_SCALE * _sigmoid(w0_ref[:, cols] + zw_all[rows])
            a = _sigmoid(a0_ref[:, cols] + za_all[rows])
            kkp = kraw * kk_ref[:, cols]
            kk = kkp * lax.rsqrt(jnp.maximum(_headsum(kkp * kkp, ones2, split=False), 1e-24))
            k = kraw * (1.0 + (a - 1.0) * ka_ref[:, cols])
            b = kk * a
            bon_o[rows, cols] = _headsum(r * k * rk_ref[:, cols], ones2, split=False) * v
            cum = _mm(tril2, jnp.concatenate(_hi_lo(lw), axis=0))
            dec = jnp.exp(cum[L - 1:L, :])
            dec_o[n, :, cols] = dec
            w_out = jnp.exp(-cum)
            w_end = w_out * dec
            rw_o[rows, cols] = (r * jnp.exp(cum)).astype(BF16)
            kkw_o[rows, cols] = (kk * jnp.exp(cum - lw)).astype(BF16)
            kt_o[rows, cols] = (k * w_out).astype(BF16)
            bt_o[rows, cols] = (b * w_out).astype(BF16)
            kend_o[rows, cols] = (k * w_end).astype(BF16)
            bend_o[rows, cols] = (b * w_end).astype(BF16)
            vb_o[rows, cols] = v.astype(BF16)
            yield

    proj = []
    for _ in project(0, proj):
        pass
    for g in range(groups):
        nxt = []
        ahead = project(g + 1, nxt) if g + 1 < groups else iter(())
        for _ in itertools.zip_longest(ahead, per_token(g, proj)):
            pass
        proj = nxt


def _front(u, mu, w_r, w_k, w_v, w_g, w1, w2, w0, a1, a2, a0, k_k, k_a, r_k, seq, tm):
    t, d = u.shape
    n = w_r.shape[1]
    groups = n // FRONT_COLS
    rb = tm // SUBLANES
    row = lambda i: (i, 0)

    def const(shape):
        return pl.BlockSpec(shape, lambda i: (0,) * len(shape), pipeline_mode=pl.Buffered(1))

    def colgroup(g):
        return pl.BlockSpec((d, FRONT_COLS), lambda i: (0, g), pipeline_mode=pl.Buffered(1))

    tok = pl.BlockSpec((tm, n), row)
    tok16 = jax.ShapeDtypeStruct((t, n), BF16)
    tok32 = jax.ShapeDtypeStruct((t, n), F32)
    wspecs = [colgroup(g) for g in range(groups)]
    return pl.pallas_call(
        functools.partial(_front_kernel, steps_per_seq=seq // tm, groups=groups),
        grid=(t // tm,),
        in_specs=[pl.BlockSpec((tm, d), row),
                  pl.BlockSpec((SUBLANES, d), lambda i: (jnp.maximum(i * rb - 1, 0), 0)),
                  const(mu.shape)] + wspecs * 4
        + [const(w1.shape), const(w2.shape), const((1, n)),
           const(a1.shape), const(a2.shape), const((1, n)),
           const((1, n)), const((1, n)), const((1, n))],
        out_specs=[tok] * 9 + [pl.BlockSpec((tm // CHUNK, 1, n), lambda i: (i, 0, 0))],
        out_shape=[tok16] * 7 + [tok32] * 2 + [jax.ShapeDtypeStruct((t // CHUNK, 1, n), F32)],
        compiler_params=_cparams("parallel"),
        name="l1_front",
    )(u, u, mu, *([w_r] * groups + [w_k] * groups + [w_v] * groups + [w_g] * groups),
      w1, w2, w0, a1, a2, a0, k_k, k_a, r_k)


def _rwkv_kernel(rw_ref, kkw_ref, kt_ref, bt_ref, kend_ref, bend_ref, vb_ref, bon_ref, g_ref,
                 dec_ref, lg_ref, lb_ref, o_ref, st, rt_s, y0_s, q_s, n_s, *, pairs):
    tc = rw_ref.shape[0]
    L = CHUNK
    hd = RW_HEAD_DIM
    nchunks = tc // L

    @pl.when(pl.program_id(2) == 0)
    def _():
        st[...] = jnp.zeros_like(st)

    row = lax.broadcasted_iota(jnp.int32, (L, LANES), 0)
    lane = lax.broadcasted_iota(jnp.int32, (L, LANES), 1)
    lane_h = lane & (hd - 1)
    strict = lane_h < row
    incl = lane_h <= row
    eye = jnp.where(lane_h == row, 1.0, 0.0).astype(F32)
    head_a = lane < hd
    row2 = lax.broadcasted_iota(jnp.int32, (2 * L, LANES), 0)
    lane2 = lax.broadcasted_iota(jnp.int32, (2 * L, LANES), 1)
    same_head = (row2 < hd) == (lane2 < hd)
    ones2 = _pair_ones()
    pcols = [slice(p * LANES, (p + 1) * LANES) for p in range(pairs)]

    def split(x):
        x = x.astype(F32)
        return jnp.concatenate([jnp.where(head_a, x, 0.0), jnp.where(head_a, 0.0, x)],
                               axis=0).astype(BF16)

    def build(g):
        jobs = [(c, p, slice(c * L, (c + 1) * L), pcols[p])
                for c in range(g * BUILD_CHUNKS, (g + 1) * BUILD_CHUNKS) for p in range(pairs)]
        J = range(len(jobs))
        ld = lambda ref: [ref[rows, cols] for (_, _, rows, cols) in jobs]
        rw, kkw, vb, bend = ld(rw_ref), ld(kkw_ref), ld(vb_ref), ld(bend_ref)
        zkb = [jnp.concatenate([split(tk), split(tb)], axis=0)
               for tk, tb in zip(ld(kt_ref), ld(bt_ref))]
        lhs = [jnp.concatenate([kkw[j], rw[j]], axis=0) for j in J]
        p_kb = [_mm_nt(lhs[j], zkb[j]) for j in J]
        p_k = [t[:, 0:LANES] for t in p_kb]
        p_b = [t[:, LANES:] for t in p_kb]
        yield
        amat = [jnp.where(strict, p_b[j][0:L], 0.0) for j in J]
        x = [eye - amat[j] for j in J]
        m = [_mm(amat[j].astype(BF16), split(amat[j])) for j in J]
        yield
        for _ in range(L.bit_length() - 3):
            xm = [_mm(jnp.concatenate([x[j], m[j]], axis=0).astype(BF16), split(m[j])) for j in J]
            x = [x[j] + xm[j][0:L] for j in J]
            m = [xm[j][L:2 * L] for j in J]
            yield
        x = [(x[j] + _mm(x[j].astype(BF16), split(m[j]))).astype(BF16) for j in J]
        yield
        a_kr = [jnp.concatenate([jnp.where(strict, p_k[j][0:L], 0.0),
                                 jnp.where(incl, p_k[j][L:2 * L], 0.0)], axis=0).astype(BF16)
                for j in J]
        akv = [_mm(a_kr[j], split(vb[j])) for j in J]
        yield
        ku = [_mm(x[j], jnp.concatenate([split(kkw[j]), split(akv[j][0:L])], axis=1)) for j in J]
        yield
        a_rb = [jnp.where(incl, p_b[j][L:2 * L], 0.0).astype(BF16) for j in J]
        ru = [_mm(a_rb[j], jnp.concatenate([split(ku[j][:, 0:LANES]), split(ku[j][:, LANES:])],
                                           axis=1)) for j in J]
        for j, (_, _, rows, cols) in enumerate(jobs):
            rt_s[rows, cols] = (rw[j].astype(F32) - ru[j][:, 0:LANES]).astype(BF16)
            y0_s[rows, cols] = akv[j][L:2 * L] - ru[j][:, LANES:]
        yield
        vu_t = [jnp.concatenate([vb[j].astype(F32), -ku[j][:, LANES:]], axis=0).T.astype(BF16)
                for j in J]
        kt_t = [ku[j][:, 0:LANES].T.astype(BF16) for j in J]
        kb_end = [jnp.concatenate([t, bend[j]], axis=0) for j, t in enumerate(ld(kend_ref))]
        nmat = [_mm(vu_t[j], kb_end[j]) for j in J]
        yield
        qmat = [_mm(kt_t[j], bend[j]) for j in J]
        for j, (c, p, _, _) in enumerate(jobs):
            n_s[c, p] = jnp.where(same_head, nmat[j], 0.0)
            q_s[c, p] = jnp.where(same_head, qmat[j], 0.0).astype(BF16)
        yield

    def scan_step(c):
        rows = slice(c * L, (c + 1) * L)
        dec = dec_ref[c]
        for p in range(pairs):
            s = st[p]
            sb = s.astype(BF16)
            y0_s[rows, pcols[p]] = y0_s[rows, pcols[p]] + _mm_nt(rt_s[rows, pcols[p]], sb)
            st[p] = s * dec[0:1, pcols[p]] + n_s[c, p] - _mm(sb, q_s[c, p])

    def finish(rows):
        y = y0_s[rows, :]
        yc = y - _headsum(y, ones2, split=False) * (1.0 / hd)
        var = _headsum(yc * yc, ones2, split=False) * (1.0 / hd)
        yn = yc * lax.rsqrt(var + GN_EPS) * lg_ref[...] + lb_ref[...]
        o_ref[rows, :] = ((yn + bon_ref[rows, :]) * g_ref[rows, :]).astype(o_ref.dtype)

    groups = nchunks // BUILD_CHUNKS
    half = BUILD_CHUNKS * L // 2
    for blk in range(groups + 2):
        scans, fins = [], []
        if 1 <= blk <= groups:
            scans = [functools.partial(scan_step, c)
                     for c in range((blk - 1) * BUILD_CHUNKS, blk * BUILD_CHUNKS)]
        if blk >= 2:
            r0 = (blk - 2) * BUILD_CHUNKS * L
            fins = [functools.partial(finish, slice(r0 + i * half, r0 + (i + 1) * half))
                    for i in range(2)]
        side = [t for pair in itertools.zip_longest(scans, fins) for t in pair if t is not None]
        if blk < groups:
            stride = max(1, BUILD_STAGES // (len(side) + 1))
            n_stages = 0
            for i, _ in enumerate(build(blk)):
                n_stages += 1
                if side and i % stride == stride - 1:
                    side.pop(0)()
            assert n_stages == BUILD_STAGES
        for thunk in side:
            thunk()


def _rwkv(rw, kkw, kt, bt, kend, bend, vb, bon, gate, dec, lnx_g, lnx_b, bsz, seq, tc, pairs):
    t, width = rw.shape
    wb = pairs * LANES
    nt = seq // tc
    nchunks = tc // CHUNK
    tok = pl.BlockSpec((tc, wb), lambda b, p, i: (b * nt + i, p))
    par = pl.BlockSpec((1, wb), lambda b, p, i: (0, p))
    return pl.pallas_call(
        functools.partial(_rwkv_kernel, pairs=pairs),
        grid=(bsz, width // wb, nt),
        in_specs=[tok] * 9 + [pl.BlockSpec((nchunks, 1, wb), lambda b, p, i: (b * nt + i, 0, p)),
                              par, par],
        out_specs=tok,
        out_shape=jax.ShapeDtypeStruct((t, width), BF16),
        scratch_shapes=[pltpu.VMEM((pairs, LANES, LANES), F32),
                        pltpu.VMEM((tc, wb), BF16),
                        pltpu.VMEM((tc, wb), F32),
                        pltpu.VMEM((nchunks, pairs, LANES, LANES), BF16),
                        pltpu.VMEM((nchunks, pairs, LANES, LANES), F32)],
        compiler_params=_cparams("parallel", "parallel", "arbitrary"),
        name="l1_rwkv7",
    )(rw, kkw, kt, bt, kend, bend, vb, bon, gate, dec, lnx_g, lnx_b)


def _outproj1_kernel(y_ref, w_ref, h_ref, g_ref, o_ref):
    h = h_ref[...] + _mm(y_ref[...], w_ref[...])
    o_ref[...] = _rms(h, g_ref[...])


def _outproj1(y, w, h, g, tm):
    t, d = h.shape
    row = lambda i: (i, 0)
    full = lambda i: (0, 0)
    return pl.pallas_call(
        _outproj1_kernel,
        grid=(t // tm,),
        in_specs=[pl.BlockSpec((tm, y.shape[1]), row), pl.BlockSpec(w.shape, full),
                  pl.BlockSpec((tm, d), row), pl.BlockSpec((1, d), full)],
        out_specs=pl.BlockSpec((tm, d), row),
        out_shape=jax.ShapeDtypeStruct((t, d), F32),
        compiler_params=_cparams("parallel"),
        name="l1_outproj_final",
    )(y, w, h, g)


class _Tiles(NamedTuple):
    inproj_rglru: int
    hgrn2: int
    outproj: int
    front: int
    rwkv: int


def _tile(n, pref):
    t = min(n, pref)
    assert n % t == 0, (n, pref)
    return t


def _tiles(tokens, seq):
    return _Tiles(inproj_rglru=_tile(seq, 512), hgrn2=_tile(seq, 2048),
                  outproj=_tile(tokens, 1024), front=_tile(seq, 256), rwkv=_tile(seq, 1024))


def _layer0(x2, bsz, seq, tiles, norm_g, w_in, conv_w, conv_b, w_a, b_a, w_x, b_x, lam,
            lb_logits, hg_g, w_out, next_norm_g):
    row = lambda p: p.reshape(1, -1)
    ya, zb = _inproj_rglru(x2, row(norm_g), w_in.astype(BF16), conv_w, row(conv_b),
                           w_a.astype(BF16), row(b_a), w_x.astype(BF16), row(b_x), row(lam),
                           bsz, seq, tiles.inproj_rglru)
    yb = _hgrn2(zb, lb_logits, row(hg_g), bsz, seq, tiles.hgrn2, 0, HG_HEADS_PER_STEP)
    return _outproj0(ya, yb, w_out.astype(BF16), x2, row(next_norm_g), tiles.outproj)


def _layer1(h, u, bsz, seq, tiles, mu, w_r, w_k, w_v, w_g, w0, w1, w2, a0, a1, a2, k_k, k_a,
            r_k, lnx_g, lnx_b, w_o, final_g):
    row = lambda p: p.reshape(1, -1)
    bf = lambda w: w.astype(BF16)
    ops = _front(u, mu, bf(w_r), bf(w_k), bf(w_v), bf(w_g), bf(w1), bf(w2), row(w0), bf(a1),
                 bf(a2), row(a0), row(k_k), row(k_a), row(r_k), seq, tiles.front)
    y = _rwkv(*ops, row(lnx_g), row(lnx_b), bsz, seq, tiles.rwkv, RW_PAIRS_PER_STEP)
    return _outproj1(y, bf(w_o), h, row(final_g), tiles.outproj)


def kernel(x, ab_norm_g, ab_w_in, rg_conv_w, rg_conv_b, rg_w_a, rg_b_a, rg_w_x, rg_b_x, rg_lambda, hg_lb_logits, hg_norm_g, ab_w_out, c_norm_g, c_mu, c_w_r, c_w_k, c_w_v, c_w_g, c_w0, c_w1, c_w2, c_a0, c_a1, c_a2, c_k_k, c_k_a, c_r_k, c_lnx_g, c_lnx_b, c_w_o, final_g):
    bsz, seq, d = x.shape
    assert ab_norm_g.shape[0] == 1 and c_norm_g.shape[0] == 1, "two-layer trunk only"
    x2 = x.reshape(bsz * seq, d)
    tiles = _tiles(bsz * seq, seq)
    h, u = _layer0(x2, bsz, seq, tiles, ab_norm_g[0], ab_w_in[0], rg_conv_w[0], rg_conv_b[0],
                   rg_w_a[0], rg_b_a[0], rg_w_x[0], rg_b_x[0], rg_lambda[0], hg_lb_logits,
                   hg_norm_g[0], ab_w_out[0], c_norm_g[0])
    out = _layer1(h, u, bsz, seq, tiles, c_mu[0], c_w_r[0], c_w_k[0], c_w_v[0], c_w_g[0],
                  c_w0[0], c_w1[0], c_w2[0], c_a0[0], c_a1[0], c_a2[0], c_k_k[0], c_k_a[0],
                  c_r_k[0].reshape(-1), c_lnx_g[0], c_lnx_b[0], c_w_o[0], final_g)
    return out.reshape(bsz, seq, d)
```

```python
import functools
import itertools
from typing import NamedTuple

import jax
import jax.numpy as jnp
from jax import lax
from jax.experimental import pallas as pl
from jax.experimental.pallas import tpu as pltpu

F32 = jnp.float32
BF16 = jnp.bfloat16

RMS_EPS = 1e-6
GN_EPS = 64e-5
DECAY_SCALE = 0.6065306597126334
RG_C = 8.0
RG_BLOCKS = 8
RG_CONV = 4
RG_ROWS = 32
HG_HEAD_DIM = 128
HG_HEADS_PER_STEP = 2
RW_HEAD_DIM = 64
RW_PAIRS_PER_STEP = 4
CHUNK = 64
BUILD_CHUNKS = 4
BUILD_STAGES = 12
FRONT_COLS = 256
LANES = 128
SUBLANES = 8
VMEM_LIMIT = 56 * 1024 * 1024


def _cparams(*sem):
    return pltpu.CompilerParams(dimension_semantics=sem, vmem_limit_bytes=VMEM_LIMIT)


def _mm(a, b):
    return jnp.dot(a, b, preferred_element_type=F32)


def _mm_nt(a, b):
    return lax.dot_general(a, b, (((1,), (1,)), ((), ())), preferred_element_type=F32)


def _hi_lo(x):
    hi = x.astype(BF16)
    return hi, (x - hi.astype(F32)).astype(BF16)


def _rms(x, g):
    return x * lax.rsqrt(jnp.mean(x * x, axis=-1, keepdims=True) + RMS_EPS) * g


def _sigmoid(x):
    return 1.0 / (1.0 + jnp.exp(-x))


def _silu(x):
    return x * _sigmoid(x)


def _softplus(x):
    return jnp.maximum(x, 0.0) + jnp.log(1.0 + jnp.exp(-jnp.abs(x)))


def _inproj_rglru_kernel(x_ref, g_ref, *refs, nslab):
    w_refs = refs[:nslab]
    (cw_ref, cb_ref, wa_ref, ba_ref, wx_ref, bx_ref, lam_ref,
     ya_ref, zb_ref, xbuf, hcar) = refs[nslab:]
    ts, width = ya_ref.shape
    bd = width // RG_BLOCKS

    @pl.when(pl.program_id(1) == 0)
    def _():
        xbuf[0:SUBLANES, :] = jnp.zeros((SUBLANES, width), F32)
        hcar[...] = jnp.zeros_like(hcar)

    xn = _rms(x_ref[...], g_ref[...]).astype(BF16)

    def project_rest():
        piece = width // 4
        for k in range(2, nslab):
            for q in range(4):
                c0 = (k - 2) * width + q * piece
                zb_ref[:, c0:c0 + piece] = _mm(xn, w_refs[k][:, q * piece:(q + 1) * piece])
                yield

    def rglru():
        xa = _mm(xn, w_refs[0][...])
        ga = _mm(xn, w_refs[1][...])
        xbuf[SUBLANES:SUBLANES + ts, :] = xa
        yield
        sp = _softplus(-lam_ref[...])
        h = hcar[0:1, :]
        ngrp = RG_ROWS // SUBLANES
        for c in range(ts // RG_ROWS):
            r0 = c * RG_ROWS
            rows = slice(r0, r0 + RG_ROWS)
            xc = cb_ref[...] + cw_ref[RG_CONV - 1:RG_CONV, :] * xa[rows]
            for j in range(1, RG_CONV):
                xc = xc + (cw_ref[RG_CONV - 1 - j:RG_CONV - j, :]
                           * xbuf[pl.ds(SUBLANES - j + r0, RG_ROWS), :])
            xcb = xc.astype(BF16)
            gr, gi = [], []
            for n in range(RG_BLOCKS):
                blk = xcb[:, n * bd:(n + 1) * bd]
                gr.append(_mm(blk, wa_ref[n]))
                gi.append(_mm(blk, wx_ref[n]))
            gate_r = _sigmoid(jnp.concatenate(gr, axis=-1) + ba_ref[...])
            gate_i = _sigmoid(jnp.concatenate(gi, axis=-1) + bx_ref[...])
            log_a = (-RG_C) * gate_r * sp
            a = jnp.exp(log_a)
            u = jnp.sqrt(-jnp.tanh(log_a) * (1.0 + a * a)) * (gate_i * xc)
            a = a.reshape(ngrp, SUBLANES, width)
            u = u.reshape(ngrp, SUBLANES, width)
            row = lax.broadcasted_iota(jnp.int32, a.shape, 1)
            d = 1
            while d < SUBLANES:
                keep = row >= d
                u = jnp.where(keep, a * pltpu.roll(u, d, axis=1) + u, u)
                a = jnp.where(keep, a * pltpu.roll(a, d, axis=1), a)
                d *= 2
            hs = []
            for j in range(ngrp):
                hb = a[j] * h + u[j]
                hs.append(hb)
                h = hb[SUBLANES - 1:SUBLANES, :]
            ya_ref[rows, :] = (jnp.concatenate(hs, axis=0) * _silu(ga[rows])).astype(ya_ref.dtype)
            yield
        xbuf[0:SUBLANES, :] = xa[ts - SUBLANES:ts, :]
        hcar[0:1, :] = h

    head, rest = rglru(), project_rest()
    next(head)
    nsub = ts // RG_ROWS
    npiece = 4 * (nslab - 2)
    for c in range(nsub):
        next(head)
        for _ in range((c + 1) * npiece // nsub - c * npiece // nsub):
            next(rest)
    for _ in itertools.chain(head, rest):
        pass


def _inproj_rglru(x, g, w_in, conv_w, conv_b, w_a, b_a, w_x, b_x, lam, bsz, seq, ts):
    t, d = x.shape
    width = conv_w.shape[1]
    nslab = w_in.shape[1] // width
    nt = seq // ts
    row = lambda b, i: (b * nt + i, 0)

    def const(shape):
        return pl.BlockSpec(shape, lambda b, i: (0,) * len(shape), pipeline_mode=pl.Buffered(1))

    def slab(k):
        return pl.BlockSpec((d, width), lambda b, i: (0, k), pipeline_mode=pl.Buffered(1))

    return pl.pallas_call(
        functools.partial(_inproj_rglru_kernel, nslab=nslab),
        grid=(bsz, nt),
        in_specs=[pl.BlockSpec((ts, d), row), const((1, d))] + [slab(k) for k in range(nslab)]
        + [const(conv_w.shape), const((1, width)), const(w_a.shape), const((1, width)),
           const(w_x.shape), const((1, width)), const((1, width))],
        out_specs=[pl.BlockSpec((ts, width), row), pl.BlockSpec((ts, (nslab - 2) * width), row)],
        out_shape=[jax.ShapeDtypeStruct((t, width), BF16),
                   jax.ShapeDtypeStruct((t, (nslab - 2) * width), F32)],
        scratch_shapes=[pltpu.VMEM((ts + SUBLANES, width), F32),
                        pltpu.VMEM((SUBLANES, width), F32)],
        compiler_params=_cparams("parallel", "arbitrary"),
        name="l0_inproj_rglru",
    )(x, g, *([w_in] * nslab), conv_w, conv_b, w_a, b_a, w_x, b_x, lam)


def _hgrn2_kernel(q_ref, f_ref, v_ref, g_ref, lbl_ref, gn_ref, o_ref, st):
    tc = q_ref.shape[0]

    @pl.when(pl.program_id(2) == 0)
    def _():
        st[...] = jnp.zeros_like(st)

    ri = lax.broadcasted_iota(jnp.int32, (CHUNK, CHUNK), 0)
    ci = lax.broadcasted_iota(jnp.int32, (CHUNK, CHUNK), 1)
    causal = ri >= ci
    tril = jnp.where(causal, 1.0, 0.0).astype(BF16)
    tril2 = jnp.concatenate([tril, tril], axis=1)
    C = range(tc // CHUNK)
    sl = [slice(n * CHUNK, (n + 1) * CHUNK) for n in C]

    for h in range(st.shape[0]):
        hc = slice(h * HG_HEAD_DIM, (h + 1) * HG_HEAD_DIM)
        lbl = lbl_ref[:, hc]
        e = jnp.exp(lbl - jnp.max(lbl, axis=0, keepdims=True))
        lb = e[0:1, :] / jnp.sum(e, axis=0, keepdims=True)

        sg = _sigmoid(f_ref[:, hc])
        log_f = jnp.log(lb + (1.0 - lb) * sg)
        k_all = (1.0 - lb) * (1.0 - sg)
        lf_hi, lf_lo = _hi_lo(jnp.concatenate([log_f[s] for s in sl], axis=1))
        cum_w = _mm(tril2, jnp.concatenate([lf_hi, lf_lo], axis=0))
        cum = [cum_w[:, n * HG_HEAD_DIM:(n + 1) * HG_HEAD_DIM] for n in C]
        dec = [jnp.exp(cum[n][CHUNK - 1:CHUNK, :]) for n in C]
        k = [k_all[s] for s in sl]
        v = [v_ref[s, hc] for s in sl]
        q_dec = [(q_ref[sl[n], hc] * jnp.exp(cum[n])).astype(BF16) for n in C]
        k_inv = [k[n] * jnp.exp(-cum[n]) for n in C]
        k_end = [(k_inv[n] * dec[n]).astype(BF16) for n in C]
        k_inv = [t.astype(BF16) for t in k_inv]
        scores = [jnp.where(causal, _mm_nt(q_dec[n], k_inv[n]), 0.0).astype(BF16) for n in C]
        upd = [_mm(v[n].T.astype(BF16), k_end[n]) for n in C]
        o = [_mm(scores[n], v[n].astype(BF16)) for n in C]
        state = st[h]
        states = []
        for n in C:
            states.append(state.astype(BF16))
            state = state * dec[n] + upd[n]
        st[h] = state
        o = jnp.concatenate([o[n] + _mm_nt(q_dec[n], states[n]) for n in C], axis=0)
        o = o * lax.rsqrt(jnp.mean(o * o, axis=-1, keepdims=True) + RMS_EPS) * gn_ref[...]
        o_ref[:, hc] = (o * _silu(g_ref[:, hc])).astype(o_ref.dtype)


def _hgrn2(z, lb_logits, hg_g, bsz, seq, tc, col0, hps):
    width = lb_logits.shape[1]
    wb = hps * HG_HEAD_DIM
    nt = seq // tc

    def col(k):
        return lambda b, h, i: (b * nt + i, (col0 + k * width) // wb + h)

    return pl.pallas_call(
        _hgrn2_kernel,
        grid=(bsz, width // wb, nt),
        in_specs=[pl.BlockSpec((tc, wb), col(0)),
                  pl.BlockSpec((tc, wb), col(1)),
                  pl.BlockSpec((tc, wb), col(2)),
                  pl.BlockSpec((tc, wb), col(3)),
                  pl.BlockSpec((lb_logits.shape[0], wb), lambda b, h, i: (0, h)),
                  pl.BlockSpec((1, HG_HEAD_DIM), lambda b, h, i: (0, 0))],
        out_specs=pl.BlockSpec((tc, wb), lambda b, h, i: (b * nt + i, h)),
        out_shape=jax.ShapeDtypeStruct((bsz * seq, width), BF16),
        scratch_shapes=[pltpu.VMEM((hps, HG_HEAD_DIM, HG_HEAD_DIM), F32)],
        compiler_params=_cparams("parallel", "parallel", "arbitrary"),
        name="l0_hgrn2",
    )(z, z, z, z, lb_logits, hg_g)


def _outproj0_kernel(ya_ref, yb_ref, wa_ref, wb_ref, x_ref, g_ref, h_ref, u_ref):
    h = x_ref[...] + _mm(ya_ref[...], wa_ref[...]) + _mm(yb_ref[...], wb_ref[...])
    h_ref[...] = h
    u_ref[...] = _rms(h, g_ref[...])


def _outproj0(ya, yb, w_a, w_b, x, g, tm):
    t, d = x.shape
    row = lambda i: (i, 0)
    full = lambda i: (0, 0)
    return pl.pallas_call(
        _outproj0_kernel,
        grid=(t // tm,),
        in_specs=[pl.BlockSpec((tm, ya.shape[1]), row),
                  pl.BlockSpec((tm, yb.shape[1]), row),
                  pl.BlockSpec(w_a.shape, full),
                  pl.BlockSpec(w_b.shape, full),
                  pl.BlockSpec((tm, d), row),
                  pl.BlockSpec((1, d), full)],
        out_specs=[pl.BlockSpec((tm, d), row), pl.BlockSpec((tm, d), row)],
        out_shape=[jax.ShapeDtypeStruct((t, d), F32), jax.ShapeDtypeStruct((t, d), F32)],
        compiler_params=_cparams("parallel"),
        name="l0_outproj",
    )(ya, yb, w_a, w_b, x, g)


def _shifted(u, up_ref, first_of_seq):
    prev_last = jnp.where(first_of_seq, 0.0, up_ref[SUBLANES - 1:SUBLANES, :])
    row = lax.broadcasted_iota(jnp.int32, u.shape, 0)
    return jnp.where(row == 0, prev_last, pltpu.roll(u, 1, axis=0))


def _pair_ones():
    r = lax.broadcasted_iota(jnp.int32, (2 * LANES, LANES), 0) & (LANES - 1)
    c = lax.broadcasted_iota(jnp.int32, (2 * LANES, LANES), 1)
    return jnp.where((r < RW_HEAD_DIM) == (c < RW_HEAD_DIM), 1.0, 0.0).astype(BF16)


def _headsum(x, ones2, split=True):
    n, nblk = x.shape[0], x.shape[1] // LANES
    xs = jnp.concatenate([x[:, i * LANES:(i + 1) * LANES] for i in range(nblk)], axis=0)
    if split:
        s = _mm(jnp.concatenate(_hi_lo(xs), axis=1), ones2)
    else:
        s = _mm(xs.astype(BF16), ones2[0:LANES])
    return jnp.concatenate([s[i * n:(i + 1) * n] for i in range(nblk)], axis=1)


def _front_kernel(u_ref, up_ref, mu_ref, *refs, steps_per_seq, groups):
    wr_ref, wk_ref, wv_ref, wg_ref = (refs[i * groups:(i + 1) * groups] for i in range(4))
    (w1_ref, w2_ref, w0_ref, a1_ref, a2_ref, a0_ref, kk_ref, ka_ref, rk_ref,
     rw_o, kkw_o, kt_o, bt_o, kend_o, bend_o, vb_o, bon_o, gate_o, dec_o) = refs[4 * groups:]
    tm = u_ref.shape[0]
    L = CHUNK
    nch = tm // L
    u = u_ref[...]
    delta = _shifted(u, up_ref, pl.program_id(0) % steps_per_seq == 0) - u
    x_r, x_w, x_k, x_v, x_a, x_g = ((u + delta * mu_ref[i:i + 1, :]).astype(BF16)
                                    for i in range(6))
    zw = jnp.tanh(_mm(x_w, w1_ref[...])).astype(BF16)
    za = _mm(x_a, a1_ref[...]).astype(BF16)
    ones2 = _pair_ones()
    ri = lax.broadcasted_iota(jnp.int32, (L, L), 0)
    ci = lax.broadcasted_iota(jnp.int32, (L, L), 1)
    tril = jnp.where(ri >= ci, 1.0, 0.0).astype(BF16)
    tril2 = jnp.concatenate([tril, tril], axis=1)

    def project(g, out):
        cols = slice(g * FRONT_COLS, (g + 1) * FRONT_COLS)
        out.append(_mm(x_r, wr_ref[g][...]))
        yield
        out.append(_mm(x_k, wk_ref[g][...]))
        yield
        out.append(_mm(x_v, wv_ref[g][...]))
        out.append(_mm(zw, w2_ref[:, cols]))
        out.append(_mm(za, a2_ref[:, cols]))
        yield

    def gate(g):
        cols = slice(g * FRONT_COLS, (g + 1) * FRONT_COLS)
        gate_o[:, cols] = _silu(_mm(x_g, wg_ref[g][...]))
        yield

    def per_token(g, proj):
        cols = slice(g * FRONT_COLS, (g + 1) * FRONT_COLS)
        r_all, kraw_all, v_all, zw_all, za_all = proj
        for n in range(nch):
            rows = slice(n * L, (n + 1) * L)
            r, kraw, v = r_all[rows], kraw_all[rows], v_all[rows]
            lw = -DECAY_SCALE * _sigmoid(w0_ref[:, cols] + zw_all[rows])
            a = _sigmoid(a0_ref[:, cols] + za_all[rows])
            kkp = kraw * kk_ref[:, cols]
            kk = kkp * lax.rsqrt(jnp.maximum(_headsum(kkp * kkp, ones2, split=False), 1e-24))
            k = kraw * (1.0 + (a - 1.0) * ka_ref[:, cols])
            b = kk * a
            bon_o[rows, cols] = _headsum(r * k * rk_ref[:, cols], ones2, split=False) * v
            cum = _mm(tril2, jnp.concatenate(_hi_lo(lw), axis=0))
            dec = jnp.exp(cum[L - 1:L, :])
            dec_o[n, :, cols] = dec
            w_out = jnp.exp(-cum)
            w_end = w_out * dec
            rw_o[rows, cols] = (r * jnp.exp(cum)).astype(BF16)
            kkw_o[rows, cols] = (kk * jnp.exp(cum - lw)).astype(BF16)
            kt_o[rows, cols] = (k * w_out).astype(BF16)
            bt_o[rows, cols] = (b * w_out).astype(BF16)
            kend_o[rows, cols] = (k * w_end).astype(BF16)
            bend_o[rows, cols] = (b * w_end).astype(BF16)
            vb_o[rows, cols] = v.astype(BF16)
            yield

    proj = []
    for _ in project(0, proj):
        pass
    for g in range(groups):
        nxt = []
        ahead = project(g + 1, nxt) if g + 1 < groups else iter(())
        for _ in itertools.zip_longest(itertools.chain(ahead, gate(g)), per_token(g, proj)):
            pass
        proj = nxt


def _front(u, mu, w_r, w_k, w_v, w_g, w1, w2, w0, a1, a2, a0, k_k, k_a, r_k, seq, tm):
    t, d = u.shape
    n = w_r.shape[1]
    groups = n // FRONT_COLS
    rb = tm // SUBLANES
    row = lambda i: (i, 0)

    def const(shape):
        return pl.BlockSpec(shape, lambda i: (0,) * len(shape), pipeline_mode=pl.Buffered(1))

    def colgroup(g):
        return pl.BlockSpec((d, FRONT_COLS), lambda i: (0, g), pipeline_mode=pl.Buffered(1))

    tok = pl.BlockSpec((tm, n), row)
    tok16 = jax.ShapeDtypeStruct((t, n), BF16)
    tok32 = jax.ShapeDtypeStruct((t, n), F32)
    wspecs = [colgroup(g) for g in range(groups)]
    return pl.pallas_call(
        functools.partial(_front_kernel, steps_per_seq=seq // tm, groups=groups),
        grid=(t // tm,),
        in_specs=[pl.BlockSpec((tm, d), row),
                  pl.BlockSpec((SUBLANES, d), lambda i: (jnp.maximum(i * rb - 1, 0), 0)),
                  const(mu.shape)] + wspecs * 4
        + [const(w1.shape), const(w2.shape), const((1, n)),
           const(a1.shape), const(a2.shape), const((1, n)),
           const((1, n)), const((1, n)), const((1, n))],
        out_specs=[tok] * 9 + [pl.BlockSpec((tm // CHUNK, 1, n), lambda i: (i, 0, 0))],
        out_shape=[tok16] * 7 + [tok32] * 2 + [jax.ShapeDtypeStruct((t // CHUNK, 1, n), F32)],
        compiler_params=_cparams("parallel"),
        name="l1_front",
    )(u, u, mu, *([w_r] * groups + [w_k] * groups + [w_v] * groups + [w_g] * groups),
      w1, w2, w0, a1, a2, a0, k_k, k_a, r_k)


def _rwkv_kernel(rw_ref, kkw_ref, kt_ref, bt_ref, kend_ref, bend_ref, vb_ref, bon_ref, g_ref,
                 dec_ref, lg_ref, lb_ref, o_ref, st, rt_s, y0_s, q_s, n_s, *, pairs):
    tc = rw_ref.shape[0]
    L = CHUNK
    hd = RW_HEAD_DIM
    nchunks = tc // L

    @pl.when(pl.program_id(2) == 0)
    def _():
        st[...] = jnp.zeros_like(st)

    row = lax.broadcasted_iota(jnp.int32, (L, LANES), 0)
    lane = lax.broadcasted_iota(jnp.int32, (L, LANES), 1)
    lane_h = lane & (hd - 1)
    strict = lane_h < row
    incl = lane_h <= row
    eye = jnp.where(lane_h == row, 1.0, 0.0).astype(F32)
    head_a = lane < hd
    row2 = lax.broadcasted_iota(jnp.int32, (2 * L, LANES), 0)
    lane2 = lax.broadcasted_iota(jnp.int32, (2 * L, LANES), 1)
    same_head = (row2 < hd) == (lane2 < hd)
    ones2 = _pair_ones()
    pcols = [slice(p * LANES, (p + 1) * LANES) for p in range(pairs)]

    def split(x):
        x = x.astype(F32)
        return jnp.concatenate([jnp.where(head_a, x, 0.0), jnp.where(head_a, 0.0, x)],
                               axis=0).astype(BF16)

    def build(g):
        jobs = [(c, p, slice(c * L, (c + 1) * L), pcols[p])
                for c in range(g * BUILD_CHUNKS, (g + 1) * BUILD_CHUNKS) for p in range(pairs)]
        J = range(len(jobs))
        ld = lambda ref: [ref[rows, cols] for (_, _, rows, cols) in jobs]
        rw, kkw, vb, bend = ld(rw_ref), ld(kkw_ref), ld(vb_ref), ld(bend_ref)
        zkb = [jnp.concatenate([split(tk), split(tb)], axis=0)
               for tk, tb in zip(ld(kt_ref), ld(bt_ref))]
        lhs = [jnp.concatenate([kkw[j], rw[j]], axis=0) for j in J]
        p_kb = [_mm_nt(lhs[j], zkb[j]) for j in J]
        p_k = [t[:, 0:LANES] for t in p_kb]
        p_b = [t[:, LANES:] for t in p_kb]
        yield
        amat = [jnp.where(strict, p_b[j][0:L], 0.0) for j in J]
        x = [eye - amat[j] for j in J]
        m = [_mm(amat[j].astype(BF16), split(amat[j])) for j in J]
        yield
        for _ in range(L.bit_length() - 3):
            xm = [_mm(jnp.concatenate([x[j], m[j]], axis=0).astype(BF16), split(m[j])) for j in J]
            x = [x[j] + xm[j][0:L] for j in J]
            m = [xm[j][L:2 * L] for j in J]
            yield
        x = [(x[j] + _mm(x[j].astype(BF16), split(m[j]))).astype(BF16) for j in J]
        yield
        a_kr = [jnp.concatenate([jnp.where(strict, p_k[j][0:L], 0.0),
                                 jnp.where(incl, p_k[j][L:2 * L], 0.0)], axis=0).astype(BF16)
                for j in J]
        akv = [_mm(a_kr[j], split(vb[j])) for j in J]
        yield
        ku = [_mm(x[j], jnp.concatenate([split(kkw[j]), split(akv[j][0:L])], axis=1)) for j in J]
        yield
        a_rb = [jnp.where(incl, p_b[j][L:2 * L], 0.0).astype(BF16) for j in J]
        ru = [_mm(a_rb[j], jnp.concatenate([split(ku[j][:, 0:LANES]), split(ku[j][:, LANES:])],
                                           axis=1)) for j in J]
        for j, (_, _, rows, cols) in enumerate(jobs):
            rt_s[rows, cols] = (rw[j].astype(F32) - ru[j][:, 0:LANES]).astype(BF16)
            y0_s[rows, cols] = akv[j][L:2 * L] - ru[j][:, LANES:]
        yield
        vu_t = [jnp.concatenate([vb[j].astype(F32), -ku[j][:, LANES:]], axis=0).T.astype(BF16)
                for j in J]
        kt_t = [ku[j][:, 0:LANES].T.astype(BF16) for j in J]
        kb_end = [jnp.concatenate([t, bend[j]], axis=0) for j, t in enumerate(ld(kend_ref))]
        nmat = [_mm(vu_t[j], kb_end[j]) for j in J]
        yield
        qmat = [_mm(kt_t[j], bend[j]) for j in J]
        for j, (c, p, _, _) in enumerate(jobs):
            n_s[c, p] = jnp.where(same_head, nmat[j], 0.0)
            q_s[c, p] = jnp.where(same_head, qmat[j], 0.0).astype(BF16)
        yield

    def scan_step(c):
        rows = slice(c * L, (c + 1) * L)
        dec = dec_ref[c]
        for p in range(pairs):
            s = st[p]
            sb = s.astype(BF16)
            y0_s[rows, pcols[p]] = y0_s[rows, pcols[p]] + _mm_nt(rt_s[rows, pcols[p]], sb)
            st[p] = s * dec[0:1, pcols[p]] + n_s[c, p] - _mm(sb, q_s[c, p])

    def finish(rows):
        y = y0_s[rows, :]
        yc = y - _headsum(y, ones2, split=False) * (1.0 / hd)
        var = _headsum(yc * yc, ones2, split=False) * (1.0 / hd)
        yn = yc * lax.rsqrt(var + GN_EPS) * lg_ref[...] + lb_ref[...]
        o_ref[rows, :] = ((yn + bon_ref[rows, :]) * g_ref[rows, :]).astype(o_ref.dtype)

    groups = nchunks // BUILD_CHUNKS
    half = BUILD_CHUNKS * L // 2
    for blk in range(groups + 2):
        scans, fins = [], []
        if 1 <= blk <= groups:
            scans = [functools.partial(scan_step, c)
                     for c in range((blk - 1) * BUILD_CHUNKS, blk * BUILD_CHUNKS)]
        if blk >= 2:
            r0 = (blk - 2) * BUILD_CHUNKS * L
            fins = [functools.partial(finish, slice(r0 + i * half, r0 + (i + 1) * half))
                    for i in range(2)]
        side = [t for pair in itertools.zip_longest(scans, fins) for t in pair if t is not None]
        if blk < groups:
            stride = max(1, BUILD_STAGES // (len(side) + 1))
            n_stages = 0
            for i, _ in enumerate(build(blk)):
                n_stages += 1
                if side and i % stride == stride - 1:
                    side.pop(0)()
            assert n_stages == BUILD_STAGES
        for thunk in side:
            thunk()


def _rwkv(rw, kkw, kt, bt, kend, bend, vb, bon, gate, dec, lnx_g, lnx_b, bsz, seq, tc, pairs):
    t, width = rw.shape
    wb = pairs * LANES
    nt = seq // tc
    nchunks = tc // CHUNK
    tok = pl.BlockSpec((tc, wb), lambda b, p, i: (b * nt + i, p))
    par = pl.BlockSpec((1, wb), lambda b, p, i: (0, p))
    return pl.pallas_call(
        functools.partial(_rwkv_kernel, pairs=pairs),
        grid=(bsz, width // wb, nt),
        in_specs=[tok] * 9 + [pl.BlockSpec((nchunks, 1, wb), lambda b, p, i: (b * nt + i, 0, p)),
                              par, par],
        out_specs=tok,
        out_shape=jax.ShapeDtypeStruct((t, width), BF16),
        scratch_shapes=[pltpu.VMEM((pairs, LANES, LANES), F32),
                        pltpu.VMEM((tc, wb), BF16),
                        pltpu.VMEM((tc, wb), F32),
                        pltpu.VMEM((nchunks, pairs, LANES, LANES), BF16),
                        pltpu.VMEM((nchunks, pairs, LANES, LANES), F32)],
        compiler_params=_cparams("parallel", "parallel", "arbitrary"),
        name="l1_rwkv7",
    )(rw, kkw, kt, bt, kend, bend, vb, bon, gate, dec, lnx_g, lnx_b)


def _outproj1_kernel(y_ref, w_ref, h_ref, g_ref, o_ref):
    h = h_ref[...] + _mm(y_ref[...], w_ref[...])
    o_ref[...] = _rms(h, g_ref[...])


def _outproj1(y, w, h, g, tm):
    t, d = h.shape
    row = lambda i: (i, 0)
    full = lambda i: (0, 0)
    return pl.pallas_call(
        _outproj1_kernel,
        grid=(t // tm,),
        in_specs=[pl.BlockSpec((tm, y.shape[1]), row), pl.BlockSpec(w.shape, full),
                  pl.BlockSpec((tm, d), row), pl.BlockSpec((1, d), full)],
        out_specs=pl.BlockSpec((tm, d), row),
        out_shape=jax.ShapeDtypeStruct((t, d), F32),
        compiler_params=_cparams("parallel"),
        name="l1_outproj_final",
    )(y, w, h, g)


class _Tiles(NamedTuple):
    inproj_rglru: int
    hgrn2: int
    outproj: int
    front: int
    rwkv: int


def _tile(n, pref):
    t = min(n, pref)
    assert n % t == 0, (n, pref)
    return t


def _tiles(tokens, seq):
    return _Tiles(inproj_rglru=_tile(seq, 512), hgrn2=_tile(seq, 2048),
                  outproj=_tile(tokens, 1024), front=_tile(seq, 256), rwkv=_tile(seq, 1024))


def _layer0(x2, bsz, seq, tiles, norm_g, w_in, conv_w, conv_b, w_a, b_a, w_x, b_x, lam,
            lb_logits, hg_g, w_out, next_norm_g):
    width = conv_w.shape[1]
    row = lambda p: p.reshape(1, -1)
    ya, zb = _inproj_rglru(x2, row(norm_g), w_in.astype(BF16), conv_w, row(conv_b),
                           w_a.astype(BF16), row(b_a), w_x.astype(BF16), row(b_x), row(lam),
                           bsz, seq, tiles.inproj_rglru)
    yb = _hgrn2(zb, lb_logits, row(hg_g), bsz, seq, tiles.hgrn2, 0, HG_HEADS_PER_STEP)
    w_out = w_out.astype(BF16)
    return _outproj0(ya, yb, w_out[:width], w_out[width:], x2, row(next_norm_g), tiles.outproj)


def _layer1(h, u, bsz, seq, tiles, mu, w_r, w_k, w_v, w_g, w0, w1, w2, a0, a1, a2, k_k, k_a,
            r_k, lnx_g, lnx_b, w_o, final_g):
    row = lambda p: p.reshape(1, -1)
    bf = lambda w: w.astype(BF16)
    ops = _front(u, mu, bf(w_r), bf(w_k), bf(w_v), bf(w_g), bf(w1), bf(w2), row(w0), bf(a1),
                 bf(a2), row(a0), row(k_k), row(k_a), row(r_k), seq, tiles.front)
    y = _rwkv(*ops, row(lnx_g), row(lnx_b), bsz, seq, tiles.rwkv, RW_PAIRS_PER_STEP)
    return _outproj1(y, bf(w_o), h, row(final_g), tiles.outproj)


def kernel(x, ab_norm_g, ab_w_in, rg_conv_w, rg_conv_b, rg_w_a, rg_b_a, rg_w_x, rg_b_x, rg_lambda, hg_lb_logits, hg_norm_g, ab_w_out, c_norm_g, c_mu, c_w_r, c_w_k, c_w_v, c_w_g, c_w0, c_w1, c_w2, c_a0, c_a1, c_a2, c_k_k, c_k_a, c_r_k, c_lnx_g, c_lnx_b, c_w_o, final_g):
    bsz, seq, d = x.shape
    assert ab_norm_g.shape[0] == 1 and c_norm_g.shape[0] == 1, "two-layer trunk only"
    x2 = x.reshape(bsz * seq, d)
    tiles = _tiles(bsz * seq, seq)
    h, u = _layer0(x2, bsz, seq, tiles, ab_norm_g[0], ab_w_in[0], rg_conv_w[0], rg_conv_b[0],
                   rg_w_a[0], rg_b_a[0], rg_w_x[0], rg_b_x[0], rg_lambda[0], hg_lb_logits,
                   hg_norm_g[0], ab_w_out[0], c_norm_g[0])
    out = _layer1(h, u, bsz, seq, tiles, c_mu[0], c_w_r[0], c_w_k[0], c_w_v[0], c_w_g[0],
                  c_w0[0], c_w1[0], c_w2[0], c_a0[0], c_a1[0], c_a2[0], c_k_k[0], c_k_a[0],
                  c_r_k[0].reshape(-1), c_lnx_g[0], c_lnx_b[0], c_w_o[0], final_g)
    return out.reshape(bsz, seq, d)
```

```python
import functools
import itertools
from typing import NamedTuple

import jax
import jax.numpy as jnp
from jax import lax
from jax.experimental import pallas as pl
from jax.experimental.pallas import tpu as pltpu

F32 = jnp.float32
BF16 = jnp.bfloat16

RMS_EPS = 1e-6
GN_EPS = 64e-5
DECAY_SCALE = 0.6065306597126334
RG_C = 8.0
RG_BLOCKS = 8
RG_CONV = 4
RG_ROWS = 32
HG_HEAD_DIM = 128
HG_HEADS_PER_STEP = 2
RW_HEAD_DIM = 64
RW_PAIRS_PER_STEP = 4
CHUNK = 64
BUILD_CHUNKS = 4
BUILD_STAGES = 12
FRONT_COLS = 256
LANES = 128
SUBLANES = 8
VMEM_LIMIT = 56 * 1024 * 1024


def _cparams(*sem):
    return pltpu.CompilerParams(dimension_semantics=sem, vmem_limit_bytes=VMEM_LIMIT)


def _mm(a, b):
    return jnp.dot(a, b, preferred_element_type=F32)


def _mm_nt(a, b):
    return lax.dot_general(a, b, (((1,), (1,)), ((), ())), preferred_element_type=F32)


def _hi_lo(x):
    hi = x.astype(BF16)
    return hi, (x - hi.astype(F32)).astype(BF16)


def _rms(x, g):
    return x * lax.rsqrt(jnp.mean(x * x, axis=-1, keepdims=True) + RMS_EPS) * g


def _sigmoid(x):
    return 1.0 / (1.0 + jnp.exp(-x))


def _silu(x):
    return x * _sigmoid(x)


def _softplus(x):
    return jnp.maximum(x, 0.0) + jnp.log(1.0 + jnp.exp(-jnp.abs(x)))


def _inproj_rglru_kernel(x_ref, g_ref, *refs, nslab):
    w_refs = refs[:nslab]
    (cw_ref, cb_ref, wa_ref, ba_ref, wx_ref, bx_ref, lam_ref,
     ya_ref, zb_ref, xbuf, hcar) = refs[nslab:]
    ts, width = ya_ref.shape
    bd = width // RG_BLOCKS

    @pl.when(pl.program_id(1) == 0)
    def _():
        xbuf[0:SUBLANES, :] = jnp.zeros((SUBLANES, width), F32)
        hcar[...] = jnp.zeros_like(hcar)

    xn = _rms(x_ref[...], g_ref[...]).astype(BF16)

    def project_rest():
        piece = width // 4
        for k in range(2, nslab):
            for q in range(4):
                c0 = (k - 2) * width + q * piece
                zb_ref[:, c0:c0 + piece] = _mm(xn, w_refs[k][:, q * piece:(q + 1) * piece])
                yield

    def rglru():
        xa = _mm(xn, w_refs[0][...])
        ga = _mm(xn, w_refs[1][...])
        xbuf[SUBLANES:SUBLANES + ts, :] = xa
        yield
        sp = _softplus(-lam_ref[...])
        h = hcar[0:1, :]
        ngrp = RG_ROWS // SUBLANES
        for c in range(ts // RG_ROWS):
            r0 = c * RG_ROWS
            rows = slice(r0, r0 + RG_ROWS)
            xc = cb_ref[...] + cw_ref[RG_CONV - 1:RG_CONV, :] * xa[rows]
            for j in range(1, RG_CONV):
                xc = xc + (cw_ref[RG_CONV - 1 - j:RG_CONV - j, :]
                           * xbuf[pl.ds(SUBLANES - j + r0, RG_ROWS), :])
            xcb = xc.astype(BF16)
            gr, gi = [], []
            for n in range(RG_BLOCKS):
                blk = xcb[:, n * bd:(n + 1) * bd]
                gr.append(_mm(blk, wa_ref[n]))
                gi.append(_mm(blk, wx_ref[n]))
            gate_r = _sigmoid(jnp.concatenate(gr, axis=-1) + ba_ref[...])
            gate_i = _sigmoid(jnp.concatenate(gi, axis=-1) + bx_ref[...])
            log_a = (-RG_C) * gate_r * sp
            a = jnp.exp(log_a)
            u = jnp.sqrt(-jnp.tanh(log_a) * (1.0 + a * a)) * (gate_i * xc)
            a = a.reshape(ngrp, SUBLANES, width)
            u = u.reshape(ngrp, SUBLANES, width)
            row = lax.broadcasted_iota(jnp.int32, a.shape, 1)
            d = 1
            while d < SUBLANES:
                keep = row >= d
                u = jnp.where(keep, a * pltpu.roll(u, d, axis=1) + u, u)
                a = jnp.where(keep, a * pltpu.roll(a, d, axis=1), a)
                d *= 2
            hs = []
            for j in range(ngrp):
                hb = a[j] * h + u[j]
                hs.append(hb)
                h = hb[SUBLANES - 1:SUBLANES, :]
            ya_ref[rows, :] = (jnp.concatenate(hs, axis=0) * _silu(ga[rows])).astype(ya_ref.dtype)
            yield
        xbuf[0:SUBLANES, :] = xa[ts - SUBLANES:ts, :]
        hcar[0:1, :] = h

    head, rest = rglru(), project_rest()
    next(head)
    nsub = ts // RG_ROWS
    npiece = 4 * (nslab - 2)
    for c in range(nsub):
        for _ in range((c + 1) * npiece // nsub - c * npiece // nsub):
            next(rest)
        next(head)
    for _ in itertools.chain(head, rest):
        pass


def _inproj_rglru(x, g, w_in, conv_w, conv_b, w_a, b_a, w_x, b_x, lam, bsz, seq, ts):
    t, d = x.shape
    width = conv_w.shape[1]
    nslab = w_in.shape[1] // width
    nt = seq // ts
    row = lambda b, i: (b * nt + i, 0)

    def const(shape):
        return pl.BlockSpec(shape, lambda b, i: (0,) * len(shape), pipeline_mode=pl.Buffered(1))

    def slab(k):
        return pl.BlockSpec((d, width), lambda b, i: (0, k), pipeline_mode=pl.Buffered(1))

    return pl.pallas_call(
        functools.partial(_inproj_rglru_kernel, nslab=nslab),
        grid=(bsz, nt),
        in_specs=[pl.BlockSpec((ts, d), row), const((1, d))] + [slab(k) for k in range(nslab)]
        + [const(conv_w.shape), const((1, width)), const(w_a.shape), const((1, width)),
           const(w_x.shape), const((1, width)), const((1, width))],
        out_specs=[pl.BlockSpec((ts, width), row), pl.BlockSpec((ts, (nslab - 2) * width), row)],
        out_shape=[jax.ShapeDtypeStruct((t, width), BF16),
                   jax.ShapeDtypeStruct((t, (nslab - 2) * width), F32)],
        scratch_shapes=[pltpu.VMEM((ts + SUBLANES, width), F32),
                        pltpu.VMEM((SUBLANES, width), F32)],
        compiler_params=_cparams("parallel", "arbitrary"),
        name="l0_inproj_rglru",
    )(x, g, *([w_in] * nslab), conv_w, conv_b, w_a, b_a, w_x, b_x, lam)


def _hgrn2_kernel(q_ref, f_ref, v_ref, g_ref, lbl_ref, gn_ref, o_ref, st):
    tc = q_ref.shape[0]

    @pl.when(pl.program_id(2) == 0)
    def _():
        st[...] = jnp.zeros_like(st)

    ri = lax.broadcasted_iota(jnp.int32, (CHUNK, CHUNK), 0)
    ci = lax.broadcasted_iota(jnp.int32, (CHUNK, CHUNK), 1)
    causal = ri >= ci
    tril = jnp.where(causal, 1.0, 0.0).astype(BF16)
    tril2 = jnp.concatenate([tril, tril], axis=1)
    C = range(tc // CHUNK)
    sl = [slice(n * CHUNK, (n + 1) * CHUNK) for n in C]

    for h in range(st.shape[0]):
        hc = slice(h * HG_HEAD_DIM, (h + 1) * HG_HEAD_DIM)
        lbl = lbl_ref[:, hc]
        e = jnp.exp(lbl - jnp.max(lbl, axis=0, keepdims=True))
        lb = e[0:1, :] / jnp.sum(e, axis=0, keepdims=True)

        sg = _sigmoid(f_ref[:, hc])
        log_f = jnp.log(lb + (1.0 - lb) * sg)
        k_all = (1.0 - lb) * (1.0 - sg)
        lf_hi, lf_lo = _hi_lo(jnp.concatenate([log_f[s] for s in sl], axis=1))
        cum_w = _mm(tril2, jnp.concatenate([lf_hi, lf_lo], axis=0))
        cum = [cum_w[:, n * HG_HEAD_DIM:(n + 1) * HG_HEAD_DIM] for n in C]
        dec = [jnp.exp(cum[n][CHUNK - 1:CHUNK, :]) for n in C]
        k = [k_all[s] for s in sl]
        v = [v_ref[s, hc] for s in sl]
        q_dec = [(q_ref[sl[n], hc] * jnp.exp(cum[n])).astype(BF16) for n in C]
        k_inv = [k[n] * jnp.exp(-cum[n]) for n in C]
        k_end = [(k_inv[n] * dec[n]).astype(BF16) for n in C]
        k_inv = [t.astype(BF16) for t in k_inv]
        scores = [jnp.where(causal, _mm_nt(q_dec[n], k_inv[n]), 0.0).astype(BF16) for n in C]
        upd = [_mm(v[n].T.astype(BF16), k_end[n]) for n in C]
        o = [_mm(scores[n], v[n].astype(BF16)) for n in C]
        state = st[h]
        states = []
        for n in C:
            states.append(state.astype(BF16))
            state = state * dec[n] + upd[n]
        st[h] = state
        o = jnp.concatenate([o[n] + _mm_nt(q_dec[n], states[n]) for n in C], axis=0)
        o = o * lax.rsqrt(jnp.mean(o * o, axis=-1, keepdims=True) + RMS_EPS) * gn_ref[...]
        o_ref[:, hc] = (o * _silu(g_ref[:, hc])).astype(o_ref.dtype)


def _hgrn2(z, lb_logits, hg_g, bsz, seq, tc, col0, hps):
    width = lb_logits.shape[1]
    wb = hps * HG_HEAD_DIM
    nt = seq // tc

    def col(k):
        return lambda b, h, i: (b * nt + i, (col0 + k * width) // wb + h)

    return pl.pallas_call(
        _hgrn2_kernel,
        grid=(bsz, width // wb, nt),
        in_specs=[pl.BlockSpec((tc, wb), col(0)),
                  pl.BlockSpec((tc, wb), col(1)),
                  pl.BlockSpec((tc, wb), col(2)),
                  pl.BlockSpec((tc, wb), col(3)),
                  pl.BlockSpec((lb_logits.shape[0], wb), lambda b, h, i: (0, h)),
                  pl.BlockSpec((1, HG_HEAD_DIM), lambda b, h, i: (0, 0))],
        out_specs=pl.BlockSpec((tc, wb), lambda b, h, i: (b * nt + i, h)),
        out_shape=jax.ShapeDtypeStruct((bsz * seq, width), BF16),
        scratch_shapes=[pltpu.VMEM((hps, HG_HEAD_DIM, HG_HEAD_DIM), F32)],
        compiler_params=_cparams("parallel", "parallel", "arbitrary"),
        name="l0_hgrn2",
    )(z, z, z, z, lb_logits, hg_g)


def _outproj0_kernel(ya_ref, yb_ref, wa_ref, wb_ref, x_ref, h_ref):
    h_ref[...] = x_ref[...] + _mm(ya_ref[...], wa_ref[...]) + _mm(yb_ref[...], wb_ref[...])


def _outproj0(ya, yb, w_a, w_b, x, tm):
    t, d = x.shape
    row = lambda i: (i, 0)
    full = lambda i: (0, 0)
    return pl.pallas_call(
        _outproj0_kernel,
        grid=(t // tm,),
        in_specs=[pl.BlockSpec((tm, ya.shape[1]), row),
                  pl.BlockSpec((tm, yb.shape[1]), row),
                  pl.BlockSpec(w_a.shape, full),
                  pl.BlockSpec(w_b.shape, full),
                  pl.BlockSpec((tm, d), row)],
        out_specs=pl.BlockSpec((tm, d), row),
        out_shape=jax.ShapeDtypeStruct((t, d), F32),
        compiler_params=_cparams("parallel"),
        name="l0_outproj",
    )(ya, yb, w_a, w_b, x)


def _shifted(u, prev_row, first_of_seq):
    prev_last = jnp.where(first_of_seq, 0.0, prev_row)
    row = lax.broadcasted_iota(jnp.int32, u.shape, 0)
    return jnp.where(row == 0, prev_last, pltpu.roll(u, 1, axis=0))


def _pair_ones():
    r = lax.broadcasted_iota(jnp.int32, (2 * LANES, LANES), 0) & (LANES - 1)
    c = lax.broadcasted_iota(jnp.int32, (2 * LANES, LANES), 1)
    return jnp.where((r < RW_HEAD_DIM) == (c < RW_HEAD_DIM), 1.0, 0.0).astype(BF16)


def _headsum(x, ones2, split=True):
    n, nblk = x.shape[0], x.shape[1] // LANES
    xs = jnp.concatenate([x[:, i * LANES:(i + 1) * LANES] for i in range(nblk)], axis=0)
    if split:
        s = _mm(jnp.concatenate(_hi_lo(xs), axis=1), ones2)
    else:
        s = _mm(xs.astype(BF16), ones2[0:LANES])
    return jnp.concatenate([s[i * n:(i + 1) * n] for i in range(nblk)], axis=1)


def _front_kernel(h_ref, hp_ref, gn_ref, mu_ref, *refs, steps_per_seq, groups):
    wr_ref, wk_ref, wv_ref, wg_ref = (refs[i * groups:(i + 1) * groups] for i in range(4))
    (w1_ref, w2_ref, w0_ref, a1_ref, a2_ref, a0_ref, kk_ref, ka_ref, rk_ref,
     rw_o, kkw_o, kt_o, bt_o, kend_o, bend_o, vb_o, bon_o, gate_o, dec_o) = refs[4 * groups:]
    tm = h_ref.shape[0]
    L = CHUNK
    nch = tm // L
    u = _rms(h_ref[...], gn_ref[...])
    u_prev = _rms(hp_ref[SUBLANES - 1:SUBLANES, :], gn_ref[...])
    delta = _shifted(u, u_prev, pl.program_id(0) % steps_per_seq == 0) - u
    x_r, x_w, x_k, x_v, x_a, x_g = ((u + delta * mu_ref[i:i + 1, :]).astype(BF16)
                                    for i in range(6))
    zw = jnp.tanh(_mm(x_w, w1_ref[...])).astype(BF16)
    za = _mm(x_a, a1_ref[...]).astype(BF16)
    ones2 = _pair_ones()
    ri = lax.broadcasted_iota(jnp.int32, (L, L), 0)
    ci = lax.broadcasted_iota(jnp.int32, (L, L), 1)
    tril = jnp.where(ri >= ci, 1.0, 0.0).astype(BF16)
    tril2 = jnp.concatenate([tril, tril], axis=1)

    def project(g, out):
        cols = slice(g * FRONT_COLS, (g + 1) * FRONT_COLS)
        out.append(_mm(x_r, wr_ref[g][...]))
        yield
        out.append(_mm(x_k, wk_ref[g][...]))
        yield
        out.append(_mm(x_v, wv_ref[g][...]))
        out.append(_mm(zw, w2_ref[:, cols]))
        out.append(_mm(za, a2_ref[:, cols]))
        yield
        gate_o[:, cols] = _silu(_mm(x_g, wg_ref[g][...]))
        yield

    def per_token(g, proj):
        cols = slice(g * FRONT_COLS, (g + 1) * FRONT_COLS)
        r_all, kraw_all, v_all, zw_all, za_all = proj
        for n in range(nch):
            rows = slice(n * L, (n + 1) * L)
            r, kraw, v = r_all[rows], kraw_all[rows], v_all[rows]
            lw = -DECAY_SCALE * _sigmoid(w0_ref[:, cols] + zw_all[rows])
            a = _sigmoid(a0_ref[:, cols] + za_all[rows])
            kkp = kraw * kk_ref[:, cols]
            kk = kkp * lax.rsqrt(jnp.maximum(_headsum(kkp * kkp, ones2, split=False), 1e-24))
            k = kraw * (1.0 + (a - 1.0) * ka_ref[:, cols])
            b = kk * a
            bon_o[rows, cols] = _headsum(r * k * rk_ref[:, cols], ones2, split=False) * v
            cum = _mm(tril2, jnp.concatenate(_hi_lo(lw), axis=0))
            dec = jnp.exp(cum[L - 1:L, :])
            dec_o[n, :, cols] = dec
            w_out = jnp.exp(-cum)
            w_end = w_out * dec
            rw_o[rows, cols] = (r * jnp.exp(cum)).astype(BF16)
            kkw_o[rows, cols] = (kk * jnp.exp(cum - lw)).astype(BF16)
            kt_o[rows, cols] = (k * w_out).astype(BF16)
            bt_o[rows, cols] = (b * w_out).astype(BF16)
            kend_o[rows, cols] = (k * w_end).astype(BF16)
            bend_o[rows, cols] = (b * w_end).astype(BF16)
            vb_o[rows, cols] = v.astype(BF16)
            yield

    proj = []
    for _ in project(0, proj):
        pass
    for g in range(groups):
        nxt = []
        ahead = project(g + 1, nxt) if g + 1 < groups else iter(())
        for _ in itertools.zip_longest(ahead, per_token(g, proj)):
            pass
        proj = nxt


def _front(h, norm_g, mu, w_r, w_k, w_v, w_g, w1, w2, w0, a1, a2, a0, k_k, k_a, r_k, seq, tm):
    t, d = h.shape
    n = w_r.shape[1]
    groups = n // FRONT_COLS
    rb = tm // SUBLANES
    row = lambda i: (i, 0)

    def const(shape):
        return pl.BlockSpec(shape, lambda i: (0,) * len(shape), pipeline_mode=pl.Buffered(1))

    def colgroup(g):
        return pl.BlockSpec((d, FRONT_COLS), lambda i: (0, g), pipeline_mode=pl.Buffered(1))

    tok = pl.BlockSpec((tm, n), row)
    tok16 = jax.ShapeDtypeStruct((t, n), BF16)
    tok32 = jax.ShapeDtypeStruct((t, n), F32)
    wspecs = [colgroup(g) for g in range(groups)]
    return pl.pallas_call(
        functools.partial(_front_kernel, steps_per_seq=seq // tm, groups=groups),
        grid=(t // tm,),
        in_specs=[pl.BlockSpec((tm, d), row),
                  pl.BlockSpec((SUBLANES, d), lambda i: (jnp.maximum(i * rb - 1, 0), 0)),
                  const((1, d)), const(mu.shape)] + wspecs * 4
        + [const(w1.shape), const(w2.shape), const((1, n)),
           const(a1.shape), const(a2.shape), const((1, n)),
           const((1, n)), const((1, n)), const((1, n))],
        out_specs=[tok] * 9 + [pl.BlockSpec((tm // CHUNK, 1, n), lambda i: (i, 0, 0))],
        out_shape=[tok16] * 7 + [tok32] * 2 + [jax.ShapeDtypeStruct((t // CHUNK, 1, n), F32)],
        compiler_params=_cparams("parallel"),
        name="l1_front",
    )(h, h, norm_g, mu, *([w_r] * groups + [w_k] * groups + [w_v] * groups + [w_g] * groups),
      w1, w2, w0, a1, a2, a0, k_k, k_a, r_k)


def _rwkv_kernel(rw_ref, kkw_ref, kt_ref, bt_ref, kend_ref, bend_ref, vb_ref, bon_ref, g_ref,
                 dec_ref, lg_ref, lb_ref, o_ref, st, rt_s, y0_s, q_s, n_s, *, pairs):
    tc = rw_ref.shape[0]
    L = CHUNK
    hd = RW_HEAD_DIM
    nchunks = tc // L

    @pl.when(pl.program_id(2) == 0)
    def _():
        st[...] = jnp.zeros_like(st)

    row = lax.broadcasted_iota(jnp.int32, (L, LANES), 0)
    lane = lax.broadcasted_iota(jnp.int32, (L, LANES), 1)
    lane_h = lane & (hd - 1)
    strict = lane_h < row
    incl = lane_h <= row
    eye = jnp.where(lane_h == row, 1.0, 0.0).astype(F32)
    head_a = lane < hd
    row2 = lax.broadcasted_iota(jnp.int32, (2 * L, LANES), 0)
    lane2 = lax.broadcasted_iota(jnp.int32, (2 * L, LANES), 1)
    same_head = (row2 < hd) == (lane2 < hd)
    ones2 = _pair_ones()
    pcols = [slice(p * LANES, (p + 1) * LANES) for p in range(pairs)]

    def split(x):
        x = x.astype(F32)
        return jnp.concatenate([jnp.where(head_a, x, 0.0), jnp.where(head_a, 0.0, x)],
                               axis=0).astype(BF16)

    def build(g):
        jobs = [(c, p, slice(c * L, (c + 1) * L), pcols[p])
                for c in range(g * BUILD_CHUNKS, (g + 1) * BUILD_CHUNKS) for p in range(pairs)]
        J = range(len(jobs))
        ld = lambda ref: [ref[rows, cols] for (_, _, rows, cols) in jobs]
        rw, kkw, vb, bend = ld(rw_ref), ld(kkw_ref), ld(vb_ref), ld(bend_ref)
        zkb = [jnp.concatenate([split(tk), split(tb)], axis=0)
               for tk, tb in zip(ld(kt_ref), ld(bt_ref))]
        lhs = [jnp.concatenate([kkw[j], rw[j]], axis=0) for j in J]
        p_kb = [_mm_nt(lhs[j], zkb[j]) for j in J]
        p_k = [t[:, 0:LANES] for t in p_kb]
        p_b = [t[:, LANES:] for t in p_kb]
        yield
        amat = [jnp.where(strict, p_b[j][0:L], 0.0) for j in J]
        x = [eye - amat[j] for j in J]
        m = [_mm(amat[j].astype(BF16), split(amat[j])) for j in J]
        yield
        for _ in range(L.bit_length() - 3):
            xm = [_mm(jnp.concatenate([x[j], m[j]], axis=0).astype(BF16), split(m[j])) for j in J]
            x = [x[j] + xm[j][0:L] for j in J]
            m = [xm[j][L:2 * L] for j in J]
            yield
        x = [(x[j] + _mm(x[j].astype(BF16), split(m[j]))).astype(BF16) for j in J]
        yield
        a_kr = [jnp.concatenate([jnp.where(strict, p_k[j][0:L], 0.0),
                                 jnp.where(incl, p_k[j][L:2 * L], 0.0)], axis=0).astype(BF16)
                for j in J]
        akv = [_mm(a_kr[j], split(vb[j])) for j in J]
        yield
        ku = [_mm(x[j], jnp.concatenate([split(kkw[j]), split(akv[j][0:L])], axis=1)) for j in J]
        yield
        a_rb = [jnp.where(incl, p_b[j][L:2 * L], 0.0).astype(BF16) for j in J]
        ru = [_mm(a_rb[j], jnp.concatenate([split(ku[j][:, 0:LANES]), split(ku[j][:, LANES:])],
                                           axis=1)) for j in J]
        for j, (_, _, rows, cols) in enumerate(jobs):
            rt_s[rows, cols] = (rw[j].astype(F32) - ru[j][:, 0:LANES]).astype(BF16)
            y0_s[rows, cols] = akv[j][L:2 * L] - ru[j][:, LANES:]
        yield
        vu_t = [jnp.concatenate([vb[j].astype(F32), -ku[j][:, LANES:]], axis=0).T.astype(BF16)
                for j in J]
        kt_t = [ku[j][:, 0:LANES].T.astype(BF16) for j in J]
        kb_end = [jnp.concatenate([t, bend[j]], axis=0) for j, t in enumerate(ld(kend_ref))]
        nmat = [_mm(vu_t[j], kb_end[j]) for j in J]
        yield
        qmat = [_mm(kt_t[j], bend[j]) for j in J]
        for j, (c, p, _, _) in enumerate(jobs):
            n_s[c, p] = jnp.where(same_head, nmat[j], 0.0)
            q_s[c, p] = jnp.where(same_head, qmat[j], 0.0).astype(BF16)
        yield

    def scan_step(c):
        rows = slice(c * L, (c + 1) * L)
        dec = dec_ref[c]
        for p in range(pairs):
            s = st[p]
            sb = s.astype(BF16)
            y0_s[rows, pcols[p]] = y0_s[rows, pcols[p]] + _mm_nt(rt_s[rows, pcols[p]], sb)
            st[p] = s * dec[0:1, pcols[p]] + n_s[c, p] - _mm(sb, q_s[c, p])

    def finish(rows):
        y = y0_s[rows, :]
        yc = y - _headsum(y, ones2, split=False) * (1.0 / hd)
        var = _headsum(yc * yc, ones2, split=False) * (1.0 / hd)
        yn = yc * lax.rsqrt(var + GN_EPS) * lg_ref[...] + lb_ref[...]
        o_ref[rows, :] = ((yn + bon_ref[rows, :]) * g_ref[rows, :]).astype(o_ref.dtype)

    groups = nchunks // BUILD_CHUNKS
    half = BUILD_CHUNKS * L // 2
    for blk in range(groups + 2):
        scans, fins = [], []
        if 1 <= blk <= groups:
            scans = [functools.partial(scan_step, c)
                     for c in range((blk - 1) * BUILD_CHUNKS, blk * BUILD_CHUNKS)]
        if blk >= 2:
            r0 = (blk - 2) * BUILD_CHUNKS * L
            fins = [functools.partial(finish, slice(r0 + i * half, r0 + (i + 1) * half))
                    for i in range(2)]
        side = [t for pair in itertools.zip_longest(scans, fins) for t in pair if t is not None]
        if blk < groups:
            stride = max(1, BUILD_STAGES // (len(side) + 1))
            n_stages = 0
            for i, _ in enumerate(build(blk)):
                n_stages += 1
                if side and i % stride == stride - 1:
                    side.pop(0)()
            assert n_stages == BUILD_STAGES
        for thunk in side:
            thunk()


def _rwkv(rw, kkw, kt, bt, kend, bend, vb, bon, gate, dec, lnx_g, lnx_b, bsz, seq, tc, pairs):
    t, width = rw.shape
    wb = pairs * LANES
    nt = seq // tc
    nchunks = tc // CHUNK
    tok = pl.BlockSpec((tc, wb), lambda b, p, i: (b * nt + i, p))
    par = pl.BlockSpec((1, wb), lambda b, p, i: (0, p))
    return pl.pallas_call(
        functools.partial(_rwkv_kernel, pairs=pairs),
        grid=(bsz, width // wb, nt),
        in_specs=[tok] * 9 + [pl.BlockSpec((nchunks, 1, wb), lambda b, p, i: (b * nt + i, 0, p)),
                              par, par],
        out_specs=tok,
        out_shape=jax.ShapeDtypeStruct((t, width), BF16),
        scratch_shapes=[pltpu.VMEM((pairs, LANES, LANES), F32),
                        pltpu.VMEM((tc, wb), BF16),
                        pltpu.VMEM((tc, wb), F32),
                        pltpu.VMEM((nchunks, pairs, LANES, LANES), BF16),
                        pltpu.VMEM((nchunks, pairs, LANES, LANES), F32)],
        compiler_params=_cparams("parallel", "parallel", "arbitrary"),
        name="l1_rwkv7",
    )(rw, kkw, kt, bt, kend, bend, vb, bon, gate, dec, lnx_g, lnx_b)


def _outproj1_kernel(y_ref, w_ref, h_ref, g_ref, o_ref):
    h = h_ref[...] + _mm(y_ref[...], w_ref[...])
    o_ref[...] = _rms(h, g_ref[...])


def _outproj1(y, w, h, g, tm):
    t, d = h.shape
    row = lambda i: (i, 0)
    full = lambda i: (0, 0)
    return pl.pallas_call(
        _outproj1_kernel,
        grid=(t // tm,),
        in_specs=[pl.BlockSpec((tm, y.shape[1]), row), pl.BlockSpec(w.shape, full),
                  pl.BlockSpec((tm, d), row), pl.BlockSpec((1, d), full)],
        out_specs=pl.BlockSpec((tm, d), row),
        out_shape=jax.ShapeDtypeStruct((t, d), F32),
        compiler_params=_cparams("parallel"),
        name="l1_outproj_final",
    )(y, w, h, g)


class _Tiles(NamedTuple):
    inproj_rglru: int
    hgrn2: int
    outproj: int
    front: int
    rwkv: int


def _tile(n, pref):
    t = min(n, pref)
    assert n % t == 0, (n, pref)
    return t


def _tiles(tokens, seq):
    return _Tiles(inproj_rglru=_tile(seq, 512), hgrn2=_tile(seq, 2048),
                  outproj=_tile(tokens, 1024), front=_tile(seq, 256), rwkv=_tile(seq, 1024))


def _layer0(x2, bsz, seq, tiles, norm_g, w_in, conv_w, conv_b, w_a, b_a, w_x, b_x, lam,
            lb_logits, hg_g, w_out):
    width = conv_w.shape[1]
    row = lambda p: p.reshape(1, -1)
    ya, zb = _inproj_rglru(x2, row(norm_g), w_in.astype(BF16), conv_w, row(conv_b),
                           w_a.astype(BF16), row(b_a), w_x.astype(BF16), row(b_x), row(lam),
                           bsz, seq, tiles.inproj_rglru)
    yb = _hgrn2(zb, lb_logits, row(hg_g), bsz, seq, tiles.hgrn2, 0, HG_HEADS_PER_STEP)
    w_out = w_out.astype(BF16)
    return _outproj0(ya, yb, w_out[:width], w_out[width:], x2, tiles.outproj)


def _layer1(h, bsz, seq, tiles, norm_g, mu, w_r, w_k, w_v, w_g, w0, w1, w2, a0, a1, a2, k_k, k_a,
            r_k, lnx_g, lnx_b, w_o, final_g):
    row = lambda p: p.reshape(1, -1)
    bf = lambda w: w.astype(BF16)
    ops = _front(h, row(norm_g), mu, bf(w_r), bf(w_k), bf(w_v), bf(w_g), bf(w1), bf(w2), row(w0), bf(a1),
                 bf(a2), row(a0), row(k_k), row(k_a), row(r_k), seq, tiles.front)
    y = _rwkv(*ops, row(lnx_g), row(lnx_b), bsz, seq, tiles.rwkv, RW_PAIRS_PER_STEP)
    return _outproj1(y, bf(w_o), h, row(final_g), tiles.outproj)


def kernel(x, ab_norm_g, ab_w_in, rg_conv_w, rg_conv_b, rg_w_a, rg_b_a, rg_w_x, rg_b_x, rg_lambda, hg_lb_logits, hg_norm_g, ab_w_out, c_norm_g, c_mu, c_w_r, c_w_k, c_w_v, c_w_g, c_w0, c_w1, c_w2, c_a0, c_a1, c_a2, c_k_k, c_k_a, c_r_k, c_lnx_g, c_lnx_b, c_w_o, final_g):
    bsz, seq, d = x.shape
    assert ab_norm_g.shape[0] == 1 and c_norm_g.shape[0] == 1, "two-layer trunk only"
    x2 = x.reshape(bsz * seq, d)
    tiles = _tiles(bsz * seq, seq)
    h = _layer0(x2, bsz, seq, tiles, ab_norm_g[0], ab_w_in[0], rg_conv_w[0], rg_conv_b[0],
                rg_w_a[0], rg_b_a[0], rg_w_x[0], rg_b_x[0], rg_lambda[0], hg_lb_logits,
                hg_norm_g[0], ab_w_out[0])
    out = _layer1(h, bsz, seq, tiles, c_norm_g[0], c_mu[0], c_w_r[0], c_w_k[0], c_w_v[0], c_w_g[0],
                  c_w0[0], c_w1[0], c_w2[0], c_a0[0], c_a1[0], c_a2[0], c_k_k[0], c_k_a[0],
                  c_r_k[0].reshape(-1), c_lnx_g[0], c_lnx_b[0], c_w_o[0], final_g)
    return out.reshape(bsz, seq, d)
```
